```python
import math
import jax
import jax.numpy as jnp
from jax import lax
import numpy as np

D_MODEL = 1024
BATCH = 8
SEQ = 2048
DEPTH = 4

GRID_W = 64
CTX_LEN = 256
N_EVEN = (DEPTH + 1) // 2
N_ODD = DEPTH // 2

HGRN_HEADS = 4
HGRN_HEAD_DIM = 128
HGRN_WIDTH = HGRN_HEADS * HGRN_HEAD_DIM

MLA_HEADS = 8
MLA_Q_RANK = 384
MLA_KV_RANK = 256
MLA_NOPE = 64
MLA_ROPE = 32
MLA_V = 64

RET_HEADS = 4
RET_QK = 64
RET_V = 128

DIFF_HEADS = 4
DIFF_QK = 64
DIFF_V = 128

FFN_HIDDEN = -(-8 * D_MODEL // (3 * 256)) * 256

EVEN_SIZES = (HGRN_WIDTH,) * 5 + (MLA_Q_RANK, MLA_KV_RANK, MLA_ROPE)
ODD_SIZES = (RET_HEADS * RET_QK, RET_HEADS * RET_QK, RET_HEADS * RET_V, RET_HEADS * RET_V,
             DIFF_HEADS * 2 * DIFF_QK, DIFF_HEADS * 2 * DIFF_QK, DIFF_HEADS * DIFF_V)
EVEN_IN = sum(EVEN_SIZES)
ODD_IN = sum(ODD_SIZES)
MIX_WIDTH = HGRN_WIDTH + MLA_HEADS * MLA_V

CHUNK = 64
Q_BLOCK = 128
ROPE_BASE = 10000.0
EPS = 1e-6

kernel_name = 'hybrid_prefix_diffusion_trunk'


def rms_norm(x, g=None):
    xf = x.astype(jnp.float32)
    y = xf * lax.rsqrt(jnp.mean(xf * xf, axis=-1, keepdims=True) + EPS)
    if g is not None:
        y = y * g.astype(jnp.float32)
    return y.astype(x.dtype)


def split_cols(a, sizes):
    idx = [int(i) for i in np.cumsum(sizes)[:-1]]
    return jnp.split(a, idx, axis=-1)


def rope_1d(x, pos):
    half = x.shape[-1] // 2
    inv = ROPE_BASE ** (-jnp.arange(half, dtype=jnp.float32) / half)
    ang = pos.astype(jnp.float32)[:, None] * inv[None, :]
    shape = (1, x.shape[1]) + (1,) * (x.ndim - 3) + (half,)
    cos = jnp.cos(ang).reshape(shape).astype(x.dtype)
    sin = jnp.sin(ang).reshape(shape).astype(x.dtype)
    x1, x2 = x[..., :half], x[..., half:]
    return jnp.concatenate([x1 * cos - x2 * sin, x1 * sin + x2 * cos], axis=-1)


def rope_2d(x):
    n = x.shape[1]
    rows = n // GRID_W
    row = jnp.repeat(jnp.arange(rows), GRID_W)
    col = jnp.arange(rows * GRID_W) % GRID_W
    half = x.shape[-1] // 2
    return jnp.concatenate([rope_1d(x[..., :half], row), rope_1d(x[..., half:], col)], axis=-1)


def attend(q, k, v, scale):
    s = jnp.einsum('bqhd,bkhd->bhqk', q, k).astype(jnp.float32) * scale
    p = jax.nn.softmax(s, axis=-1).astype(v.dtype)
    return jnp.einsum('bhqk,bkhv->bqhv', p, v)


def diff_attend(q, k, v, scale, lam):
    s = jnp.einsum('bqhmd,bkhmd->bhmqk', q, k).astype(jnp.float32) * scale
    p = jax.nn.softmax(s, axis=-1)
    a = (p[:, :, 0] - lam * p[:, :, 1]).astype(v.dtype)
    return jnp.einsum('bhqk,bkhv->bqhv', a, v)


def sweep_query_blocks(fn, q):
    b, n = q.shape[:2]
    nb = n // Q_BLOCK
    qb = jnp.moveaxis(q.reshape((b, nb, Q_BLOCK) + q.shape[2:]), 1, 0)
    out = lax.map(fn, qb)
    return jnp.moveaxis(out, 0, 1).reshape((b, n) + out.shape[3:])


def gla_scan(q, k, v, log_f, s0):
    b, n, h, _ = q.shape
    dv = v.shape[-1]
    nc = n // CHUNK

    def chunks(a):
        return a.astype(jnp.float32).reshape(b, nc, CHUNK, h, a.shape[-1]).transpose(1, 0, 3, 2, 4)

    causal = jnp.tril(jnp.ones((CHUNK, CHUNK), dtype=bool))[:, :, None]

    def step(S, inp):
        qc, kc, vc, gc = inp
        bcum = jnp.cumsum(gc, axis=2)
        diff = bcum[:, :, :, None, :] - bcum[:, :, None, :, :]
        dec = jnp.where(causal, jnp.exp(jnp.where(causal, diff, 0.0)), 0.0)
        att = jnp.einsum('bhtd,bhsd,bhtsd->bhts', qc, kc, dec)
        o = (jnp.einsum('bhts,bhsv->bhtv', att, vc)
             + jnp.einsum('bhtd,bhdv->bhtv', qc * jnp.exp(bcum), S))
        b_last = bcum[:, :, -1:, :]
        S = (jnp.exp(b_last[:, :, 0, :, None]) * S
             + jnp.einsum('bhsd,bhsv->bhdv', kc * jnp.exp(b_last - bcum), vc))
        return S, o

    S, o = lax.scan(step, s0, (chunks(q), chunks(k), chunks(v), chunks(log_f)))
    o = o.transpose(1, 0, 3, 2, 4).reshape(b, n, h, dv)
    return o.astype(v.dtype), S


def retention_scan(q, k, v, log_gamma, s0):
    b, n, h, _ = q.shape
    dv = v.shape[-1]
    nc = n // CHUNK
    lg = log_gamma.astype(jnp.float32)
    t = jnp.arange(CHUNK, dtype=jnp.float32)
    rel = t[:, None] - t[None, :]
    dmat = jnp.where(rel >= 0, jnp.exp(jnp.maximum(rel, 0.0)[None] * lg[:, None, None]), 0.0)
    q_dec = jnp.exp((t + 1.0)[None, :] * lg[:, None])
    k_dec = jnp.exp((CHUNK - 1.0 - t)[None, :] * lg[:, None])
    c_dec = jnp.exp(CHUNK * lg)

    def chunks(a):
        return a.astype(jnp.float32).reshape(b, nc, CHUNK, h, a.shape[-1]).transpose(1, 0, 3, 2, 4)

    def step(S, inp):
        qc, kc, vc = inp
        att = jnp.einsum('bhtd,bhsd->bhts', qc, kc) * dmat
        o = (jnp.einsum('bhts,bhsv->bhtv', att, vc)
             + jnp.einsum('bhtd,bhdv->bhtv', qc * q_dec[:, :, None], S))
        S = c_dec[:, None, None] * S + jnp.einsum('bhsd,bhsv->bhdv', kc * k_dec[:, :, None], vc)
        return S, o

    S, o = lax.scan(step, s0, (chunks(q), chunks(k), chunks(v)))
    o = o.transpose(1, 0, 3, 2, 4).reshape(b, n, h, dv)
    return o.astype(v.dtype), S


def run_prefix_scan(scan_fn, ctx_seqs, lat_seqs, extra, s0, reverse):
    flip = (lambda a: jnp.flip(a, axis=1)) if reverse else (lambda a: a)
    o_ctx, s_ctx = scan_fn(*[flip(a) for a in ctx_seqs], *extra, s0)
    o_lat, _ = scan_fn(*[flip(a) for a in lat_seqs], *extra, s_ctx)
    return flip(o_ctx), flip(o_lat)


def hgrn2_group(z_lat, z_ctx, lb, norm_g, need_ctx):
    scale = HGRN_HEAD_DIM ** -0.5

    def heads(a):
        return a.reshape(a.shape[:2] + (HGRN_HEADS, HGRN_HEAD_DIM))

    def seqs(z, d):
        q, i, f_fwd, f_bwd, _ = z
        zf = (f_fwd, f_bwd)[d].astype(jnp.float32)
        lb_d = lb[d]
        k = (1.0 - lb_d) * jax.nn.sigmoid(-zf)
        log_f = jax.nn.log_sigmoid(zf) + jnp.log1p(lb_d * jnp.exp(-zf))
        return (heads(q) * scale, heads(k), heads(i), heads(log_f))

    b = z_lat[0].shape[0]
    s0 = jnp.zeros((b, HGRN_HEADS, HGRN_HEAD_DIM, HGRN_HEAD_DIM), jnp.float32)
    fwd = run_prefix_scan(gla_scan, seqs(z_ctx, 0), seqs(z_lat, 0), (), s0, False)
    bwd = run_prefix_scan(gla_scan, seqs(z_ctx, 1), seqs(z_lat, 1), (), s0, True)

    def readout(o, g):
        return (rms_norm(o, norm_g) * jax.nn.silu(heads(g))).reshape(g.shape)

    out_lat = readout(fwd[1] + bwd[1], z_lat[4])
    out_ctx = readout(fwd[0] + bwd[0], z_ctx[4]) if need_ctx else None
    return out_lat, out_ctx


def mla_group(p_lat, p_ctx, q_norm_g, w_uq, kv_norm_g, w_ukv, need_ctx):
    scale = (MLA_NOPE + MLA_ROPE) ** -0.5

    def qkv(p, rotate):
        c_q, c_kv, k_rope = p
        b, n = c_q.shape[:2]
        q = (rms_norm(c_q, q_norm_g) @ w_uq).reshape(b, n, MLA_HEADS, MLA_NOPE + MLA_ROPE)
        kv = (rms_norm(c_kv, kv_norm_g) @ w_ukv).reshape(b, n, MLA_HEADS, MLA_NOPE + MLA_V)
        q_nope, q_rope = q[..., :MLA_NOPE], q[..., MLA_NOPE:]
        k_nope, v = kv[..., :MLA_NOPE], kv[..., MLA_NOPE:]
        k_rope = k_rope[:, :, None, :]
        if rotate:
            q_rope, k_rope = rope_2d(q_rope), rope_2d(k_rope)
        q = jnp.concatenate([q_nope, q_rope], axis=-1)
        k = jnp.concatenate([k_nope, jnp.broadcast_to(k_rope, (b, n, MLA_HEADS, MLA_ROPE))], axis=-1)
        return q, k, v

    q_l, k_l, v_l = qkv(p_lat, True)
    q_c, k_c, v_c = qkv(p_ctx, False)
    k_all = jnp.concatenate([k_c, k_l], axis=1)
    v_all = jnp.concatenate([v_c, v_l], axis=1)
    o_lat = sweep_query_blocks(lambda qb: attend(qb, k_all, v_all, scale), q_l)
    out_lat = o_lat.reshape(o_lat.shape[:2] + (MLA_HEADS * MLA_V,))
    out_ctx = None
    if need_ctx:
        o_ctx = attend(q_c, k_c, v_c, scale)
        out_ctx = o_ctx.reshape(o_ctx.shape[:2] + (MLA_HEADS * MLA_V,))
    return out_lat, out_ctx


def retention_group(p_lat, p_ctx, decay_logits, need_ctx):
    log_gamma = jax.nn.log_sigmoid(decay_logits.astype(jnp.float32))

    def qkv(p, rotate):
        q, k, v, _ = p
        b, n = q.shape[:2]
        q = q.reshape(b, n, RET_HEADS, RET_QK)
        k = k.reshape(b, n, RET_HEADS, RET_QK) * (RET_QK ** -0.5)
        v = v.reshape(b, n, RET_HEADS, RET_V)
        if rotate:
            pos = jnp.arange(n)
            q, k = rope_1d(q, pos), rope_1d(k, pos)
        return q, k, v

    lat = qkv(p_lat, True)
    ctx = qkv(p_ctx, False)
    b = p_lat[0].shape[0]
    s0 = jnp.zeros((b, RET_HEADS, RET_QK, RET_V), jnp.float32)
    fwd = run_prefix_scan(retention_scan, ctx, lat, (log_gamma[0],), s0, False)
    bwd = run_prefix_scan(retention_scan, ctx, lat, (log_gamma[1],), s0, True)

    def readout(o, g):
        gh = g.reshape(g.shape[:2] + (RET_HEADS, RET_V))
        return (rms_norm(o) * jax.nn.silu(gh)).reshape(g.shape)

    out_lat = readout(fwd[1] + bwd[1], p_lat[3])
    out_ctx = readout(fwd[0] + bwd[0], p_ctx[3]) if need_ctx else None
    return out_lat, out_ctx


def diff_group(p_lat, p_ctx, lam_params, subln_g, layer_idx, need_ctx):
    scale = DIFF_QK ** -0.5
    lambda_init = 0.8 - 0.6 * math.exp(-0.3 * layer_idx)
    lp = lam_params.astype(jnp.float32)
    lam = jnp.exp(jnp.sum(lp[0] * lp[1])) - jnp.exp(jnp.sum(lp[2] * lp[3])) + lambda_init

    def qkv(p, rotate):
        q, k, v = p
        b, n = q.shape[:2]
        q = q.reshape(b, n, DIFF_HEADS, 2, DIFF_QK)
        k = k.reshape(b, n, DIFF_HEADS, 2, DIFF_QK)
        v = v.reshape(b, n, DIFF_HEADS, DIFF_V)
        if rotate:
            q, k = rope_2d(q), rope_2d(k)
        return q, k, v

    q_l, k_l, v_l = qkv(p_lat, True)
    q_c, k_c, v_c = qkv(p_ctx, False)
    k_all = jnp.concatenate([k_c, k_l], axis=1)
    v_all = jnp.concatenate([v_c, v_l], axis=1)

    def readout(o):
        return (rms_norm(o, subln_g) * (1.0 - lambda_init)).reshape(o.shape[:2] + (DIFF_HEADS * DIFF_V,))

    out_lat = readout(sweep_query_blocks(lambda qb: diff_attend(qb, k_all, v_all, scale, lam), q_l))
    out_ctx = readout(diff_attend(q_c, k_c, v_c, scale, lam)) if need_ctx else None
    return out_lat, out_ctx


def even_mixer(h_lat, h_ctx, w_in, w_out, lb, hgrn_norm_g, q_norm_g, w_uq, kv_norm_g, w_ukv, need_ctx):
    p_lat = split_cols(h_lat @ w_in, EVEN_SIZES)
    p_ctx = split_cols(h_ctx @ w_in, EVEN_SIZES)
    a_lat, a_ctx = hgrn2_group(p_lat[:5], p_ctx[:5], lb, hgrn_norm_g, need_ctx)
    b_lat, b_ctx = mla_group(p_lat[5:], p_ctx[5:], q_norm_g, w_uq, kv_norm_g, w_ukv, need_ctx)
    out_lat = jnp.concatenate([a_lat, b_lat], axis=-1) @ w_out
    out_ctx = (jnp.concatenate([a_ctx, b_ctx], axis=-1) @ w_out) if need_ctx else None
    return out_lat, out_ctx


def odd_mixer(h_lat, h_ctx, w_in, w_out, decay_logits, lam_params, subln_g, layer_idx, need_ctx):
    p_lat = split_cols(h_lat @ w_in, ODD_SIZES)
    p_ctx = split_cols(h_ctx @ w_in, ODD_SIZES)
    a_lat, a_ctx = retention_group(p_lat[:4], p_ctx[:4], decay_logits, need_ctx)
    b_lat, b_ctx = diff_group(p_lat[4:], p_ctx[4:], lam_params, subln_g, layer_idx, need_ctx)
    out_lat = jnp.concatenate([a_lat, b_lat], axis=-1) @ w_out
    out_ctx = (jnp.concatenate([a_ctx, b_ctx], axis=-1) @ w_out) if need_ctx else None
    return out_lat, out_ctx


def swiglu(h, w_gate, w_up, w_down):
    return (jax.nn.silu(h @ w_gate) * (h @ w_up)) @ w_down


def setup_inputs(seed: int = 0) -> dict:
    key = jax.random.key(seed)
    ks = jax.random.split(key, 24)
    f32 = jnp.float32

    def nrm(k, shape, fan_in, gain=1.0):
        return gain * (fan_in ** -0.5) * jax.random.normal(k, shape, f32)

    def gain_like(k, shape):
        return 1.0 + 0.05 * jax.random.normal(k, shape, f32)

    ret_base = jnp.log(2.0 ** (5.0 + jnp.arange(RET_HEADS, dtype=f32)) - 1.0)
    return {
        'x': jax.random.normal(ks[0], (BATCH, SEQ, D_MODEL), f32),
        'c': jax.random.normal(ks[1], (BATCH, D_MODEL), f32),
        'ctx': jax.random.normal(ks[2], (BATCH, CTX_LEN, D_MODEL), f32),
        'c_ctx': jax.random.normal(ks[3], (D_MODEL,), f32),
        'ada_w': nrm(ks[4], (DEPTH, D_MODEL, 6 * D_MODEL), D_MODEL, 0.5),
        'ada_b': 0.01 * jax.random.normal(ks[5], (DEPTH, 6 * D_MODEL), f32),
        'norm_g': gain_like(ks[6], (DEPTH, 2, D_MODEL)),
        'mix_w_out': nrm(ks[7], (DEPTH, MIX_WIDTH, D_MODEL), MIX_WIDTH),
        'ffn_w_gate': nrm(ks[8], (DEPTH, D_MODEL, FFN_HIDDEN), D_MODEL),
        'ffn_w_up': nrm(ks[9], (DEPTH, D_MODEL, FFN_HIDDEN), D_MODEL),
        'ffn_w_down': nrm(ks[10], (DEPTH, FFN_HIDDEN, D_MODEL), FFN_HIDDEN),
        'even_w_in': nrm(ks[11], (N_EVEN, D_MODEL, EVEN_IN), D_MODEL),
        'hgrn_lb_logits': 0.1 * jax.random.normal(ks[12], (N_EVEN, 2, HGRN_WIDTH), f32),
        'hgrn_norm_g': gain_like(ks[13], (N_EVEN, HGRN_HEAD_DIM)),
        'mla_q_norm_g': gain_like(ks[14], (N_EVEN, MLA_Q_RANK)),
        'mla_w_uq': nrm(ks[15], (N_EVEN, MLA_Q_RANK, MLA_HEADS * (MLA_NOPE + MLA_ROPE)), MLA_Q_RANK),
        'mla_kv_norm_g': gain_like(ks[16], (N_EVEN, MLA_KV_RANK)),
        'mla_w_ukv': nrm(ks[17], (N_EVEN, MLA_KV_RANK, MLA_HEADS * (MLA_NOPE + MLA_V)), MLA_KV_RANK),
        'odd_w_in': nrm(ks[18], (N_ODD, D_MODEL, ODD_IN), D_MODEL),
        'ret_decay_logits': ret_base[None, None, :] + 0.01 * jax.random.normal(ks[19], (N_ODD, 2, RET_HEADS), f32),
        'diff_lambda': 0.1 * jax.random.normal(ks[20], (N_ODD, 4, DIFF_QK), f32),
        'diff_subln_g': gain_like(ks[21], (N_ODD, DIFF_V)),
        'final_norm_g': gain_like(ks[22], (D_MODEL,)),
    }


def reference(x, c, ctx, c_ctx, ada_w, ada_b, norm_g, mix_w_out, ffn_w_gate, ffn_w_up, ffn_w_down,
              even_w_in, hgrn_lb_logits, hgrn_norm_g, mla_q_norm_g, mla_w_uq, mla_kv_norm_g, mla_w_ukv,
              odd_w_in, ret_decay_logits, diff_lambda, diff_subln_g, final_norm_g):
    lb_soft = jax.nn.softmax(hgrn_lb_logits.astype(jnp.float32), axis=0)
    lbs = jnp.cumsum(lb_soft, axis=0) - lb_soft[0:1]
    s_lat = jax.nn.silu(c)
    s_ctx = jax.nn.silu(c_ctx)
    for layer in range(DEPTH):
        need_ctx = layer < DEPTH - 1
        mod = (s_lat @ ada_w[layer] + ada_b[layer])[:, None, :]
        mod_c = (s_ctx @ ada_w[layer] + ada_b[layer])[None, None, :]
        sh1, sc1, g1, sh2, sc2, g2 = jnp.split(mod, 6, axis=-1)
        csh1, csc1, cg1, csh2, csc2, cg2 = jnp.split(mod_c, 6, axis=-1)
        h_lat = rms_norm(x, norm_g[layer, 0]) * (1.0 + sc1) + sh1
        h_ctx = rms_norm(ctx, norm_g[layer, 0]) * (1.0 + csc1) + csh1
        if layer % 2 == 0:
            e = layer // 2
            u_lat, u_ctx = even_mixer(h_lat, h_ctx, even_w_in[e], mix_w_out[layer], lbs[e], hgrn_norm_g[e],
                                      mla_q_norm_g[e], mla_w_uq[e], mla_kv_norm_g[e], mla_w_ukv[e], need_ctx)
        else:
            o = layer // 2
            u_lat, u_ctx = odd_mixer(h_lat, h_ctx, odd_w_in[o], mix_w_out[layer], ret_decay_logits[o],
                                     diff_lambda[o], diff_subln_g[o], layer, need_ctx)
        x = x + g1 * u_lat
        x = x + g2 * swiglu(rms_norm(x, norm_g[layer, 1]) * (1.0 + sc2) + sh2,
                            ffn_w_gate[layer], ffn_w_up[layer], ffn_w_down[layer])
        if need_ctx:
            ctx = ctx + cg1 * u_ctx
            ctx = ctx + cg2 * swiglu(rms_norm(ctx, norm_g[layer, 1]) * (1.0 + csc2) + csh2,
                                     ffn_w_gate[layer], ffn_w_up[layer], ffn_w_down[layer])
    return rms_norm(x, final_norm_g)
```

```python
import functools
import math

import numpy as np
import jax
import jax.numpy as jnp
from jax import lax
from jax.experimental import pallas as pl
from jax.experimental.pallas import tpu as pltpu

F32 = jnp.float32
BF16 = jnp.bfloat16

LANES = 128
GRID_W = 64
CHUNK = 64
ROPE_BASE = 10000.0
EPS = 1e-6

HGRN_HEADS, HGRN_DIM = 4, 128
HGRN_WIDTH = HGRN_HEADS * HGRN_DIM
MLA_HEADS, MLA_Q_RANK, MLA_KV_RANK, MLA_NOPE, MLA_ROPE, MLA_V = 8, 384, 256, 64, 32, 64
RET_HEADS, RET_QK, RET_V = 4, 64, 128
DIFF_HEADS, DIFF_QK, DIFF_V = 4, 64, 128

EV_CQ, EV_KROPE, EV_G, EV_CKV, EV_Q, EV_I, EV_FF, EV_FB = 0, 3, 4, 8, 10, 14, 18, 22
EV_COLS = 26 * LANES
OD_QR, OD_KR, OD_VR, OD_GR, OD_QD, OD_KD, OD_VD = 0, 2, 4, 8, 12, 16, 20
OD_COLS = 24 * LANES

ROW_TILE = 256
VMEM_LIMIT = 48 * 1024 * 1024


def _params(n_axes, vmem=None):
    return pltpu.CompilerParams(dimension_semantics=("arbitrary",) * n_axes, vmem_limit_bytes=vmem)


def _dot(a, b):
    return jnp.dot(a, b, preferred_element_type=F32)


def _dot_nt(a, b):
    return lax.dot_general(a, b, (((1,), (1,)), ((), ())), preferred_element_type=F32)


def _dot_tn(a, b):
    return lax.dot_general(a, b, (((0,), (0,)), ((), ())), preferred_element_type=F32)


def _rms(x):
    return x * lax.rsqrt(jnp.mean(x * x, axis=-1, keepdims=True) + EPS)


def _silu(x):
    return x * jax.nn.sigmoid(x)


def _log_sigmoid(z):
    return jnp.minimum(z, 0.0) - jnp.log1p(jnp.exp(-jnp.abs(z)))


def _rope(x, c, sa, sb, shift):
    return x * c + pltpu.roll(x, shift, 1) * sa + pltpu.roll(x, LANES - shift, 1) * sb


def _mod_kernel(s_ref, w_ref, b_ref, o_ref):
    s = _silu(s_ref[...])
    o_ref[...] = jnp.dot(s, w_ref[...], precision=lax.Precision.HIGHEST, preferred_element_type=F32) + b_ref[...]


def _modulation(s_rows, ada_w, ada_b):
    depth, d, d6 = ada_w.shape
    rows = s_rows.shape[0]
    tn = d
    return pl.pallas_call(
        _mod_kernel,
        grid=(depth, d6 // tn),
        in_specs=[pl.BlockSpec((rows, d), lambda l, n: (0, 0)),
                  pl.BlockSpec((None, d, tn), lambda l, n: (l, 0, n)),
                  pl.BlockSpec((None, 1, tn), lambda l, n: (l, 0, n))],
        out_specs=pl.BlockSpec((None, rows, tn), lambda l, n: (l, 0, n)),
        out_shape=jax.ShapeDtypeStruct((depth, rows, d6), F32),
        compiler_params=_params(2),
        name="modulation",
    )(s_rows, ada_w, ada_b.reshape(depth, 1, d6))


def _inproj_kernel(x_ref, mod_ref, g_ref, w_ref, o_ref):
    h = _rms(x_ref[...]) * g_ref[0:1, :] * (1.0 + mod_ref[1:2, :]) + mod_ref[0:1, :]
    o_ref[...] = _dot(h.astype(BF16), w_ref[...])


def _mod_spec(layer, batch, n_ctx_tiles, d):
    return pl.BlockSpec((None, None, 6, d), lambda b, t: (layer, jnp.where(t < n_ctx_tiles, batch, b), 0, 0))


def _in_projection(x, mod, norm_g, w, layer, tm, n_ctx_tiles):
    b, t, d = x.shape
    n = w.shape[1]
    return pl.pallas_call(
        _inproj_kernel,
        grid=(b, t // tm),
        in_specs=[pl.BlockSpec((None, tm, d), lambda i, j: (i, j, 0)),
                  _mod_spec(layer, b, n_ctx_tiles, d),
                  pl.BlockSpec((None, 2, d), lambda i, j: (layer, 0, 0)),
                  pl.BlockSpec((d, n), lambda i, j: (0, 0))],
        out_specs=pl.BlockSpec((None, tm, n), lambda i, j: (i, j, 0)),
        out_shape=jax.ShapeDtypeStruct((b, t, n), F32),
        compiler_params=_params(2, VMEM_LIMIT),
        name="in_projection",
    )(x, mod, norm_g, w)


_LEVELS = 6


def _gla_tables():
    c = CHUNK
    m = np.zeros((_LEVELS + 2, c, c), np.float32)
    masks = np.zeros((_LEVELS + 1, c, c), np.float32)
    masks[0] = np.eye(c)
    for lv in range(_LEVELS):
        b = 32 >> lv
        for tau in range(c):
            mid = (tau // (2 * b)) * 2 * b + b
            if tau >= mid:
                m[lv, tau, mid:tau + 1] = 1.0
                masks[lv + 1, tau, mid - b:mid] = 1.0
            else:
                m[lv, tau, tau + 1:mid] = 1.0
    for tau in range(c):
        m[_LEVELS, tau, :tau + 1] = 1.0
        m[_LEVELS + 1, tau, tau + 1:] = 1.0
    flip = lambda a: a[:, ::-1, ::-1]
    mcat = np.stack([m, flip(m)]).reshape(2, (_LEVELS + 2) * c, c)
    mk = np.stack([masks, flip(masks)]).reshape(2, (_LEVELS + 1) * c, c)
    return jnp.asarray(mcat, BF16), jnp.asarray(mk, F32)


def _gla_chunk(q, v, z, lb, mcat, masks, st, reverse):
    c = CHUNK
    qs = q * (HGRN_DIM ** -0.5)
    k = (1.0 - lb) * jax.nn.sigmoid(-z)
    log_f = _log_sigmoid(z) + jnp.log1p(lb * jnp.exp(-z))
    g_hi = log_f.astype(BF16)
    r1 = log_f - g_hi.astype(F32)
    g_mid = r1.astype(BF16)
    g_lo = (r1 - g_mid.astype(F32)).astype(BF16)
    e3 = _dot(mcat, jnp.concatenate([g_hi, g_mid, g_lo], axis=1))
    dec = jnp.exp(e3[:, :LANES] + e3[:, LANES:2 * LANES] + e3[:, 2 * LANES:])
    row = lax.broadcasted_iota(jnp.int32, (c, LANES), 0)
    tau = (c - 1 - row) if reverse else row
    att = masks[0:c] * _dot_nt(qs.astype(BF16), k.astype(BF16))
    for lv in range(_LEVELS):
        b = 32 >> lv
        zl = (jnp.where((tau & b) != 0, qs, k) * dec[lv * c:(lv + 1) * c]).astype(BF16)
        att = att + masks[(lv + 1) * c:(lv + 2) * c] * _dot_nt(zl, zl)
    dec_q = dec[_LEVELS * c:(_LEVELS + 1) * c]
    dec_k = dec[(_LEVELS + 1) * c:(_LEVELS + 2) * c]
    o = _dot(att.astype(BF16), v.astype(BF16)) + _dot_nt((qs * dec_q).astype(BF16), st.astype(BF16))
    last = 0 if reverse else c - 1
    st = st * dec_q[last:last + 1, :] + _dot_tn(v.astype(BF16), (k * dec_k).astype(BF16))
    return o, st


def _hgrn_kernel(qf_ref, if_ref, zf_ref, qb_ref, ib_ref, zb_ref, lbf_ref, lbb_ref, m_ref, mk_ref,
                 of_ref, ob_ref, sf_ref, sb_ref, *, n_chunks):
    @pl.when(pl.program_id(2) == 0)
    def _():
        sf_ref[...] = jnp.zeros_like(sf_ref)
        sb_ref[...] = jnp.zeros_like(sb_ref)

    dirs = ((qf_ref, if_ref, zf_ref, lbf_ref, of_ref, sf_ref), (qb_ref, ib_ref, zb_ref, lbb_ref, ob_ref, sb_ref))
    for d, (q_ref, i_ref, z_ref, lb_ref, o_ref, s_ref) in enumerate(dirs):
        st = s_ref[...]
        for ci in (range(n_chunks) if d == 0 else reversed(range(n_chunks))):
            rows = slice(ci * CHUNK, (ci + 1) * CHUNK)
            o, st = _gla_chunk(q_ref[rows, :], i_ref[rows, :], z_ref[rows, :], lb_ref[...],
                               m_ref[d], mk_ref[d], st, d == 1)
            o_ref[rows, :] = o
        s_ref[...] = st


def _scan_block_maps(n_blocks, n_ctx_blocks):
    def bwd(j):
        return jnp.where(j < n_ctx_blocks, n_ctx_blocks - 1 - j, n_blocks - 1 - (j - n_ctx_blocks))
    return (lambda j: j), bwd


def _hgrn_scan(p, lbs, tb, n_ctx_blocks):
    b, t, _ = p.shape
    nb = t // tb
    fwd, bwd = _scan_block_maps(nb, n_ctx_blocks)
    mcat, masks = _gla_tables()

    def col(base, order):
        return pl.BlockSpec((None, tb, LANES), lambda i, h, j: (i, order(j), base + h))

    def lb_spec(d):
        return pl.BlockSpec((None, 1, LANES), lambda i, h, j: (d, 0, h))

    def out_spec(order):
        return pl.BlockSpec((None, tb, LANES), lambda i, h, j: (i, order(j), h))

    full = lambda a: pl.BlockSpec(a.shape, lambda i, h, j: (0, 0, 0))
    out = jax.ShapeDtypeStruct((b, t, HGRN_WIDTH), F32)
    return pl.pallas_call(
        functools.partial(_hgrn_kernel, n_chunks=tb // CHUNK),
        grid=(b, HGRN_HEADS, nb),
        in_specs=[col(EV_Q, fwd), col(EV_I, fwd), col(EV_FF, fwd), col(EV_Q, bwd), col(EV_I, bwd), col(EV_FB, bwd),
                  lb_spec(0), lb_spec(1), full(mcat), full(masks)],
        out_specs=[out_spec(fwd), out_spec(bwd)],
        out_shape=[out, out],
        scratch_shapes=[pltpu.VMEM((HGRN_DIM, HGRN_DIM), F32), pltpu.VMEM((HGRN_DIM, HGRN_DIM), F32)],
        compiler_params=_params(3),
        name="hgrn_scan",
    )(p, p, p, p, p, p, lbs, lbs, mcat, masks)


def _mla_prep_kernel(cq_ref, ckv_ref, kr_ref, gq_ref, gkv_ref, wq_ref, wkv_ref, tq_ref, tk_ref,
                     q_ref, k_ref, v_ref):
    shift = MLA_ROPE // 4
    q = _dot((_rms(cq_ref[...]) * gq_ref[...]).astype(BF16), wq_ref[...])
    kv = _dot((_rms(ckv_ref[...]) * gkv_ref[...]).astype(BF16), wkv_ref[...])
    k_rope = _rope(kr_ref[...], tk_ref[0], tk_ref[1], tk_ref[2], shift)
    for h in range(MLA_HEADS):
        cols = slice(h * LANES, (h + 1) * LANES)
        q_ref[:, cols] = _rope(q[:, cols], tq_ref[0], tq_ref[1], tq_ref[2], shift).astype(BF16)
        k_ref[:, cols] = (kv[:, cols] + k_rope).astype(BF16)
    v_ref[...] = kv[:, MLA_HEADS * LANES:].astype(BF16)


def _mla_prep(p, gq, gkv, wq, wkv, tab_q, tab_k, tm):
    b, t, _ = p.shape
    hq = MLA_HEADS * LANES
    tab = pl.BlockSpec((3, tm, LANES), lambda i, j: (0, j, 0))
    full = lambda a: pl.BlockSpec(a.shape, lambda i, j: (0,) * a.ndim)
    return pl.pallas_call(
        _mla_prep_kernel,
        grid=(b, t // tm),
        in_specs=[pl.BlockSpec((None, tm, MLA_Q_RANK), lambda i, j: (i, j, EV_CQ * LANES // MLA_Q_RANK)),
                  pl.BlockSpec((None, tm, MLA_KV_RANK), lambda i, j: (i, j, EV_CKV * LANES // MLA_KV_RANK)),
                  pl.BlockSpec((None, tm, LANES), lambda i, j: (i, j, EV_KROPE)),
                  full(gq), full(gkv), full(wq), full(wkv), tab, tab],
        out_specs=[pl.BlockSpec((None, tm, hq), lambda i, j: (i, j, 0)),
                   pl.BlockSpec((None, tm, hq), lambda i, j: (i, j, 0)),
                   pl.BlockSpec((None, tm, MLA_HEADS * MLA_V), lambda i, j: (i, j, 0))],
        out_shape=[jax.ShapeDtypeStruct((b, t, hq), BF16), jax.ShapeDtypeStruct((b, t, hq), BF16),
                   jax.ShapeDtypeStruct((b, t, MLA_HEADS * MLA_V), BF16)],
        compiler_params=_params(2),
        name="mla_prep",
    )(p, p, p, gq, gkv, wq, wkv, tab_q, tab_k)


def _softmax_pv(s, v):
    e = jnp.exp(s - jnp.max(s, axis=-1, keepdims=True))
    return _dot(e.astype(BF16), v) / jnp.sum(e, axis=-1, keepdims=True)


def _mla_attn_kernel(q_ref, k_ref, v_ref, o_ref, *, n_ctx_blocks, ctx_len):
    def run(n_keys):
        outs = []
        for i in range(2):
            cols = slice(i * LANES, (i + 1) * LANES)
            s = _dot_nt(q_ref[:, cols], k_ref[0:n_keys, cols])
            outs.append(_softmax_pv(s, v_ref[0:n_keys, :]))
        lane = lax.broadcasted_iota(jnp.int32, outs[0].shape, 1)
        o_ref[...] = jnp.where(lane < MLA_V, outs[0], outs[1])

    qi = pl.program_id(2)
    pl.when(qi < n_ctx_blocks)(lambda: run(ctx_len))
    pl.when(qi >= n_ctx_blocks)(lambda: run(k_ref.shape[0]))


def _mla_attention(q, k, v, tq, ctx_len):
    b, t, _ = q.shape
    return pl.pallas_call(
        functools.partial(_mla_attn_kernel, n_ctx_blocks=ctx_len // tq, ctx_len=ctx_len),
        grid=(b, MLA_HEADS // 2, t // tq),
        in_specs=[pl.BlockSpec((None, tq, 2 * LANES), lambda i, h, j: (i, j, h)),
                  pl.BlockSpec((None, t, 2 * LANES), lambda i, h, j: (i, 0, h)),
                  pl.BlockSpec((None, t, LANES), lambda i, h, j: (i, 0, h))],
        out_specs=pl.BlockSpec((None, tq, LANES), lambda i, h, j: (i, j, h)),
        out_shape=jax.ShapeDtypeStruct((b, t, MLA_HEADS * MLA_V), F32),
        compiler_params=_params(3, VMEM_LIMIT),
        name="mla_attention",
    )(q, k, v)


def _odd_prep_kernel(qr_ref, kr_ref, qd_ref, kd_ref, vd_ref, tr_ref, td_ref, qkr_ref, qdo_ref, kdo_ref, vdo_ref):
    tr = [tr_ref[i] for i in range(3)]
    td = [td_ref[i] for i in range(3)]
    n_ret = RET_HEADS * RET_QK // LANES
    for i in range(n_ret):
        cols = slice(i * LANES, (i + 1) * LANES)
        qkr_ref[:, cols] = _rope(qr_ref[:, cols], *tr, RET_QK // 2)
        qkr_ref[:, (n_ret + i) * LANES:(n_ret + i + 1) * LANES] = (
            _rope(kr_ref[:, cols], *tr, RET_QK // 2) * (RET_QK ** -0.5))
    for h in range(DIFF_HEADS):
        cols = slice(h * LANES, (h + 1) * LANES)
        qdo_ref[:, cols] = (_rope(qd_ref[:, cols], *td, DIFF_QK // 4) * (DIFF_QK ** -0.5)).astype(BF16)
        kdo_ref[:, cols] = _rope(kd_ref[:, cols], *td, DIFF_QK // 4).astype(BF16)
    vdo_ref[...] = vd_ref[...].astype(BF16)


def _odd_prep(p, tab_ret, tab_diff, tm):
    b, t, _ = p.shape
    wr = RET_HEADS * RET_QK
    wd = DIFF_HEADS * 2 * DIFF_QK

    def col(width, base):
        return pl.BlockSpec((None, tm, width), lambda i, j: (i, j, base * LANES // width))

    tab = pl.BlockSpec((3, tm, LANES), lambda i, j: (0, j, 0))
    row = lambda width: pl.BlockSpec((None, tm, width), lambda i, j: (i, j, 0))
    return pl.pallas_call(
        _odd_prep_kernel,
        grid=(b, t // tm),
        in_specs=[col(wr, OD_QR), col(wr, OD_KR), col(wd, OD_QD), col(wd, OD_KD), col(wd, OD_VD), tab, tab],
        out_specs=[row(2 * wr), row(wd), row(wd), row(wd)],
        out_shape=[jax.ShapeDtypeStruct((b, t, 2 * wr), F32), jax.ShapeDtypeStruct((b, t, wd), BF16),
                   jax.ShapeDtypeStruct((b, t, wd), BF16), jax.ShapeDtypeStruct((b, t, wd), BF16)],
        compiler_params=_params(2),
        name="odd_prep",
    )(p, p, p, p, p, tab_ret, tab_diff)


def _ret_chunk(q, k, v, lg, st, head_lanes, reverse):
    c = CHUNK
    row = lax.broadcasted_iota(jnp.int32, (c, LANES), 0)
    tau = ((c - 1 - row) if reverse else row).astype(F32)
    ti = lax.broadcasted_iota(jnp.int32, (c, c), 0)
    si = lax.broadcasted_iota(jnp.int32, (c, c), 1)
    rel = ((si - ti) if reverse else (ti - si)).astype(F32)
    dmat = jnp.where(rel >= 0, jnp.exp(jnp.maximum(rel, 0.0) * lg[:, :c]), 0.0)
    qm = jnp.where(head_lanes, q, 0.0)
    att = _dot_nt(qm.astype(BF16), k.astype(BF16)) * dmat
    q_dec = jnp.exp((tau + 1.0) * lg)
    k_dec = jnp.exp((c - 1.0 - tau) * lg)
    o = _dot(att.astype(BF16), v.astype(BF16)) + _dot_nt((qm * q_dec).astype(BF16), st.astype(BF16))
    st = st * jnp.exp(c * lg) + _dot_tn(v.astype(BF16), (k * k_dec).astype(BF16))
    return o, st


def _ret_kernel(qf_ref, kf_ref, vf_ref, qb_ref, kb_ref, vb_ref, lgf_ref, lgb_ref, of_ref, ob_ref, sf_ref, sb_ref,
                *, n_chunks):
    @pl.when(pl.program_id(2) == 0)
    def _():
        sf_ref[...] = jnp.zeros_like(sf_ref)
        sb_ref[...] = jnp.zeros_like(sb_ref)

    lane = lax.broadcasted_iota(jnp.int32, (CHUNK, LANES), 1)
    head_lanes = (lane // RET_QK) == (pl.program_id(1) % (LANES // RET_QK))
    dirs = ((qf_ref, kf_ref, vf_ref, lgf_ref, of_ref, sf_ref), (qb_ref, kb_ref, vb_ref, lgb_ref, ob_ref, sb_ref))
    for d, (q_ref, k_ref, v_ref, lg_ref, o_ref, s_ref) in enumerate(dirs):
        lg = _log_sigmoid(lg_ref[...])
        st = s_ref[...]
        for ci in (range(n_chunks) if d == 0 else reversed(range(n_chunks))):
            rows = slice(ci * CHUNK, (ci + 1) * CHUNK)
            o, st = _ret_chunk(q_ref[rows, :], k_ref[rows, :], v_ref[rows, :], lg, st, head_lanes, d == 1)
            o_ref[rows, :] = o
        s_ref[...] = st


def _ret_scan(qk, p, decay, tb, n_ctx_blocks):
    b, t, _ = p.shape
    nb = t // tb
    fwd, bwd = _scan_block_maps(nb, n_ctx_blocks)
    per_block = LANES // RET_QK
    n_pairs = RET_HEADS // per_block

    def qk_spec(base, order):
        return pl.BlockSpec((None, tb, LANES), lambda i, h, j: (i, order(j), base + h // per_block))

    def v_spec(order):
        return pl.BlockSpec((None, tb, LANES), lambda i, h, j: (i, order(j), OD_VR + h))

    def lg_spec(d):
        return pl.BlockSpec((None, None, 1, LANES), lambda i, h, j: (d, h, 0, 0))

    def out_spec(order):
        return pl.BlockSpec((None, tb, LANES), lambda i, h, j: (i, order(j), h))

    out = jax.ShapeDtypeStruct((b, t, RET_HEADS * RET_V), F32)
    return pl.pallas_call(
        functools.partial(_ret_kernel, n_chunks=tb // CHUNK),
        grid=(b, RET_HEADS, nb),
        in_specs=[qk_spec(0, fwd), qk_spec(n_pairs, fwd), v_spec(fwd), qk_spec(0, bwd), qk_spec(n_pairs, bwd),
                  v_spec(bwd), lg_spec(0), lg_spec(1)],
        out_specs=[out_spec(fwd), out_spec(bwd)],
        out_shape=[out, out],
        scratch_shapes=[pltpu.VMEM((RET_V, LANES), F32), pltpu.VMEM((RET_V, LANES), F32)],
        compiler_params=_params(3),
        name="retention_scan",
    )(qk, qk, p, qk, qk, p, decay, decay)


def _diff_attn_kernel(q_ref, k_ref, v_ref, lam_ref, g_ref, o_ref, *, n_ctx_blocks, ctx_len, lambda_init):
    lp = lam_ref[...]
    lam = (jnp.exp(jnp.sum(lp[0:1] * lp[1:2], axis=-1, keepdims=True))
           - jnp.exp(jnp.sum(lp[2:3] * lp[3:4], axis=-1, keepdims=True)) + lambda_init)

    def run(n_keys):
        q = q_ref[...]
        k = k_ref[0:n_keys, :]
        first = lax.broadcasted_iota(jnp.int32, q.shape, 1) < DIFF_QK
        zero = jnp.zeros_like(q)
        probs = []
        for qm in (jnp.where(first, q, zero), jnp.where(first, zero, q)):
            s = _dot_nt(qm, k)
            e = jnp.exp(s - jnp.max(s, axis=-1, keepdims=True))
            probs.append(e / jnp.sum(e, axis=-1, keepdims=True))
        a = (probs[0] - lam * probs[1]).astype(BF16)
        o = _dot(a, v_ref[0:n_keys, :])
        o_ref[...] = _rms(o) * g_ref[...] * (1.0 - lambda_init)

    qi = pl.program_id(2)
    pl.when(qi < n_ctx_blocks)(lambda: run(ctx_len))
    pl.when(qi >= n_ctx_blocks)(lambda: run(k_ref.shape[0]))


def _diff_attention(q, k, v, lam_params, subln_g, tq, ctx_len, lambda_init):
    b, t, _ = q.shape
    full = lambda a: pl.BlockSpec(a.shape, lambda i, h, j: (0,) * a.ndim)
    return pl.pallas_call(
        functools.partial(_diff_attn_kernel, n_ctx_blocks=ctx_len // tq, ctx_len=ctx_len, lambda_init=lambda_init),
        grid=(b, DIFF_HEADS, t // tq),
        in_specs=[pl.BlockSpec((None, tq, LANES), lambda i, h, j: (i, j, h)),
                  pl.BlockSpec((None, t, LANES), lambda i, h, j: (i, 0, h)),
                  pl.BlockSpec((None, t, LANES), lambda i, h, j: (i, 0, h)),
                  full(lam_params), full(subln_g)],
        out_specs=pl.BlockSpec((None, tq, LANES), lambda i, h, j: (i, j, h)),
        out_shape=jax.ShapeDtypeStruct((b, t, DIFF_HEADS * DIFF_V), F32),
        compiler_params=_params(3, VMEM_LIMIT),
        name="diff_attention",
    )(q, k, v, lam_params, subln_g)


def _outproj_kernel(x_ref, of_ref, ob_ref, gate_ref, mix_ref, ng_ref, mod_ref, w_ref, o_ref, *, n_heads):
    o = of_ref[...] + ob_ref[...]
    gate = gate_ref[...]
    parts = []
    for h in range(n_heads):
        cols = slice(h * LANES, (h + 1) * LANES)
        parts.append((_rms(o[:, cols]) * ng_ref[...] * _silu(gate[:, cols])).astype(BF16))
    a = jnp.concatenate(parts, axis=1)
    wa = a.shape[1]
    u = _dot(a, w_ref[0:wa, :]) + _dot(mix_ref[...].astype(BF16), w_ref[wa:, :])
    o_ref[...] = x_ref[...] + mod_ref[2:3, :] * u


def _out_projection(x, o_f, o_b, p, gate_base, mix, norm_gain, mod, w, layer, tm, n_ctx_tiles):
    b, t, d = x.shape
    wa = o_f.shape[2]
    row = lambda width: pl.BlockSpec((None, tm, width), lambda i, j: (i, j, 0))
    return pl.pallas_call(
        functools.partial(_outproj_kernel, n_heads=wa // LANES),
        grid=(b, t // tm),
        in_specs=[row(d), row(wa), row(wa),
                  pl.BlockSpec((None, tm, wa), lambda i, j: (i, j, gate_base * LANES // wa)),
                  row(mix.shape[2]),
                  pl.BlockSpec(norm_gain.shape, lambda i, j: (0, 0)),
                  _mod_spec(layer, b, n_ctx_tiles, d),
                  pl.BlockSpec(w.shape, lambda i, j: (0, 0))],
        out_specs=row(d),
        out_shape=jax.ShapeDtypeStruct((b, t, d), F32),
        compiler_params=_params(2, VMEM_LIMIT),
        name="out_projection",
    )(x, o_f, o_b, p, mix, norm_gain, mod, w)


def _ffn_kernel(x_ref, mod_ref, g_ref, wg_ref, wu_ref, wd_ref, o_ref):
    x = x_ref[...]
    h = (_rms(x) * g_ref[1:2, :] * (1.0 + mod_ref[4:5, :]) + mod_ref[3:4, :]).astype(BF16)
    act = (_silu(_dot(h, wg_ref[...])) * _dot(h, wu_ref[...])).astype(BF16)
    o_ref[...] = x + mod_ref[5:6, :] * _dot(act, wd_ref[...])


def _ffn(x, mod, norm_g, wg, wu, wd, layer, tm, n_ctx_tiles):
    b, t, d = x.shape
    row = pl.BlockSpec((None, tm, d), lambda i, j: (i, j, 0))
    full = lambda a: pl.BlockSpec(a.shape, lambda i, j: (0, 0), pipeline_mode=pl.Buffered(1))
    return pl.pallas_call(
        _ffn_kernel,
        grid=(b, t // tm),
        in_specs=[row, _mod_spec(layer, b, n_ctx_tiles, d), pl.BlockSpec((None, 2, d), lambda i, j: (layer, 0, 0)),
                  full(wg), full(wu), full(wd)],
        out_specs=row,
        out_shape=jax.ShapeDtypeStruct((b, t, d), F32),
        compiler_params=_params(2, VMEM_LIMIT),
        name="ffn",
    )(x, mod, norm_g, wg, wu, wd)


def _final_norm_kernel(x_ref, g_ref, o_ref):
    o_ref[...] = _rms(x_ref[...]) * g_ref[...]


def _final_norm(x, g, tm, n_ctx_tiles, seq):
    b, _, d = x.shape
    return pl.pallas_call(
        _final_norm_kernel,
        grid=(b, seq // tm),
        in_specs=[pl.BlockSpec((None, tm, d), lambda i, j: (i, j + n_ctx_tiles, 0)),
                  pl.BlockSpec((1, d), lambda i, j: (0, 0))],
        out_specs=pl.BlockSpec((None, tm, d), lambda i, j: (i, j, 0)),
        out_shape=jax.ShapeDtypeStruct((b, seq, d), F32),
        compiler_params=_params(2),
        name="final_norm",
    )(x, g)


def _rope_tables(ctx_len, seq, kind, scale=1.0):
    lane = np.arange(LANES)
    n = jnp.arange(seq, dtype=jnp.int32)
    if kind == "ret":
        half = RET_QK // 2
        idx = lane % RET_QK
        active = np.ones(LANES, bool)
        pos = jnp.broadcast_to(n[:, None], (seq, LANES))
    else:
        width = DIFF_QK if kind == "diff" else MLA_ROPE
        half = width // 4
        if kind == "diff":
            idx = lane % (width // 2)
            by_col = (lane % width) >= width // 2
            active = np.ones(LANES, bool)
        else:
            idx = (lane - MLA_NOPE) % (width // 2)
            by_col = (lane - MLA_NOPE) >= width // 2
            active = (lane >= MLA_NOPE) & (lane < MLA_NOPE + width)
        pos = jnp.where(jnp.asarray(by_col)[None, :], (n % GRID_W)[:, None], (n // GRID_W)[:, None])
    upper = idx >= half
    inv = ROPE_BASE ** (-jnp.asarray(idx % half, F32) / half)
    ang = pos.astype(F32) * inv[None, :]
    act = jnp.asarray(active)[None, :]
    up = jnp.asarray(upper)[None, :]
    cos = jnp.where(act, jnp.cos(ang), 1.0)
    sin = jnp.where(act, jnp.sin(ang), 0.0)
    lat = jnp.stack([cos, jnp.where(up, sin, 0.0), jnp.where(up, 0.0, -sin)])
    ident = jnp.stack([jnp.ones((ctx_len, LANES), F32), jnp.zeros((ctx_len, LANES), F32),
                       jnp.zeros((ctx_len, LANES), F32)])
    return jnp.concatenate([ident, lat], axis=1) * scale


def _pad_cols(w, width):
    return jnp.pad(w, ((0, 0), (0, width - w.shape[1])))


def kernel(x, c, ctx, c_ctx, ada_w, ada_b, norm_g, mix_w_out, ffn_w_gate, ffn_w_up, ffn_w_down,
           even_w_in, hgrn_lb_logits, hgrn_norm_g, mla_q_norm_g, mla_w_uq, mla_kv_norm_g, mla_w_ukv,
           odd_w_in, ret_decay_logits, diff_lambda, diff_subln_g, final_norm_g):
    batch, seq, d = x.shape
    ctx_len = ctx.shape[1]
    depth = ada_w.shape[0]
    tm = math.gcd(ROW_TILE, math.gcd(ctx_len, seq))
    n_ctx_tiles = ctx_len // tm
    assert tm % CHUNK == 0 and seq % GRID_W == 0

    mod_rows = -(-(batch + 1) // 8) * 8
    s_rows = jnp.concatenate([c, c_ctx[None, :], jnp.zeros((mod_rows - batch - 1, d), F32)], axis=0)
    mod = _modulation(s_rows, ada_w, ada_b).reshape(depth, mod_rows, 6, d)

    lb_soft = jax.nn.softmax(hgrn_lb_logits.astype(F32), axis=0)
    lbs = jnp.cumsum(lb_soft, axis=0) - lb_soft[0:1]

    tab_mla_q = _rope_tables(ctx_len, seq, "mla", (MLA_NOPE + MLA_ROPE) ** -0.5)
    tab_mla_k = _rope_tables(ctx_len, seq, "mla")
    tab_ret = _rope_tables(ctx_len, seq, "ret")
    tab_diff = _rope_tables(ctx_len, seq, "diff")

    stream = jnp.concatenate([ctx, x], axis=1)
    for layer in range(depth):
        w_out = mix_w_out[layer].astype(BF16)
        if layer % 2 == 0:
            e = layer // 2
            w = even_w_in[e]
            q_i_f, gate, c_q, c_kv, k_rope = (
                w[:, :4 * HGRN_WIDTH], w[:, 4 * HGRN_WIDTH:5 * HGRN_WIDTH],
                w[:, 5 * HGRN_WIDTH:5 * HGRN_WIDTH + MLA_Q_RANK],
                w[:, 5 * HGRN_WIDTH + MLA_Q_RANK:-MLA_ROPE], w[:, -MLA_ROPE:])
            k_rope = jnp.pad(k_rope, ((0, 0), (MLA_NOPE, LANES - MLA_NOPE - MLA_ROPE)))
            w_in = jnp.concatenate([c_q, k_rope, gate, c_kv, q_i_f], axis=1).astype(BF16)
            p = _in_projection(stream, mod, norm_g, w_in, layer, tm, n_ctx_tiles)
            o_f, o_b = _hgrn_scan(p, lbs[e].reshape(2, 1, HGRN_WIDTH), tm, n_ctx_tiles)
            wq = mla_w_uq[e].reshape(MLA_Q_RANK, MLA_HEADS, MLA_NOPE + MLA_ROPE)
            wq = jnp.pad(wq, ((0, 0), (0, 0), (0, LANES - MLA_NOPE - MLA_ROPE))).reshape(MLA_Q_RANK, -1)
            wkv = mla_w_ukv[e].reshape(MLA_KV_RANK, MLA_HEADS, MLA_NOPE + MLA_V)
            wk = jnp.pad(wkv[:, :, :MLA_NOPE], ((0, 0), (0, 0), (0, LANES - MLA_NOPE))).reshape(MLA_KV_RANK, -1)
            wv = wkv[:, :, MLA_NOPE:].reshape(MLA_KV_RANK, -1)
            q, k, v = _mla_prep(p, mla_q_norm_g[e][None, :], mla_kv_norm_g[e][None, :], wq.astype(BF16),
                                jnp.concatenate([wk, wv], axis=1).astype(BF16), tab_mla_q, tab_mla_k, tm)
            mix = _mla_attention(q, k, v, tm, ctx_len)
            gate_base, gain = EV_G, hgrn_norm_g[e][None, :]
        else:
            o = layer // 2
            p = _in_projection(stream, mod, norm_g, odd_w_in[o].astype(BF16), layer, tm, n_ctx_tiles)
            qk_ret, q_d, k_d, v_d = _odd_prep(p, tab_ret, tab_diff, tm)
            decay = jnp.broadcast_to(ret_decay_logits[o][:, :, None, None], (2, RET_HEADS, 1, LANES))
            o_f, o_b = _ret_scan(qk_ret, p, decay, tm, n_ctx_tiles)
            lambda_init = 0.8 - 0.6 * math.exp(-0.3 * layer)
            mix = _diff_attention(q_d, k_d, v_d, diff_lambda[o], diff_subln_g[o][None, :], tm, ctx_len, lambda_init)
            gate_base, gain = OD_GR, jnp.ones((1, LANES), F32)
        stream = _out_projection(stream, o_f, o_b, p, gate_base, mix, gain, mod, w_out, layer, tm, n_ctx_tiles)
        stream = _ffn(stream, mod, norm_g, ffn_w_gate[layer].astype(BF16), ffn_w_up[layer].astype(BF16),
                      ffn_w_down[layer].astype(BF16), layer, tm, n_ctx_tiles)
    return _final_norm(stream, final_norm_g[None, :], tm, n_ctx_tiles, seq)
```

```python
import functools
import math

import numpy as np
import jax
import jax.numpy as jnp
from jax import lax
from jax.experimental import pallas as pl
from jax.experimental.pallas import tpu as pltpu

F32 = jnp.float32
BF16 = jnp.bfloat16

LANES = 128
GRID_W = 64
CHUNK = 64
ROPE_BASE = 10000.0
EPS = 1e-6

HGRN_HEADS, HGRN_DIM = 4, 128
HGRN_WIDTH = HGRN_HEADS * HGRN_DIM
MLA_HEADS, MLA_Q_RANK, MLA_KV_RANK, MLA_NOPE, MLA_ROPE, MLA_V = 8, 384, 256, 64, 32, 64
RET_HEADS, RET_QK, RET_V = 4, 64, 128
DIFF_HEADS, DIFF_QK, DIFF_V = 4, 64, 128
RET_QK_WIDTH = RET_HEADS * RET_QK
RET_V_WIDTH = RET_HEADS * RET_V
DIFF_QK_WIDTH = DIFF_HEADS * 2 * DIFF_QK
DIFF_V_WIDTH = DIFF_HEADS * DIFF_V

EV_HGRN = 5 * HGRN_WIDTH
EV_CQ = EV_HGRN
EV_CKV = EV_CQ + MLA_Q_RANK
EV_KROPE = EV_CKV + MLA_KV_RANK
EV_COLS = EV_KROPE + LANES
OD_KR = RET_QK_WIDTH
OD_VR = OD_KR + RET_QK_WIDTH
OD_QD = OD_VR + 2 * RET_V_WIDTH
OD_KD = OD_QD + DIFF_QK_WIDTH
OD_VD = OD_KD + DIFF_QK_WIDTH

ROW_TILE = 256
VMEM_LIMIT = 48 * 1024 * 1024


def _params(n_axes, vmem=None):
    return pltpu.CompilerParams(dimension_semantics=("arbitrary",) * n_axes, vmem_limit_bytes=vmem)


def _resident(a):
    return pl.BlockSpec(a.shape, lambda *_: (0,) * a.ndim, pipeline_mode=pl.Buffered(1))


def _dot(a, b):
    return jnp.dot(a, b, preferred_element_type=F32)


def _dot_nt(a, b):
    return lax.dot_general(a, b, (((1,), (1,)), ((), ())), preferred_element_type=F32)


def _dot_tn(a, b):
    return lax.dot_general(a, b, (((0,), (0,)), ((), ())), preferred_element_type=F32)


def _rms(x):
    return x * lax.rsqrt(jnp.mean(x * x, axis=-1, keepdims=True) + EPS)


def _silu(x):
    return x * jax.nn.sigmoid(x)


def _log_sigmoid(z):
    return jnp.minimum(z, 0.0) - jnp.log1p(jnp.exp(-jnp.abs(z)))


def _rope(x, tab_ref, shift):
    return x * tab_ref[0] + pltpu.roll(x, shift, 1) * tab_ref[1] + pltpu.roll(x, LANES - shift, 1) * tab_ref[2]


def _lane_blocks(width):
    return [slice(i * LANES, (i + 1) * LANES) for i in range(width // LANES)]


def _mod_kernel(s_ref, w_ref, b_ref, o_ref):
    s = _silu(s_ref[...])
    o_ref[...] = jnp.dot(s, w_ref[...], precision=lax.Precision.HIGHEST, preferred_element_type=F32) + b_ref[...]


def _modulation(s_rows, ada_w, ada_b):
    depth, d, d6 = ada_w.shape
    rows = s_rows.shape[0]
    tn = d
    return pl.pallas_call(
        _mod_kernel,
        grid=(depth, d6 // tn),
        in_specs=[pl.BlockSpec((rows, d), lambda l, n: (0, 0)),
                  pl.BlockSpec((None, d, tn), lambda l, n: (l, 0, n)),
                  pl.BlockSpec((None, 1, tn), lambda l, n: (l, 0, n))],
        out_specs=pl.BlockSpec((None, rows, tn), lambda l, n: (l, 0, n)),
        out_shape=jax.ShapeDtypeStruct((depth, rows, d6), F32),
        compiler_params=_params(2),
        name="modulation",
    )(s_rows, ada_w, ada_b.reshape(depth, 1, d6))


def _mod_spec(layer, batch, n_ctx_tiles, d, row_off=0):
    return pl.BlockSpec((None, None, 6, d),
                        lambda b, t: (layer, jnp.where(t + row_off < n_ctx_tiles, batch, b), 0, 0))


def _norm_spec(layer, d):
    return pl.BlockSpec((None, 2, d), lambda i, j: (layer, 0, 0))


def _rows(tm, width, row_off=0):
    return pl.BlockSpec((None, tm, width), lambda i, j: (i, j + row_off, 0))


def _even_in_kernel(x_ref, mod_ref, g_ref, w_ref, gq_ref, gkv_ref, wq_ref, wkv_ref, tq_ref, tk_ref,
                    hg_ref, q_ref, k_ref, v_ref):
    shift = MLA_ROPE // 4
    h = _rms(x_ref[...]) * g_ref[0:1, :] * (1.0 + mod_ref[1:2, :]) + mod_ref[0:1, :]
    p = _dot(h.astype(BF16), w_ref[...])
    hg_ref[...] = p[:, :EV_HGRN].astype(BF16)
    q = _dot((_rms(p[:, EV_CQ:EV_CKV]) * gq_ref[...]).astype(BF16), wq_ref[...])
    kv = _dot((_rms(p[:, EV_CKV:EV_KROPE]) * gkv_ref[...]).astype(BF16), wkv_ref[...])
    k_rope = _rope(pltpu.roll(p[:, EV_KROPE:], MLA_NOPE, 1), tk_ref, shift)
    for cols in _lane_blocks(MLA_HEADS * LANES):
        q_ref[:, cols] = _rope(q[:, cols], tq_ref, shift).astype(BF16)
        k_ref[:, cols] = (kv[:, cols] + k_rope).astype(BF16)
    v_ref[...] = kv[:, MLA_HEADS * LANES:].astype(BF16)


def _even_in_projection(x, mod, norm_g, w, gq, gkv, wq, wkv, tab_q, tab_k, layer, tm, n_ctx_tiles):
    b, t, d = x.shape
    hq = MLA_HEADS * LANES
    tab = pl.BlockSpec((3, tm, LANES), lambda i, j: (0, j, 0))
    shapes = [(EV_HGRN, BF16), (hq, BF16), (hq, BF16), (MLA_HEADS * MLA_V, BF16)]
    return pl.pallas_call(
        _even_in_kernel,
        grid=(b, t // tm),
        in_specs=[_rows(tm, d), _mod_spec(layer, b, n_ctx_tiles, d), _norm_spec(layer, d), _resident(w),
                  _resident(gq), _resident(gkv), _resident(wq), _resident(wkv), tab, tab],
        out_specs=[_rows(tm, width) for width, _ in shapes],
        out_shape=[jax.ShapeDtypeStruct((b, t, width), dt) for width, dt in shapes],
        compiler_params=_params(2, VMEM_LIMIT),
        name="even_in_projection",
    )(x, mod, norm_g, w, gq, gkv, wq, wkv, tab_q, tab_k)


def _odd_in_kernel(x_ref, mod_ref, g_ref, w_ref, tr_ref, td_ref, qkr_ref, vg_ref, qd_ref, kd_ref, vd_ref):
    h = _rms(x_ref[...]) * g_ref[0:1, :] * (1.0 + mod_ref[1:2, :]) + mod_ref[0:1, :]
    p = _dot(h.astype(BF16), w_ref[...])
    for cols in _lane_blocks(RET_QK_WIDTH):
        qkr_ref[:, cols] = _rope(p[:, cols], tr_ref, RET_QK // 2)
        kc = slice(cols.start + OD_KR, cols.stop + OD_KR)
        qkr_ref[:, kc] = _rope(p[:, kc], tr_ref, RET_QK // 2) * (RET_QK ** -0.5)
    vg_ref[...] = p[:, OD_VR:OD_QD].astype(BF16)
    for cols in _lane_blocks(DIFF_QK_WIDTH):
        qc = slice(cols.start + OD_QD, cols.stop + OD_QD)
        kc = slice(cols.start + OD_KD, cols.stop + OD_KD)
        qd_ref[:, cols] = (_rope(p[:, qc], td_ref, DIFF_QK // 4) * (DIFF_QK ** -0.5)).astype(BF16)
        kd_ref[:, cols] = _rope(p[:, kc], td_ref, DIFF_QK // 4).astype(BF16)
    vd_ref[...] = p[:, OD_VD:].astype(BF16)


def _odd_in_projection(x, mod, norm_g, w, tab_ret, tab_diff, layer, tm, n_ctx_tiles):
    b, t, d = x.shape
    tab = pl.BlockSpec((3, tm, LANES), lambda i, j: (0, j, 0))
    shapes = [(2 * RET_QK_WIDTH, F32), (2 * RET_V_WIDTH, BF16), (DIFF_QK_WIDTH, BF16), (DIFF_QK_WIDTH, BF16),
              (DIFF_V_WIDTH, BF16)]
    return pl.pallas_call(
        _odd_in_kernel,
        grid=(b, t // tm),
        in_specs=[_rows(tm, d), _mod_spec(layer, b, n_ctx_tiles, d), _norm_spec(layer, d), _resident(w), tab, tab],
        out_specs=[_rows(tm, width) for width, _ in shapes],
        out_shape=[jax.ShapeDtypeStruct((b, t, width), dt) for width, dt in shapes],
        compiler_params=_params(2, VMEM_LIMIT),
        name="odd_in_projection",
    )(x, mod, norm_g, w, tab_ret, tab_diff)


_LEVELS = 6


def _gla_tables():
    c = CHUNK
    m = np.zeros((_LEVELS + 2, c, c), np.float32)
    masks = np.zeros((_LEVELS + 1, c, c), np.float32)
    masks[0] = np.eye(c)
    for lv in range(_LEVELS):
        b = 32 >> lv
        for tau in range(c):
            mid = (tau // (2 * b)) * 2 * b + b
            if tau >= mid:
                m[lv, tau, mid:tau + 1] = 1.0
                masks[lv + 1, tau, mid - b:mid] = 1.0
            else:
                m[lv, tau, tau + 1:mid] = 1.0
    for tau in range(c):
        m[_LEVELS, tau, :tau + 1] = 1.0
        m[_LEVELS + 1, tau, tau + 1:] = 1.0
    flip = lambda a: a[:, ::-1, ::-1]
    mcat = np.stack([m, flip(m)]).reshape(2, (_LEVELS + 2) * c, c)
    mcat = np.concatenate([mcat] * 3, axis=2)
    mk = np.stack([masks, flip(masks)]).reshape(2, (_LEVELS + 1) * c, c)
    return jnp.asarray(mcat, BF16), jnp.asarray(mk, F32)


def _gla_gates(z, lb):
    k = (1.0 - lb) * jax.nn.sigmoid(-z)
    log_f = _log_sigmoid(z) + jnp.log1p(lb * jnp.exp(-z))
    g_hi = log_f.astype(BF16)
    r1 = log_f - g_hi.astype(F32)
    g_mid = r1.astype(BF16)
    g_lo = (r1 - g_mid.astype(F32)).astype(BF16)
    return k, jnp.concatenate([g_hi, g_mid, g_lo], axis=0)


def _gla_chunk(q, v, k, dec, masks, st, reverse):
    c = CHUNK
    qs = q * (HGRN_DIM ** -0.5)
    row = lax.broadcasted_iota(jnp.int32, (c, LANES), 0)
    tau = (c - 1 - row) if reverse else row
    att = masks[0:c] * _dot_nt(qs.astype(BF16), k.astype(BF16))
    for lv in range(_LEVELS):
        b = 32 >> lv
        zl = (jnp.where((tau & b) != 0, qs, k) * dec[lv * c:(lv + 1) * c]).astype(BF16)
        att = att + masks[(lv + 1) * c:(lv + 2) * c] * _dot_nt(zl, zl)
    dec_q = dec[_LEVELS * c:(_LEVELS + 1) * c]
    dec_k = dec[(_LEVELS + 1) * c:(_LEVELS + 2) * c]
    o = _dot(att.astype(BF16), v.astype(BF16)) + _dot_nt((qs * dec_q).astype(BF16), st.astype(BF16))
    last = 0 if reverse else c - 1
    st = st * dec_q[last:last + 1, :] + _dot_tn(v.astype(BF16), (k * dec_k).astype(BF16))
    return o, st


def _hgrn_kernel(qf_ref, if_ref, zf_ref, qb_ref, ib_ref, zb_ref, lbf_ref, lbb_ref, m_ref, mk_ref,
                 of_ref, ob_ref, sf_ref, sb_ref, *, n_chunks):
    @pl.when(pl.program_id(2) == 0)
    def _():
        sf_ref[...] = jnp.zeros_like(sf_ref)
        sb_ref[...] = jnp.zeros_like(sb_ref)

    dirs = ((qf_ref, if_ref, zf_ref, lbf_ref, of_ref, sf_ref), (qb_ref, ib_ref, zb_ref, lbb_ref, ob_ref, sb_ref))
    for d, (q_ref, i_ref, z_ref, lb_ref, o_ref, s_ref) in enumerate(dirs):
        st = s_ref[...]
        order = list(range(n_chunks) if d == 0 else reversed(range(n_chunks)))
        for pair in (order[i:i + 2] for i in range(0, n_chunks, 2)):
            gates = [_gla_gates(z_ref[ci * CHUNK:(ci + 1) * CHUNK, :].astype(F32), lb_ref[...]) for ci in pair]
            dec = jnp.exp(_dot(m_ref[d], jnp.concatenate([g for _, g in gates], axis=1)))
            for n, ci in enumerate(pair):
                rows = slice(ci * CHUNK, (ci + 1) * CHUNK)
                o, st = _gla_chunk(q_ref[rows, :].astype(F32), i_ref[rows, :].astype(F32), gates[n][0],
                                   dec[:, n * LANES:(n + 1) * LANES], mk_ref[d], st, d == 1)
                o_ref[rows, :] = o
        s_ref[...] = st


def _scan_block_maps(n_blocks, n_ctx_blocks):
    def bwd(j):
        return jnp.where(j < n_ctx_blocks, n_ctx_blocks - 1 - j, n_blocks - 1 - (j - n_ctx_blocks))
    return (lambda j: j), bwd


def _hgrn_scan(p, lbs, tb, n_ctx_blocks):
    b, t, _ = p.shape
    nb = t // tb
    fwd, bwd = _scan_block_maps(nb, n_ctx_blocks)
    mcat, masks = _gla_tables()

    def col(group, order):
        return pl.BlockSpec((None, tb, LANES), lambda i, h, j: (i, order(j), group * HGRN_HEADS + h))

    def lb_spec(d):
        return pl.BlockSpec((None, 1, LANES), lambda i, h, j: (d, 0, h))

    def out_spec(order):
        return pl.BlockSpec((None, tb, LANES), lambda i, h, j: (i, order(j), h))

    out = jax.ShapeDtypeStruct((b, t, HGRN_WIDTH), F32)
    return pl.pallas_call(
        functools.partial(_hgrn_kernel, n_chunks=tb // CHUNK),
        grid=(b, HGRN_HEADS, nb),
        in_specs=[col(0, fwd), col(1, fwd), col(2, fwd), col(0, bwd), col(1, bwd), col(3, bwd),
                  lb_spec(0), lb_spec(1), _resident(mcat), _resident(masks)],
        out_specs=[out_spec(fwd), out_spec(bwd)],
        out_shape=[out, out],
        scratch_shapes=[pltpu.VMEM((HGRN_DIM, HGRN_DIM), F32), pltpu.VMEM((HGRN_DIM, HGRN_DIM), F32)],
        compiler_params=_params(3),
        name="hgrn_scan",
    )(p, p, p, p, p, p, lbs, lbs, mcat, masks)


def _softmax_pv(s, v):
    e = jnp.exp(s - jnp.max(s, axis=-1, keepdims=True))
    return _dot(e.astype(BF16), v) / jnp.sum(e, axis=-1, keepdims=True)


def _mla_attn_kernel(q_ref, k_ref, v_ref, o_ref, *, n_ctx_blocks, ctx_len):
    def run(n_keys):
        outs = []
        for cols in _lane_blocks(2 * LANES):
            s = _dot_nt(q_ref[:, cols], k_ref[0:n_keys, cols])
            outs.append(_softmax_pv(s, v_ref[0:n_keys, :]))
        lane = lax.broadcasted_iota(jnp.int32, outs[0].shape, 1)
        o_ref[...] = jnp.where(lane < MLA_V, outs[0], outs[1]).astype(BF16)

    qi = pl.program_id(2)
    pl.when(qi < n_ctx_blocks)(lambda: run(ctx_len))
    pl.when(qi >= n_ctx_blocks)(lambda: run(k_ref.shape[0]))


def _mla_attention(q, k, v, tq, ctx_len):
    b, t, _ = q.shape
    return pl.pallas_call(
        functools.partial(_mla_attn_kernel, n_ctx_blocks=ctx_len // tq, ctx_len=ctx_len),
        grid=(b, MLA_HEADS // 2, t // tq),
        in_specs=[pl.BlockSpec((None, tq, 2 * LANES), lambda i, h, j: (i, j, h)),
                  pl.BlockSpec((None, t, 2 * LANES), lambda i, h, j: (i, 0, h)),
                  pl.BlockSpec((None, t, LANES), lambda i, h, j: (i, 0, h))],
        out_specs=pl.BlockSpec((None, tq, LANES), lambda i, h, j: (i, j, h)),
        out_shape=jax.ShapeDtypeStruct((b, t, MLA_HEADS * MLA_V), BF16),
        compiler_params=_params(3, VMEM_LIMIT),
        name="mla_attention",
    )(q, k, v)


def _ret_chunk(q, k, v, lg, st, head_lanes, reverse):
    c = CHUNK
    row = lax.broadcasted_iota(jnp.int32, (c, LANES), 0)
    tau = ((c - 1 - row) if reverse else row).astype(F32)
    ti = lax.broadcasted_iota(jnp.int32, (c, c), 0)
    si = lax.broadcasted_iota(jnp.int32, (c, c), 1)
    rel = ((si - ti) if reverse else (ti - si)).astype(F32)
    dmat = jnp.where(rel >= 0, jnp.exp(jnp.maximum(rel, 0.0) * lg[:, :c]), 0.0)
    qm = jnp.where(head_lanes, q, 0.0)
    att = _dot_nt(qm.astype(BF16), k.astype(BF16)) * dmat
    q_dec = jnp.exp((tau + 1.0) * lg)
    k_dec = jnp.exp((c - 1.0 - tau) * lg)
    o = _dot(att.astype(BF16), v) + _dot_nt((qm * q_dec).astype(BF16), st.astype(BF16))
    st = st * jnp.exp(c * lg) + _dot_tn(v, (k * k_dec).astype(BF16))
    return o, st


def _ret_kernel(qf_ref, kf_ref, vf_ref, qb_ref, kb_ref, vb_ref, lgf_ref, lgb_ref, of_ref, ob_ref, sf_ref, sb_ref,
                *, n_chunks):
    @pl.when(pl.program_id(2) == 0)
    def _():
        sf_ref[...] = jnp.zeros_like(sf_ref)
        sb_ref[...] = jnp.zeros_like(sb_ref)

    lane = lax.broadcasted_iota(jnp.int32, (CHUNK, LANES), 1)
    head_lanes = (lane // RET_QK) == (pl.program_id(1) % (LANES // RET_QK))
    dirs = ((qf_ref, kf_ref, vf_ref, lgf_ref, of_ref, sf_ref), (qb_ref, kb_ref, vb_ref, lgb_ref, ob_ref, sb_ref))
    for d, (q_ref, k_ref, v_ref, lg_ref, o_ref, s_ref) in enumerate(dirs):
        lg = _log_sigmoid(lg_ref[...])
        st = s_ref[...]
        for ci in (range(n_chunks) if d == 0 else reversed(range(n_chunks))):
            rows = slice(ci * CHUNK, (ci + 1) * CHUNK)
            o, st = _ret_chunk(q_ref[rows, :], k_ref[rows, :], v_ref[rows, :], lg, st, head_lanes, d == 1)
            o_ref[rows, :] = o
        s_ref[...] = st


def _ret_scan(qk, vg, decay, tb, n_ctx_blocks):
    b, t, _ = qk.shape
    nb = t // tb
    fwd, bwd = _scan_block_maps(nb, n_ctx_blocks)
    per_block = LANES // RET_QK
    n_pairs = RET_HEADS // per_block

    def qk_spec(base, order):
        return pl.BlockSpec((None, tb, LANES), lambda i, h, j: (i, order(j), base + h // per_block))

    def v_spec(order):
        return pl.BlockSpec((None, tb, LANES), lambda i, h, j: (i, order(j), h))

    def lg_spec(d):
        return pl.BlockSpec((None, None, 1, LANES), lambda i, h, j: (d, h, 0, 0))

    def out_spec(order):
        return pl.BlockSpec((None, tb, LANES), lambda i, h, j: (i, order(j), h))

    out = jax.ShapeDtypeStruct((b, t, RET_V_WIDTH), F32)
    return pl.pallas_call(
        functools.partial(_ret_kernel, n_chunks=tb // CHUNK),
        grid=(b, RET_HEADS, nb),
        in_specs=[qk_spec(0, fwd), qk_spec(n_pairs, fwd), v_spec(fwd), qk_spec(0, bwd), qk_spec(n_pairs, bwd),
                  v_spec(bwd), lg_spec(0), lg_spec(1)],
        out_specs=[out_spec(fwd), out_spec(bwd)],
        out_shape=[out, out],
        scratch_shapes=[pltpu.VMEM((RET_V, LANES), F32), pltpu.VMEM((RET_V, LANES), F32)],
        compiler_params=_params(3),
        name="retention_scan",
    )(qk, qk, vg, qk, qk, vg, decay, decay)


def _diff_attn_kernel(q_ref, k_ref, v_ref, lam_ref, g_ref, o_ref, *, n_ctx_blocks, ctx_len, lambda_init):
    lp = lam_ref[...]
    lam = (jnp.exp(jnp.sum(lp[0:1] * lp[1:2], axis=-1, keepdims=True))
           - jnp.exp(jnp.sum(lp[2:3] * lp[3:4], axis=-1, keepdims=True)) + lambda_init)

    def run(n_keys):
        q = q_ref[...]
        k = k_ref[0:n_keys, :]
        first = lax.broadcasted_iota(jnp.int32, q.shape, 1) < DIFF_QK
        zero = jnp.zeros_like(q)
        probs = []
        for qm in (jnp.where(first, q, zero), jnp.where(first, zero, q)):
            s = _dot_nt(qm, k)
            e = jnp.exp(s - jnp.max(s, axis=-1, keepdims=True))
            probs.append(e / jnp.sum(e, axis=-1, keepdims=True))
        a = (probs[0] - lam * probs[1]).astype(BF16)
        o = _dot(a, v_ref[0:n_keys, :])
        o_ref[...] = (_rms(o) * g_ref[...] * (1.0 - lambda_init)).astype(BF16)

    qi = pl.program_id(2)
    pl.when(qi < n_ctx_blocks)(lambda: run(ctx_len))
    pl.when(qi >= n_ctx_blocks)(lambda: run(k_ref.shape[0]))


def _diff_attention(q, k, v, lam_params, subln_g, tq, ctx_len, lambda_init):
    b, t, _ = q.shape
    return pl.pallas_call(
        functools.partial(_diff_attn_kernel, n_ctx_blocks=ctx_len // tq, ctx_len=ctx_len, lambda_init=lambda_init),
        grid=(b, DIFF_HEADS, t // tq),
        in_specs=[pl.BlockSpec((None, tq, LANES), lambda i, h, j: (i, j, h)),
                  pl.BlockSpec((None, t, LANES), lambda i, h, j: (i, 0, h)),
                  pl.BlockSpec((None, t, LANES), lambda i, h, j: (i, 0, h)),
                  _resident(lam_params), _resident(subln_g)],
        out_specs=pl.BlockSpec((None, tq, LANES), lambda i, h, j: (i, j, h)),
        out_shape=jax.ShapeDtypeStruct((b, t, DIFF_V_WIDTH), BF16),
        compiler_params=_params(3, VMEM_LIMIT),
        name="diff_attention",
    )(q, k, v, lam_params, subln_g)


def _mix_ffn_kernel(x_ref, of_ref, ob_ref, gate_ref, mix_ref, ng_ref, mod_ref, g_ref, wo_ref, wg_ref, wu_ref, wd_ref,
                    *rest, final):
    o_ref = rest[-1]
    o = of_ref[...] + ob_ref[...]
    gate = gate_ref[...].astype(F32)
    a = jnp.concatenate([(_rms(o[:, cols]) * ng_ref[...] * _silu(gate[:, cols])).astype(BF16)
                         for cols in _lane_blocks(o.shape[1])], axis=1)
    wa = a.shape[1]
    u = _dot(a, wo_ref[0:wa, :]) + _dot(mix_ref[...], wo_ref[wa:, :])
    x = x_ref[...] + mod_ref[2:3, :] * u
    h = (_rms(x) * g_ref[1:2, :] * (1.0 + mod_ref[4:5, :]) + mod_ref[3:4, :]).astype(BF16)
    act = (_silu(_dot(h, wg_ref[...])) * _dot(h, wu_ref[...])).astype(BF16)
    x = x + mod_ref[5:6, :] * _dot(act, wd_ref[...])
    o_ref[...] = _rms(x) * rest[0][...] if final else x


def _mix_ffn(x, o_f, o_b, gates, gate_block, mix, norm_gain, mod, norm_g, w_out, wg, wu, wd, layer, tm, n_ctx_tiles,
             final_g=None):
    b, t, d = x.shape
    wa = o_f.shape[2]
    final = final_g is not None
    off = n_ctx_tiles if final else 0
    n_rows = t - off * tm
    in_specs = [_rows(tm, d, off), _rows(tm, wa, off), _rows(tm, wa, off),
                pl.BlockSpec((None, tm, wa), lambda i, j: (i, j + off, gate_block)),
                _rows(tm, mix.shape[2], off), _resident(norm_gain), _mod_spec(layer, b, n_ctx_tiles, d, off),
                _norm_spec(layer, d), _resident(w_out), _resident(wg), _resident(wu), _resident(wd)]
    args = [x, o_f, o_b, gates, mix, norm_gain, mod, norm_g, w_out, wg, wu, wd]
    if final:
        in_specs.append(_resident(final_g))
        args.append(final_g)
    return pl.pallas_call(
        functools.partial(_mix_ffn_kernel, final=final),
        grid=(b, n_rows // tm),
        in_specs=in_specs,
        out_specs=_rows(tm, d),
        out_shape=jax.ShapeDtypeStruct((b, n_rows, d), F32),
        compiler_params=_params(2, VMEM_LIMIT),
        name="mix_ffn",
    )(*args)


def _rope_tables(ctx_len, seq, kind, scale=1.0):
    lane = np.arange(LANES)
    n = jnp.arange(seq, dtype=jnp.int32)
    if kind == "ret":
        half = RET_QK // 2
        idx = lane % RET_QK
        active = np.ones(LANES, bool)
        pos = jnp.broadcast_to(n[:, None], (seq, LANES))
    else:
        width = DIFF_QK if kind == "diff" else MLA_ROPE
        half = width // 4
        if kind == "diff":
            idx = lane % (width // 2)
            by_col = (lane % width) >= width // 2
            active = np.ones(LANES, bool)
        else:
            idx = (lane - MLA_NOPE) % (width // 2)
            by_col = (lane - MLA_NOPE) >= width // 2
            active = (lane >= MLA_NOPE) & (lane < MLA_NOPE + width)
        pos = jnp.where(jnp.asarray(by_col)[None, :], (n % GRID_W)[:, None], (n // GRID_W)[:, None])
    upper = idx >= half
    inv = ROPE_BASE ** (-jnp.asarray(idx % half, F32) / half)
    ang = pos.astype(F32) * inv[None, :]
    act = jnp.asarray(active)[None, :]
    up = jnp.asarray(upper)[None, :]
    cos = jnp.where(act, jnp.cos(ang), 1.0)
    sin = jnp.where(act, jnp.sin(ang), 0.0)
    lat = jnp.stack([cos, jnp.where(up, sin, 0.0), jnp.where(up, 0.0, -sin)])
    ident = jnp.stack([jnp.ones((ctx_len, LANES), F32), jnp.zeros((ctx_len, LANES), F32),
                       jnp.zeros((ctx_len, LANES), F32)])
    return jnp.concatenate([ident, lat], axis=1) * scale


def kernel(x, c, ctx, c_ctx, ada_w, ada_b, norm_g, mix_w_out, ffn_w_gate, ffn_w_up, ffn_w_down,
           even_w_in, hgrn_lb_logits, hgrn_norm_g, mla_q_norm_g, mla_w_uq, mla_kv_norm_g, mla_w_ukv,
           odd_w_in, ret_decay_logits, diff_lambda, diff_subln_g, final_norm_g):
    batch, seq, d = x.shape
    ctx_len = ctx.shape[1]
    depth = ada_w.shape[0]
    tm = math.gcd(ROW_TILE, math.gcd(ctx_len, seq))
    n_ctx_tiles = ctx_len // tm
    assert tm % (2 * CHUNK) == 0 and seq % GRID_W == 0

    mod_rows = -(-(batch + 1) // 8) * 8
    s_rows = jnp.concatenate([c, c_ctx[None, :], jnp.zeros((mod_rows - batch - 1, d), F32)], axis=0)
    mod = _modulation(s_rows, ada_w, ada_b).reshape(depth, mod_rows, 6, d)

    lb_soft = jax.nn.softmax(hgrn_lb_logits.astype(F32), axis=0)
    lbs = jnp.cumsum(lb_soft, axis=0) - lb_soft[0:1]

    tab_mla_q = _rope_tables(ctx_len, seq, "mla", (MLA_NOPE + MLA_ROPE) ** -0.5)
    tab_mla_k = _rope_tables(ctx_len, seq, "mla")
    tab_ret = _rope_tables(ctx_len, seq, "ret")
    tab_diff = _rope_tables(ctx_len, seq, "diff")

    stream = jnp.concatenate([ctx, x], axis=1)
    for layer in range(depth):
        if layer % 2 == 0:
            e = layer // 2
            w_in = jnp.pad(even_w_in[e], ((0, 0), (0, EV_COLS - even_w_in.shape[2]))).astype(BF16)
            wq = mla_w_uq[e].reshape(MLA_Q_RANK, MLA_HEADS, MLA_NOPE + MLA_ROPE)
            wq = jnp.pad(wq, ((0, 0), (0, 0), (0, LANES - MLA_NOPE - MLA_ROPE))).reshape(MLA_Q_RANK, -1)
            wkv = mla_w_ukv[e].reshape(MLA_KV_RANK, MLA_HEADS, MLA_NOPE + MLA_V)
            wk = jnp.pad(wkv[:, :, :MLA_NOPE], ((0, 0), (0, 0), (0, LANES - MLA_NOPE))).reshape(MLA_KV_RANK, -1)
            wv = wkv[:, :, MLA_NOPE:].reshape(MLA_KV_RANK, -1)
            gates, q, k, v = _even_in_projection(
                stream, mod, norm_g, w_in, mla_q_norm_g[e][None, :], mla_kv_norm_g[e][None, :], wq.astype(BF16),
                jnp.concatenate([wk, wv], axis=1).astype(BF16), tab_mla_q, tab_mla_k, layer, tm, n_ctx_tiles)
            o_f, o_b = _hgrn_scan(gates, lbs[e].reshape(2, 1, HGRN_WIDTH), tm, n_ctx_tiles)
            mix = _mla_attention(q, k, v, tm, ctx_len)
            gate_block, gain = 4, hgrn_norm_g[e][None, :]
        else:
            o = layer // 2
            qk_ret, gates, q_d, k_d, v_d = _odd_in_projection(stream, mod, norm_g, odd_w_in[o].astype(BF16),
                                                              tab_ret, tab_diff, layer, tm, n_ctx_tiles)
            decay = jnp.broadcast_to(ret_decay_logits[o][:, :, None, None], (2, RET_HEADS, 1, LANES))
            o_f, o_b = _ret_scan(qk_ret, gates, decay, tm, n_ctx_tiles)
            lambda_init = 0.8 - 0.6 * math.exp(-0.3 * layer)
            mix = _diff_attention(q_d, k_d, v_d, diff_lambda[o], diff_subln_g[o][None, :], tm, ctx_len, lambda_init)
            gate_block, gain = 1, jnp.ones((1, LANES), F32)
        stream = _mix_ffn(stream, o_f, o_b, gates, gate_block, mix, gain, mod, norm_g, mix_w_out[layer].astype(BF16),
                          ffn_w_gate[layer].astype(BF16), ffn_w_up[layer].astype(BF16),
                          ffn_w_down[layer].astype(BF16), layer, tm, n_ctx_tiles,
                          final_norm_g[None, :] if layer == depth - 1 else None)
    return stream
```

```python
import functools
import math

import numpy as np
import jax
import jax.numpy as jnp
from jax import lax
from jax.experimental import pallas as pl
from jax.experimental.pallas import tpu as pltpu

F32 = jnp.float32
BF16 = jnp.bfloat16

LANES = 128
GRID_W = 64
CHUNK = 64
ROPE_BASE = 10000.0
EPS = 1e-6

HGRN_HEADS, HGRN_DIM = 4, 128
HGRN_WIDTH = HGRN_HEADS * HGRN_DIM
MLA_HEADS, MLA_Q_RANK, MLA_KV_RANK, MLA_NOPE, MLA_ROPE, MLA_V = 8, 384, 256, 64, 32, 64
RET_HEADS, RET_QK, RET_V = 4, 64, 128
DIFF_HEADS, DIFF_QK, DIFF_V = 4, 64, 128
RET_QK_WIDTH = RET_HEADS * RET_QK
RET_V_WIDTH = RET_HEADS * RET_V
DIFF_QK_WIDTH = DIFF_HEADS * 2 * DIFF_QK
DIFF_V_WIDTH = DIFF_HEADS * DIFF_V

EV_HGRN = 5 * HGRN_WIDTH
EV_CQ = EV_HGRN
EV_CKV = EV_CQ + MLA_Q_RANK
EV_KROPE = EV_CKV + MLA_KV_RANK
EV_COLS = EV_KROPE + LANES
OD_KR = RET_QK_WIDTH
OD_VR = OD_KR + RET_QK_WIDTH
OD_QD = OD_VR + 2 * RET_V_WIDTH
OD_KD = OD_QD + DIFF_QK_WIDTH
OD_VD = OD_KD + DIFF_QK_WIDTH

ROW_TILE = 256
KEY_CHUNK = 768
VMEM_LIMIT = 48 * 1024 * 1024


def _params(n_axes, vmem=None):
    return pltpu.CompilerParams(dimension_semantics=("arbitrary",) * n_axes, vmem_limit_bytes=vmem)


def _resident(a):
    return pl.BlockSpec(a.shape, lambda *_: (0,) * a.ndim, pipeline_mode=pl.Buffered(1))


def _dot(a, b):
    return jnp.dot(a, b, preferred_element_type=F32)


def _dot_nt(a, b):
    return lax.dot_general(a, b, (((1,), (1,)), ((), ())), preferred_element_type=F32)


def _dot_tn(a, b):
    return lax.dot_general(a, b, (((0,), (0,)), ((), ())), preferred_element_type=F32)


def _rms(x):
    return x * lax.rsqrt(jnp.mean(x * x, axis=-1, keepdims=True) + EPS)


def _silu(x):
    return x * jax.nn.sigmoid(x)


def _log_sigmoid(z):
    return jnp.minimum(z, 0.0) - jnp.log1p(jnp.exp(-jnp.abs(z)))


def _rope(x, tab_ref, shift):
    return x * tab_ref[0] + pltpu.roll(x, shift, 1) * tab_ref[1] + pltpu.roll(x, LANES - shift, 1) * tab_ref[2]


def _lane_blocks(width):
    return [slice(i * LANES, (i + 1) * LANES) for i in range(width // LANES)]


def _mod_kernel(s_ref, w_ref, b_ref, o_ref):
    s = _silu(s_ref[...])
    o_ref[...] = jnp.dot(s, w_ref[...], precision=lax.Precision.HIGHEST, preferred_element_type=F32) + b_ref[...]


def _modulation(s_rows, ada_w, ada_b):
    depth, d, d6 = ada_w.shape
    rows = s_rows.shape[0]
    tn = d
    return pl.pallas_call(
        _mod_kernel,
        grid=(depth, d6 // tn),
        in_specs=[pl.BlockSpec((rows, d), lambda l, n: (0, 0)),
                  pl.BlockSpec((None, d, tn), lambda l, n: (l, 0, n)),
                  pl.BlockSpec((None, 1, tn), lambda l, n: (l, 0, n))],
        out_specs=pl.BlockSpec((None, rows, tn), lambda l, n: (l, 0, n)),
        out_shape=jax.ShapeDtypeStruct((depth, rows, d6), F32),
        compiler_params=_params(2),
        name="modulation",
    )(s_rows, ada_w, ada_b.reshape(depth, 1, d6))


def _mod_spec(layer, batch, n_ctx_tiles, d, row_off=0):
    return pl.BlockSpec((None, None, 6, d),
                        lambda b, t: (layer, jnp.where(t + row_off < n_ctx_tiles, batch, b), 0, 0))


def _norm_spec(layer, d):
    return pl.BlockSpec((None, 2, d), lambda i, j: (layer, 0, 0))


def _rows(tm, width, row_off=0):
    return pl.BlockSpec((None, tm, width), lambda i, j: (i, j + row_off, 0))


def _even_in_kernel(x_ref, mod_ref, g_ref, w_ref, gq_ref, gkv_ref, wq_ref, wkv_ref, tq_ref, tk_ref,
                    hg_ref, q_ref, k_ref, v_ref):
    shift = MLA_ROPE // 4
    h = _rms(x_ref[...]) * g_ref[0:1, :] * (1.0 + mod_ref[1:2, :]) + mod_ref[0:1, :]
    p = _dot(h.astype(BF16), w_ref[...])
    hg_ref[...] = p[:, :EV_HGRN].astype(BF16)
    q = _dot((_rms(p[:, EV_CQ:EV_CKV]) * gq_ref[...]).astype(BF16), wq_ref[...])
    kv = _dot((_rms(p[:, EV_CKV:EV_KROPE]) * gkv_ref[...]).astype(BF16), wkv_ref[...])
    k_rope = _rope(pltpu.roll(p[:, EV_KROPE:], MLA_NOPE, 1), tk_ref, shift)
    for cols in _lane_blocks(MLA_HEADS * LANES):
        q_ref[:, cols] = _rope(q[:, cols], tq_ref, shift).astype(BF16)
        k_ref[:, cols] = (kv[:, cols] + k_rope).astype(BF16)
    v_ref[...] = kv[:, MLA_HEADS * LANES:].astype(BF16)


def _even_in_projection(x, mod, norm_g, w, gq, gkv, wq, wkv, tab_q, tab_k, layer, tm, n_ctx_tiles):
    b, t, d = x.shape
    hq = MLA_HEADS * LANES
    tab = pl.BlockSpec((3, tm, LANES), lambda i, j: (0, j, 0))
    shapes = [(EV_HGRN, BF16), (hq, BF16), (hq, BF16), (MLA_HEADS * MLA_V, BF16)]
    return pl.pallas_call(
        _even_in_kernel,
        grid=(b, t // tm),
        in_specs=[_rows(tm, d), _mod_spec(layer, b, n_ctx_tiles, d), _norm_spec(layer, d), _resident(w),
                  _resident(gq), _resident(gkv), _resident(wq), _resident(wkv), tab, tab],
        out_specs=[_rows(tm, width) for width, _ in shapes],
        out_shape=[jax.ShapeDtypeStruct((b, t, width), dt) for width, dt in shapes],
        compiler_params=_params(2, VMEM_LIMIT),
        name="even_in_projection",
    )(x, mod, norm_g, w, gq, gkv, wq, wkv, tab_q, tab_k)


def _odd_in_kernel(x_ref, mod_ref, g_ref, w_ref, tr_ref, td_ref, qkr_ref, vg_ref, qd_ref, kd_ref, vd_ref):
    h = _rms(x_ref[...]) * g_ref[0:1, :] * (1.0 + mod_ref[1:2, :]) + mod_ref[0:1, :]
    p = _dot(h.astype(BF16), w_ref[...])
    for cols in _lane_blocks(RET_QK_WIDTH):
        qkr_ref[:, cols] = _rope(p[:, cols], tr_ref, RET_QK // 2)
        kc = slice(cols.start + OD_KR, cols.stop + OD_KR)
        qkr_ref[:, kc] = _rope(p[:, kc], tr_ref, RET_QK // 2) * (RET_QK ** -0.5)
    vg_ref[...] = p[:, OD_VR:OD_QD].astype(BF16)
    for cols in _lane_blocks(DIFF_QK_WIDTH):
        qc = slice(cols.start + OD_QD, cols.stop + OD_QD)
        kc = slice(cols.start + OD_KD, cols.stop + OD_KD)
        qd_ref[:, cols] = (_rope(p[:, qc], td_ref, DIFF_QK // 4) * (DIFF_QK ** -0.5)).astype(BF16)
        kd_ref[:, cols] = _rope(p[:, kc], td_ref, DIFF_QK // 4).astype(BF16)
    vd_ref[...] = p[:, OD_VD:].astype(BF16)


def _odd_in_projection(x, mod, norm_g, w, tab_ret, tab_diff, layer, tm, n_ctx_tiles):
    b, t, d = x.shape
    tab = pl.BlockSpec((3, tm, LANES), lambda i, j: (0, j, 0))
    shapes = [(2 * RET_QK_WIDTH, F32), (2 * RET_V_WIDTH, BF16), (DIFF_QK_WIDTH, BF16), (DIFF_QK_WIDTH, BF16),
              (DIFF_V_WIDTH, BF16)]
    return pl.pallas_call(
        _odd_in_kernel,
        grid=(b, t // tm),
        in_specs=[_rows(tm, d), _mod_spec(layer, b, n_ctx_tiles, d), _norm_spec(layer, d), _resident(w), tab, tab],
        out_specs=[_rows(tm, width) for width, _ in shapes],
        out_shape=[jax.ShapeDtypeStruct((b, t, width), dt) for width, dt in shapes],
        compiler_params=_params(2, VMEM_LIMIT),
        name="odd_in_projection",
    )(x, mod, norm_g, w, tab_ret, tab_diff)


_LEVELS = 6
HGRN_STEP_HEADS = 4


def _gla_tables():
    c = CHUNK
    m = np.zeros((_LEVELS + 2, c, c), np.float32)
    masks = np.zeros((_LEVELS + 1, c, c), np.float32)
    masks[0] = np.eye(c)
    for lv in range(_LEVELS):
        b = 32 >> lv
        for tau in range(c):
            mid = (tau // (2 * b)) * 2 * b + b
            if tau >= mid:
                m[lv, tau, mid:tau + 1] = 1.0
                masks[lv + 1, tau, mid - b:mid] = 1.0
            else:
                m[lv, tau, tau + 1:mid] = 1.0
    for tau in range(c):
        m[_LEVELS, tau, :tau + 1] = 1.0
        m[_LEVELS + 1, tau, tau + 1:] = 1.0
    flip = lambda a: a[:, ::-1, ::-1]
    mcat = np.stack([m, flip(m)]).reshape(2, (_LEVELS + 2) * c, c)
    mcat = np.concatenate([mcat] * 3, axis=2)
    mk = np.stack([masks, flip(masks)]).reshape(2, (_LEVELS + 1) * c, c)
    return jnp.asarray(mcat, BF16), jnp.asarray(mk, F32)


def _gla_gates(z, lb):
    k = (1.0 - lb) * jax.nn.sigmoid(-z)
    log_f = _log_sigmoid(z) + jnp.log1p(lb * jnp.exp(-z))
    g_hi = log_f.astype(BF16)
    r1 = log_f - g_hi.astype(F32)
    g_mid = r1.astype(BF16)
    g_lo = (r1 - g_mid.astype(F32)).astype(BF16)
    return k, jnp.concatenate([g_hi, g_mid, g_lo], axis=0)


def _gla_chunk(q, v, k, dec, masks, st, reverse):
    c = CHUNK
    qs = q * (HGRN_DIM ** -0.5)
    row = lax.broadcasted_iota(jnp.int32, (c, LANES), 0)
    tau = (c - 1 - row) if reverse else row
    att = masks[0:c] * _dot_nt(qs.astype(BF16), k.astype(BF16))
    for lv in range(_LEVELS):
        b = 32 >> lv
        zl = (jnp.where((tau & b) != 0, qs, k) * dec[lv * c:(lv + 1) * c]).astype(BF16)
        att = att + masks[(lv + 1) * c:(lv + 2) * c] * _dot_nt(zl, zl)
    dec_q = dec[_LEVELS * c:(_LEVELS + 1) * c]
    dec_k = dec[(_LEVELS + 1) * c:(_LEVELS + 2) * c]
    o = _dot(att.astype(BF16), v.astype(BF16)) + _dot_nt((qs * dec_q).astype(BF16), st.astype(BF16))
    last = 0 if reverse else c - 1
    st = st * dec_q[last:last + 1, :] + _dot_tn(v.astype(BF16), (k * dec_k).astype(BF16))
    return o, st


def _hgrn_kernel(qf_ref, if_ref, zf_ref, qb_ref, ib_ref, zb_ref, lbf_ref, lbb_ref, m_ref, mk_ref,
                 of_ref, ob_ref, sf_ref, sb_ref, *, n_chunks):
    @pl.when(pl.program_id(2) == 0)
    def _():
        sf_ref[...] = jnp.zeros_like(sf_ref)
        sb_ref[...] = jnp.zeros_like(sb_ref)

    dirs = ((qf_ref, if_ref, zf_ref, lbf_ref, of_ref, sf_ref), (qb_ref, ib_ref, zb_ref, lbb_ref, ob_ref, sb_ref))
    n_heads = sf_ref.shape[0]
    states = {(d, h): dirs[d][5][h] for d in range(2) for h in range(n_heads)}
    for step in range(0, n_chunks, 2):
        for d, (q_ref, i_ref, z_ref, lb_ref, o_ref, _) in enumerate(dirs):
            pair = [step, step + 1] if d == 0 else [n_chunks - 1 - step, n_chunks - 2 - step]
            for h, cols in enumerate(_lane_blocks(n_heads * LANES)):
                gates = [_gla_gates(z_ref[ci * CHUNK:(ci + 1) * CHUNK, cols].astype(F32), lb_ref[:, cols])
                         for ci in pair]
                dec = jnp.exp(_dot(m_ref[d], jnp.concatenate([g for _, g in gates], axis=1)))
                st = states[d, h]
                for n, ci in enumerate(pair):
                    rows = slice(ci * CHUNK, (ci + 1) * CHUNK)
                    o, st = _gla_chunk(q_ref[rows, cols].astype(F32), i_ref[rows, cols].astype(F32), gates[n][0],
                                       dec[:, n * LANES:(n + 1) * LANES], mk_ref[d], st, d == 1)
                    o_ref[rows, cols] = o
                states[d, h] = st
    for (d, h), st in states.items():
        dirs[d][5][h] = st


def _scan_block_maps(n_blocks, n_ctx_blocks):
    def bwd(j):
        return jnp.where(j < n_ctx_blocks, n_ctx_blocks - 1 - j, n_blocks - 1 - (j - n_ctx_blocks))
    return (lambda j: j), bwd


def _hgrn_scan(p, lbs, tb, n_ctx_blocks):
    b, t, _ = p.shape
    nb = t // tb
    fwd, bwd = _scan_block_maps(nb, n_ctx_blocks)
    mcat, masks = _gla_tables()

    hs = HGRN_STEP_HEADS
    width = hs * LANES
    groups = HGRN_HEADS // hs

    def col(group, order):
        return pl.BlockSpec((None, tb, width), lambda i, h, j: (i, order(j), group * groups + h))

    def lb_spec(d):
        return pl.BlockSpec((None, 1, width), lambda i, h, j: (d, 0, h))

    def out_spec(order):
        return pl.BlockSpec((None, tb, width), lambda i, h, j: (i, order(j), h))

    out = jax.ShapeDtypeStruct((b, t, HGRN_WIDTH), F32)
    state = pltpu.VMEM((hs, HGRN_DIM, HGRN_DIM), F32)
    return pl.pallas_call(
        functools.partial(_hgrn_kernel, n_chunks=tb // CHUNK),
        grid=(b, groups, nb),
        in_specs=[col(0, fwd), col(1, fwd), col(2, fwd), col(0, bwd), col(1, bwd), col(3, bwd),
                  lb_spec(0), lb_spec(1), _resident(mcat), _resident(masks)],
        out_specs=[out_spec(fwd), out_spec(bwd)],
        out_shape=[out, out],
        scratch_shapes=[state, state],
        compiler_params=_params(3),
        name="hgrn_scan",
    )(p, p, p, p, p, p, lbs, lbs, mcat, masks)


def _key_chunk(n_keys):
    return max(c for c in range(LANES, min(KEY_CHUNK, n_keys) + 1, LANES) if n_keys % c == 0)


def _score_steps(q, k_ref, k_cols, n_keys, s_ref, m_ref):
    kc = _key_chunk(n_keys)
    m = None
    for c in range(n_keys // kc):
        s = _dot_nt(q, k_ref[c * kc:(c + 1) * kc, k_cols])
        s_ref[:, c * kc:(c + 1) * kc] = s
        for blk in _lane_blocks(kc):
            m = s[:, blk] if m is None else jnp.maximum(m, s[:, blk])
        yield
    m_ref[...] = jnp.broadcast_to(jnp.max(m, axis=-1, keepdims=True), m.shape)


def _weight_steps(s_ref, m_ref, v_ref, n_keys, out):
    kc = _key_chunk(n_keys)
    m = m_ref[...]
    ones = jnp.ones((kc, LANES), BF16)
    acc = None
    for c in range(n_keys // kc):
        p = jnp.concatenate([jnp.exp(s_ref[:, c * kc + blk.start:c * kc + blk.stop] - m)
                             for blk in _lane_blocks(kc)], axis=1)
        part = _dot(p.astype(BF16), jnp.concatenate([v_ref[c * kc:(c + 1) * kc, :], ones], axis=1))
        acc = part if acc is None else acc + part
        yield
    out.append(acc)


def _interleave(*step_iters):
    live = list(step_iters)
    while live:
        for it in list(live):
            if next(it, StopIteration) is StopIteration:
                live.remove(it)


def _attention_tiles(n_streams, scores, weights, finish, tq, ctx_len, t):
    def weigh(buf, n_keys, row0, *extra):
        accs = []
        for s in range(n_streams):
            out = []
            _interleave(weights(s, buf, n_keys, out), *[e(s) for e in extra])
            accs.append(out[0])
        finish(row0, accs)

    for i in range(ctx_len // tq):
        for s in range(n_streams):
            _interleave(scores(s, i * tq, 0, ctx_len))
        weigh(0, ctx_len, i * tq)
    n_lat = (t - ctx_len) // tq
    for s in range(n_streams):
        _interleave(scores(s, ctx_len, 0, t))

    def next_scores(row0, buf):
        return lambda s: scores(s, row0, buf, t)

    def pair(i, carry):
        row0 = pl.multiple_of(ctx_len + 2 * i * tq, tq)
        weigh(0, t, row0, next_scores(row0 + tq, 1))
        weigh(1, t, row0 + tq, next_scores(row0 + 2 * tq, 0))
        return carry

    n_pairs = (n_lat - 1) // 2
    lax.fori_loop(0, n_pairs, pair, 0)
    row0 = ctx_len + 2 * n_pairs * tq
    if n_lat % 2 == 0:
        weigh(0, t, row0, next_scores(row0 + tq, 1))
        weigh(1, t, row0 + tq)
    else:
        weigh(0, t, row0)


def _tile_rows(row0, tq):
    return pl.ds(row0 if isinstance(row0, int) else pl.multiple_of(row0, tq), tq)


def _attention_scratch(n_streams, tq, t):
    scores = pltpu.VMEM((n_streams, tq, t), F32)
    row_max = pltpu.VMEM((n_streams, tq, LANES), F32)
    return [scores, scores, row_max, row_max]


def _mla_attn_kernel(q_ref, k_ref, v_ref, o_ref, s0_ref, s1_ref, m0_ref, m1_ref, *, tq, ctx_len):
    heads = _lane_blocks(2 * LANES)
    s_bufs, m_bufs = (s0_ref, s1_ref), (m0_ref, m1_ref)

    def scores(h, row0, buf, n_keys):
        return _score_steps(q_ref[_tile_rows(row0, tq), heads[h]], k_ref, heads[h], n_keys,
                            s_bufs[buf].at[h], m_bufs[buf].at[h])

    def weights(h, buf, n_keys, out):
        return _weight_steps(s_bufs[buf].at[h], m_bufs[buf].at[h], v_ref, n_keys, out)

    def finish(row0, accs):
        o = [a[:, :LANES] / a[:, LANES:] for a in accs]
        lane = lax.broadcasted_iota(jnp.int32, o[0].shape, 1)
        o_ref[_tile_rows(row0, tq), :] = jnp.where(lane < MLA_V, o[0], o[1]).astype(BF16)

    _attention_tiles(2, scores, weights, finish, tq, ctx_len, k_ref.shape[0])


def _mla_attention(q, k, v, tq, ctx_len):
    b, t, _ = q.shape
    return pl.pallas_call(
        functools.partial(_mla_attn_kernel, tq=tq, ctx_len=ctx_len),
        grid=(b, MLA_HEADS // 2),
        in_specs=[pl.BlockSpec((None, t, 2 * LANES), lambda i, h: (i, 0, h)),
                  pl.BlockSpec((None, t, 2 * LANES), lambda i, h: (i, 0, h)),
                  pl.BlockSpec((None, t, LANES), lambda i, h: (i, 0, h))],
        out_specs=pl.BlockSpec((None, t, LANES), lambda i, h: (i, 0, h)),
        out_shape=jax.ShapeDtypeStruct((b, t, MLA_HEADS * MLA_V), BF16),
        scratch_shapes=_attention_scratch(2, tq, t),
        compiler_params=_params(2, VMEM_LIMIT),
        name="mla_attention",
    )(q, k, v)


def _ret_chunk(q, k, v, lg, st, head_lanes, reverse):
    c = CHUNK
    row = lax.broadcasted_iota(jnp.int32, (c, LANES), 0)
    tau = ((c - 1 - row) if reverse else row).astype(F32)
    ti = lax.broadcasted_iota(jnp.int32, (c, c), 0)
    si = lax.broadcasted_iota(jnp.int32, (c, c), 1)
    rel = ((si - ti) if reverse else (ti - si)).astype(F32)
    dmat = jnp.where(rel >= 0, jnp.exp(jnp.maximum(rel, 0.0) * lg[:, :c]), 0.0)
    qm = jnp.where(head_lanes, q, 0.0)
    att = _dot_nt(qm.astype(BF16), k.astype(BF16)) * dmat
    q_dec = jnp.exp((tau + 1.0) * lg)
    k_dec = jnp.exp((c - 1.0 - tau) * lg)
    o = _dot(att.astype(BF16), v) + _dot_nt((qm * q_dec).astype(BF16), st.astype(BF16))
    st = st * jnp.exp(c * lg) + _dot_tn(v, (k * k_dec).astype(BF16))
    return o, st


def _ret_kernel(qf_ref, kf_ref, vf_ref, qb_ref, kb_ref, vb_ref, lgf_ref, lgb_ref, of_ref, ob_ref, sf_ref, sb_ref,
                *, n_chunks):
    @pl.when(pl.program_id(2) == 0)
    def _():
        sf_ref[...] = jnp.zeros_like(sf_ref)
        sb_ref[...] = jnp.zeros_like(sb_ref)

    lane = lax.broadcasted_iota(jnp.int32, (CHUNK, LANES), 1)
    head_lanes = (lane // RET_QK) == (pl.program_id(1) % (LANES // RET_QK))
    dirs = ((qf_ref, kf_ref, vf_ref, lgf_ref, of_ref, sf_ref), (qb_ref, kb_ref, vb_ref, lgb_ref, ob_ref, sb_ref))
    for d, (q_ref, k_ref, v_ref, lg_ref, o_ref, s_ref) in enumerate(dirs):
        lg = _log_sigmoid(lg_ref[...])
        st = s_ref[...]
        for ci in (range(n_chunks) if d == 0 else reversed(range(n_chunks))):
            rows = slice(ci * CHUNK, (ci + 1) * CHUNK)
            o, st = _ret_chunk(q_ref[rows, :], k_ref[rows, :], v_ref[rows, :], lg, st, head_lanes, d == 1)
            o_ref[rows, :] = o
        s_ref[...] = st


def _ret_scan(qk, vg, decay, tb, n_ctx_blocks):
    b, t, _ = qk.shape
    nb = t // tb
    fwd, bwd = _scan_block_maps(nb, n_ctx_blocks)
    per_block = LANES // RET_QK
    n_pairs = RET_HEADS // per_block

    def qk_spec(base, order):
        return pl.BlockSpec((None, tb, LANES), lambda i, h, j: (i, order(j), base + h // per_block))

    def v_spec(order):
        return pl.BlockSpec((None, tb, LANES), lambda i, h, j: (i, order(j), h))

    def lg_spec(d):
        return pl.BlockSpec((None, None, 1, LANES), lambda i, h, j: (d, h, 0, 0))

    def out_spec(order):
        return pl.BlockSpec((None, tb, LANES), lambda i, h, j: (i, order(j), h))

    out = jax.ShapeDtypeStruct((b, t, RET_V_WIDTH), F32)
    return pl.pallas_call(
        functools.partial(_ret_kernel, n_chunks=tb // CHUNK),
        grid=(b, RET_HEADS, nb),
        in_specs=[qk_spec(0, fwd), qk_spec(n_pairs, fwd), v_spec(fwd), qk_spec(0, bwd), qk_spec(n_pairs, bwd),
                  v_spec(bwd), lg_spec(0), lg_spec(1)],
        out_specs=[out_spec(fwd), out_spec(bwd)],
        out_shape=[out, out],
        scratch_shapes=[pltpu.VMEM((RET_V, LANES), F32), pltpu.VMEM((RET_V, LANES), F32)],
        compiler_params=_params(3),
        name="retention_scan",
    )(qk, qk, vg, qk, qk, vg, decay, decay)


def _diff_attn_kernel(q_ref, k_ref, v_ref, lam_ref, g_ref, o_ref, s0_ref, s1_ref, m0_ref, m1_ref,
                      *, tq, ctx_len, lambda_init):
    lp = lam_ref[...]
    lam = (jnp.exp(jnp.sum(lp[0:1] * lp[1:2], axis=-1, keepdims=True))
           - jnp.exp(jnp.sum(lp[2:3] * lp[3:4], axis=-1, keepdims=True)) + lambda_init)

    s_bufs, m_bufs = (s0_ref, s1_ref), (m0_ref, m1_ref)
    all_lanes = slice(0, LANES)

    def scores(i, row0, buf, n_keys):
        q = q_ref[_tile_rows(row0, tq), :]
        first = lax.broadcasted_iota(jnp.int32, q.shape, 1) < DIFF_QK
        q = jnp.where(first == (i == 0), q, jnp.zeros_like(q))
        return _score_steps(q, k_ref, all_lanes, n_keys, s_bufs[buf].at[i], m_bufs[buf].at[i])

    def weights(i, buf, n_keys, out):
        return _weight_steps(s_bufs[buf].at[i], m_bufs[buf].at[i], v_ref, n_keys, out)

    def finish(row0, accs):
        o = accs[0][:, :LANES] / accs[0][:, LANES:] - lam * (accs[1][:, :LANES] / accs[1][:, LANES:])
        o_ref[_tile_rows(row0, tq), :] = (_rms(o) * g_ref[...] * (1.0 - lambda_init)).astype(BF16)

    _attention_tiles(2, scores, weights, finish, tq, ctx_len, k_ref.shape[0])


def _diff_attention(q, k, v, lam_params, subln_g, tq, ctx_len, lambda_init):
    b, t, _ = q.shape
    head = pl.BlockSpec((None, t, LANES), lambda i, h: (i, 0, h))
    return pl.pallas_call(
        functools.partial(_diff_attn_kernel, tq=tq, ctx_len=ctx_len, lambda_init=lambda_init),
        grid=(b, DIFF_HEADS),
        in_specs=[head, head, head, _resident(lam_params), _resident(subln_g)],
        out_specs=head,
        out_shape=jax.ShapeDtypeStruct((b, t, DIFF_V_WIDTH), BF16),
        scratch_shapes=_attention_scratch(2, tq, t),
        compiler_params=_params(2, VMEM_LIMIT),
        name="diff_attention",
    )(q, k, v, lam_params, subln_g)


def _mix_ffn_kernel(x_ref, of_ref, ob_ref, gate_ref, mix_ref, ng_ref, mod_ref, g_ref, wo_ref, wg_ref, wu_ref, wd_ref,
                    *rest, final):
    o_ref = rest[-1]
    o = of_ref[...] + ob_ref[...]
    gate = gate_ref[...].astype(F32)
    a = jnp.concatenate([(_rms(o[:, cols]) * ng_ref[...] * _silu(gate[:, cols])).astype(BF16)
                         for cols in _lane_blocks(o.shape[1])], axis=1)
    wa = a.shape[1]
    u = _dot(a, wo_ref[0:wa, :]) + _dot(mix_ref[...], wo_ref[wa:, :])
    x = x_ref[...] + mod_ref[2:3, :] * u
    h = (_rms(x) * g_ref[1:2, :] * (1.0 + mod_ref[4:5, :]) + mod_ref[3:4, :]).astype(BF16)
    act = (_silu(_dot(h, wg_ref[...])) * _dot(h, wu_ref[...])).astype(BF16)
    x = x + mod_ref[5:6, :] * _dot(act, wd_ref[...])
    o_ref[...] = _rms(x) * rest[0][...] if final else x


def _mix_ffn(x, o_f, o_b, gates, gate_block, mix, norm_gain, mod, norm_g, w_out, wg, wu, wd, layer, tm, n_ctx_tiles,
             final_g=None):
    b, t, d = x.shape
    wa = o_f.shape[2]
    final = final_g is not None
    off = n_ctx_tiles if final else 0
    n_rows = t - off * tm
    in_specs = [_rows(tm, d, off), _rows(tm, wa, off), _rows(tm, wa, off),
                pl.BlockSpec((None, tm, wa), lambda i, j: (i, j + off, gate_block)),
                _rows(tm, mix.shape[2], off), _resident(norm_gain), _mod_spec(layer, b, n_ctx_tiles, d, off),
                _norm_spec(layer, d), _resident(w_out), _resident(wg), _resident(wu), _resident(wd)]
    args = [x, o_f, o_b, gates, mix, norm_gain, mod, norm_g, w_out, wg, wu, wd]
    if final:
        in_specs.append(_resident(final_g))
        args.append(final_g)
    return pl.pallas_call(
        functools.partial(_mix_ffn_kernel, final=final),
        grid=(b, n_rows // tm),
        in_specs=in_specs,
        out_specs=_rows(tm, d),
        out_shape=jax.ShapeDtypeStruct((b, n_rows, d), F32),
        compiler_params=_params(2, VMEM_LIMIT),
        name="mix_ffn",
    )(*args)


def _rope_tables(ctx_len, seq, kind, scale=1.0):
    lane = np.arange(LANES)
    n = jnp.arange(seq, dtype=jnp.int32)
    if kind == "ret":
        half = RET_QK // 2
        idx = lane % RET_QK
        active = np.ones(LANES, bool)
        pos = jnp.broadcast_to(n[:, None], (seq, LANES))
    else:
        width = DIFF_QK if kind == "diff" else MLA_ROPE
        half = width // 4
        if kind == "diff":
            idx = lane % (width // 2)
            by_col = (lane % width) >= width // 2
            active = np.ones(LANES, bool)
        else:
            idx = (lane - MLA_NOPE) % (width // 2)
            by_col = (lane - MLA_NOPE) >= width // 2
            active = (lane >= MLA_NOPE) & (lane < MLA_NOPE + width)
        pos = jnp.where(jnp.asarray(by_col)[None, :], (n % GRID_W)[:, None], (n // GRID_W)[:, None])
    upper = idx >= half
    inv = ROPE_BASE ** (-jnp.asarray(idx % half, F32) / half)
    ang = pos.astype(F32) * inv[None, :]
    act = jnp.asarray(active)[None, :]
    up = jnp.asarray(upper)[None, :]
    cos = jnp.where(act, jnp.cos(ang), 1.0)
    sin = jnp.where(act, jnp.sin(ang), 0.0)
    lat = jnp.stack([cos, jnp.where(up, sin, 0.0), jnp.where(up, 0.0, -sin)])
    ident = jnp.stack([jnp.ones((ctx_len, LANES), F32), jnp.zeros((ctx_len, LANES), F32),
                       jnp.zeros((ctx_len, LANES), F32)])
    return jnp.concatenate([ident, lat], axis=1) * scale


def kernel(x, c, ctx, c_ctx, ada_w, ada_b, norm_g, mix_w_out, ffn_w_gate, ffn_w_up, ffn_w_down,
           even_w_in, hgrn_lb_logits, hgrn_norm_g, mla_q_norm_g, mla_w_uq, mla_kv_norm_g, mla_w_ukv,
           odd_w_in, ret_decay_logits, diff_lambda, diff_subln_g, final_norm_g):
    batch, seq, d = x.shape
    ctx_len = ctx.shape[1]
    depth = ada_w.shape[0]
    tm = math.gcd(ROW_TILE, math.gcd(ctx_len, seq))
    n_ctx_tiles = ctx_len // tm
    assert tm % (2 * CHUNK) == 0 and seq % GRID_W == 0

    mod_rows = -(-(batch + 1) // 8) * 8
    s_rows = jnp.concatenate([c, c_ctx[None, :], jnp.zeros((mod_rows - batch - 1, d), F32)], axis=0)
    mod = _modulation(s_rows, ada_w, ada_b).reshape(depth, mod_rows, 6, d)

    lb_soft = jax.nn.softmax(hgrn_lb_logits.astype(F32), axis=0)
    lbs = jnp.cumsum(lb_soft, axis=0) - lb_soft[0:1]

    tab_mla_q = _rope_tables(ctx_len, seq, "mla", (MLA_NOPE + MLA_ROPE) ** -0.5)
    tab_mla_k = _rope_tables(ctx_len, seq, "mla")
    tab_ret = _rope_tables(ctx_len, seq, "ret")
    tab_diff = _rope_tables(ctx_len, seq, "diff")

    stream = jnp.concatenate([ctx, x], axis=1)
    for layer in range(depth):
        if layer % 2 == 0:
            e = layer // 2
            w_in = jnp.pad(even_w_in[e], ((0, 0), (0, EV_COLS - even_w_in.shape[2]))).astype(BF16)
            wq = mla_w_uq[e].reshape(MLA_Q_RANK, MLA_HEADS, MLA_NOPE + MLA_ROPE)
            wq = jnp.pad(wq, ((0, 0), (0, 0), (0, LANES - MLA_NOPE - MLA_ROPE))).reshape(MLA_Q_RANK, -1)
            wkv = mla_w_ukv[e].reshape(MLA_KV_RANK, MLA_HEADS, MLA_NOPE + MLA_V)
            wk = jnp.pad(wkv[:, :, :MLA_NOPE], ((0, 0), (0, 0), (0, LANES - MLA_NOPE))).reshape(MLA_KV_RANK, -1)
            wv = wkv[:, :, MLA_NOPE:].reshape(MLA_KV_RANK, -1)
            gates, q, k, v = _even_in_projection(
                stream, mod, norm_g, w_in, mla_q_norm_g[e][None, :], mla_kv_norm_g[e][None, :], wq.astype(BF16),
                jnp.concatenate([wk, wv], axis=1).astype(BF16), tab_mla_q, tab_mla_k, layer, tm, n_ctx_tiles)
            o_f, o_b = _hgrn_scan(gates, lbs[e].reshape(2, 1, HGRN_WIDTH), tm, n_ctx_tiles)
            mix = _mla_attention(q, k, v, tm, ctx_len)
            gate_block, gain = 4, hgrn_norm_g[e][None, :]
        else:
            o = layer // 2
            qk_ret, gates, q_d, k_d, v_d = _odd_in_projection(stream, mod, norm_g, odd_w_in[o].astype(BF16),
                                                              tab_ret, tab_diff, layer, tm, n_ctx_tiles)
            decay = jnp.broadcast_to(ret_decay_logits[o][:, :, None, None], (2, RET_HEADS, 1, LANES))
            o_f, o_b = _ret_scan(qk_ret, gates, decay, tm, n_ctx_tiles)
            lambda_init = 0.8 - 0.6 * math.exp(-0.3 * layer)
            mix = _diff_attention(q_d, k_d, v_d, diff_lambda[o], diff_subln_g[o][None, :], tm, ctx_len, lambda_init)
            gate_block, gain = 1, jnp.ones((1, LANES), F32)
        stream = _mix_ffn(stream, o_f, o_b, gates, gate_block, mix, gain, mod, norm_g, mix_w_out[layer].astype(BF16),
                          ffn_w_gate[layer].astype(BF16), ffn_w_up[layer].astype(BF16),
                          ffn_w_down[layer].astype(BF16), layer, tm, n_ctx_tiles,
                          final_norm_g[None, :] if layer == depth - 1 else None)
    return stream
```

```python
import functools
import math

import numpy as np
import jax
import jax.numpy as jnp
from jax import lax
from jax.experimental import pallas as pl
from jax.experimental.pallas import tpu as pltpu

F32 = jnp.float32
BF16 = jnp.bfloat16

LANES = 128
GRID_W = 64
CHUNK = 64
ROPE_BASE = 10000.0
EPS = 1e-6

HGRN_HEADS, HGRN_DIM = 4, 128
HGRN_WIDTH = HGRN_HEADS * HGRN_DIM
MLA_HEADS, MLA_Q_RANK, MLA_KV_RANK, MLA_NOPE, MLA_ROPE, MLA_V = 8, 384, 256, 64, 32, 64
RET_HEADS, RET_QK, RET_V = 4, 64, 128
DIFF_HEADS, DIFF_QK, DIFF_V = 4, 64, 128
RET_QK_WIDTH = RET_HEADS * RET_QK
RET_V_WIDTH = RET_HEADS * RET_V
DIFF_QK_WIDTH = DIFF_HEADS * 2 * DIFF_QK
DIFF_V_WIDTH = DIFF_HEADS * DIFF_V

EV_HGRN = 5 * HGRN_WIDTH
EV_CQ = 0
EV_CKV = EV_CQ + MLA_Q_RANK
EV_KROPE = EV_CKV + MLA_KV_RANK
EV_HGRN0 = EV_KROPE + LANES
OD_KR = RET_QK_WIDTH
OD_VR = OD_KR + RET_QK_WIDTH
OD_QD = OD_VR + 2 * RET_V_WIDTH
OD_KD = OD_QD + DIFF_QK_WIDTH
OD_VD = OD_KD + DIFF_QK_WIDTH

ROW_TILE = 256
KEY_CHUNK = 768
VMEM_LIMIT = 48 * 1024 * 1024


def _params(n_axes, vmem=None):
    return pltpu.CompilerParams(dimension_semantics=("arbitrary",) * n_axes, vmem_limit_bytes=vmem)


def _resident(a):
    return pl.BlockSpec(a.shape, lambda *_: (0,) * a.ndim, pipeline_mode=pl.Buffered(1))


def _dot(a, b):
    return jnp.dot(a, b, preferred_element_type=F32)


def _dot_nt(a, b):
    return lax.dot_general(a, b, (((1,), (1,)), ((), ())), preferred_element_type=F32)


def _dot_tn(a, b):
    return lax.dot_general(a, b, (((0,), (0,)), ((), ())), preferred_element_type=F32)


def _rms(x):
    return x * lax.rsqrt(jnp.mean(x * x, axis=-1, keepdims=True) + EPS)


def _silu(x):
    return x * jax.nn.sigmoid(x)


def _log_sigmoid(z):
    return jnp.minimum(z, 0.0) - jnp.log1p(jnp.exp(-jnp.abs(z)))


def _rope(x, tab_ref, shift):
    return x * tab_ref[0] + pltpu.roll(x, shift, 1) * tab_ref[1] + pltpu.roll(x, LANES - shift, 1) * tab_ref[2]


def _lane_blocks(width):
    return [slice(i * LANES, (i + 1) * LANES) for i in range(width // LANES)]


def _mod_kernel(s_ref, w_ref, b_ref, o_ref):
    s = _silu(s_ref[...])
    o_ref[...] = jnp.dot(s, w_ref[...], precision=lax.Precision.HIGHEST, preferred_element_type=F32) + b_ref[...]


def _modulation(s_rows, ada_w, ada_b):
    depth, d, d6 = ada_w.shape
    rows = s_rows.shape[0]
    tn = d
    return pl.pallas_call(
        _mod_kernel,
        grid=(depth, d6 // tn),
        in_specs=[pl.BlockSpec((rows, d), lambda l, n: (0, 0)),
                  pl.BlockSpec((None, d, tn), lambda l, n: (l, 0, n)),
                  pl.BlockSpec((None, 1, tn), lambda l, n: (l, 0, n))],
        out_specs=pl.BlockSpec((None, rows, tn), lambda l, n: (l, 0, n)),
        out_shape=jax.ShapeDtypeStruct((depth, rows, d6), F32),
        compiler_params=_params(2),
        name="modulation",
    )(s_rows, ada_w, ada_b.reshape(depth, 1, d6))


def _mod_spec(layer, batch, n_ctx_tiles, d, row_off=0):
    return pl.BlockSpec((None, None, 6, d),
                        lambda b, t: (layer, jnp.where(t + row_off < n_ctx_tiles, batch, b), 0, 0))


def _norm_spec(layer, d):
    return pl.BlockSpec((None, 2, d), lambda i, j: (layer, 0, 0))


def _rows(tm, width, row_off=0):
    return pl.BlockSpec((None, tm, width), lambda i, j: (i, j + row_off, 0))


def _even_in_kernel(x_ref, mod_ref, g_ref, w_ref, gq_ref, gkv_ref, wq_ref, wkv_ref, tq_ref, tk_ref,
                    hg_ref, q_ref, k_ref, v_ref):
    shift = MLA_ROPE // 4
    h = (_rms(x_ref[...]) * g_ref[0:1, :] * (1.0 + mod_ref[1:2, :]) + mod_ref[0:1, :]).astype(BF16)
    p = _dot(h, w_ref[:, :EV_HGRN0])
    hg_ref[...] = _dot(h, w_ref[:, EV_HGRN0:]).astype(BF16)
    q = _dot((_rms(p[:, EV_CQ:EV_CKV]) * gq_ref[...]).astype(BF16), wq_ref[...])
    kv = _dot((_rms(p[:, EV_CKV:EV_KROPE]) * gkv_ref[...]).astype(BF16), wkv_ref[...])
    k_rope = _rope(pltpu.roll(p[:, EV_KROPE:], MLA_NOPE, 1), tk_ref, shift)
    for cols in _lane_blocks(MLA_HEADS * LANES):
        q_ref[:, cols] = _rope(q[:, cols], tq_ref, shift).astype(BF16)
        k_ref[:, cols] = (kv[:, cols] + k_rope).astype(BF16)
    v_ref[...] = kv[:, MLA_HEADS * LANES:].astype(BF16)


def _even_in_projection(x, mod, norm_g, w, gq, gkv, wq, wkv, tab_q, tab_k, layer, tm, n_ctx_tiles):
    b, t, d = x.shape
    hq = MLA_HEADS * LANES
    tab = pl.BlockSpec((3, tm, LANES), lambda i, j: (0, j, 0))
    shapes = [(EV_HGRN, BF16), (hq, BF16), (hq, BF16), (MLA_HEADS * MLA_V, BF16)]
    return pl.pallas_call(
        _even_in_kernel,
        grid=(b, t // tm),
        in_specs=[_rows(tm, d), _mod_spec(layer, b, n_ctx_tiles, d), _norm_spec(layer, d), _resident(w),
                  _resident(gq), _resident(gkv), _resident(wq), _resident(wkv), tab, tab],
        out_specs=[_rows(tm, width) for width, _ in shapes],
        out_shape=[jax.ShapeDtypeStruct((b, t, width), dt) for width, dt in shapes],
        compiler_params=_params(2, VMEM_LIMIT),
        name="even_in_projection",
    )(x, mod, norm_g, w, gq, gkv, wq, wkv, tab_q, tab_k)


def _odd_in_kernel(x_ref, mod_ref, g_ref, w_ref, tr_ref, td_ref, qkr_ref, vg_ref, qd_ref, kd_ref, vd_ref):
    h = _rms(x_ref[...]) * g_ref[0:1, :] * (1.0 + mod_ref[1:2, :]) + mod_ref[0:1, :]
    p = _dot(h.astype(BF16), w_ref[...])
    for cols in _lane_blocks(RET_QK_WIDTH):
        qkr_ref[:, cols] = _rope(p[:, cols], tr_ref, RET_QK // 2)
        kc = slice(cols.start + OD_KR, cols.stop + OD_KR)
        qkr_ref[:, kc] = _rope(p[:, kc], tr_ref, RET_QK // 2) * (RET_QK ** -0.5)
    vg_ref[...] = p[:, OD_VR:OD_QD].astype(BF16)
    for cols in _lane_blocks(DIFF_QK_WIDTH):
        qc = slice(cols.start + OD_QD, cols.stop + OD_QD)
        kc = slice(cols.start + OD_KD, cols.stop + OD_KD)
        qd_ref[:, cols] = (_rope(p[:, qc], td_ref, DIFF_QK // 4) * (DIFF_QK ** -0.5)).astype(BF16)
        kd_ref[:, cols] = _rope(p[:, kc], td_ref, DIFF_QK // 4).astype(BF16)
    vd_ref[...] = p[:, OD_VD:].astype(BF16)


def _odd_in_projection(x, mod, norm_g, w, tab_ret, tab_diff, layer, tm, n_ctx_tiles):
    b, t, d = x.shape
    tab = pl.BlockSpec((3, tm, LANES), lambda i, j: (0, j, 0))
    shapes = [(2 * RET_QK_WIDTH, F32), (2 * RET_V_WIDTH, BF16), (DIFF_QK_WIDTH, BF16), (DIFF_QK_WIDTH, BF16),
              (DIFF_V_WIDTH, BF16)]
    return pl.pallas_call(
        _odd_in_kernel,
        grid=(b, t // tm),
        in_specs=[_rows(tm, d), _mod_spec(layer, b, n_ctx_tiles, d), _norm_spec(layer, d), _resident(w), tab, tab],
        out_specs=[_rows(tm, width) for width, _ in shapes],
        out_shape=[jax.ShapeDtypeStruct((b, t, width), dt) for width, dt in shapes],
        compiler_params=_params(2, VMEM_LIMIT),
        name="odd_in_projection",
    )(x, mod, norm_g, w, tab_ret, tab_diff)


_LEVELS = 6
HGRN_STEP_HEADS = 4


def _gla_tables():
    c = CHUNK
    m = np.zeros((_LEVELS + 2, c, c), np.float32)
    masks = np.zeros((_LEVELS + 1, c, c), np.float32)
    masks[0] = np.eye(c)
    for lv in range(_LEVELS):
        b = 32 >> lv
        for tau in range(c):
            mid = (tau // (2 * b)) * 2 * b + b
            if tau >= mid:
                m[lv, tau, mid:tau + 1] = 1.0
                masks[lv + 1, tau, mid - b:mid] = 1.0
            else:
                m[lv, tau, tau + 1:mid] = 1.0
    for tau in range(c):
        m[_LEVELS, tau, :tau + 1] = 1.0
        m[_LEVELS + 1, tau, tau + 1:] = 1.0
    flip = lambda a: a[:, ::-1, ::-1]
    mcat = np.stack([m, flip(m)]).reshape(2, (_LEVELS + 2) * c, c)
    mcat = np.concatenate([mcat] * 3, axis=2)
    mk = np.stack([masks, flip(masks)]).reshape(2, (_LEVELS + 1) * c, c)
    return jnp.asarray(mcat, BF16), jnp.asarray(mk, F32)


def _gla_gates(z, lb):
    k = (1.0 - lb) * jax.nn.sigmoid(-z)
    log_f = _log_sigmoid(z) + jnp.log1p(lb * jnp.exp(-z))
    g_hi = log_f.astype(BF16)
    r1 = log_f - g_hi.astype(F32)
    g_mid = r1.astype(BF16)
    g_lo = (r1 - g_mid.astype(F32)).astype(BF16)
    return k, jnp.concatenate([g_hi, g_mid, g_lo], axis=0)


def _gla_chunk(q, v, k, dec, masks, st, reverse):
    c = CHUNK
    qs = q * (HGRN_DIM ** -0.5)
    row = lax.broadcasted_iota(jnp.int32, (c, LANES), 0)
    tau = (c - 1 - row) if reverse else row
    att = masks[0:c] * _dot_nt(qs.astype(BF16), k.astype(BF16))
    for lv in range(_LEVELS):
        b = 32 >> lv
        zl = (jnp.where((tau & b) != 0, qs, k) * dec[lv * c:(lv + 1) * c]).astype(BF16)
        att = att + masks[(lv + 1) * c:(lv + 2) * c] * _dot_nt(zl, zl)
    dec_q = dec[_LEVELS * c:(_LEVELS + 1) * c]
    dec_k = dec[(_LEVELS + 1) * c:(_LEVELS + 2) * c]
    o = _dot(att.astype(BF16), v.astype(BF16)) + _dot_nt((qs * dec_q).astype(BF16), st.astype(BF16))
    last = 0 if reverse else c - 1
    st = st * dec_q[last:last + 1, :] + _dot_tn(v.astype(BF16), (k * dec_k).astype(BF16))
    return o, st


def _hgrn_kernel(qf_ref, if_ref, zf_ref, qb_ref, ib_ref, zb_ref, lbf_ref, lbb_ref, m_ref, mk_ref,
                 of_ref, ob_ref, sf_ref, sb_ref, *, n_chunks):
    @pl.when(pl.program_id(2) == 0)
    def _():
        sf_ref[...] = jnp.zeros_like(sf_ref)
        sb_ref[...] = jnp.zeros_like(sb_ref)

    dirs = ((qf_ref, if_ref, zf_ref, lbf_ref, of_ref, sf_ref), (qb_ref, ib_ref, zb_ref, lbb_ref, ob_ref, sb_ref))
    n_heads = sf_ref.shape[0]
    states = {(d, h): dirs[d][5][h] for d in range(2) for h in range(n_heads)}
    for step in range(0, n_chunks, 2):
        for d, (q_ref, i_ref, z_ref, lb_ref, o_ref, _) in enumerate(dirs):
            pair = [step, step + 1] if d == 0 else [n_chunks - 1 - step, n_chunks - 2 - step]
            for h, cols in enumerate(_lane_blocks(n_heads * LANES)):
                gates = [_gla_gates(z_ref[ci * CHUNK:(ci + 1) * CHUNK, cols].astype(F32), lb_ref[:, cols])
                         for ci in pair]
                dec = jnp.exp(_dot(m_ref[d], jnp.concatenate([g for _, g in gates], axis=1)))
                st = states[d, h]
                for n, ci in enumerate(pair):
                    rows = slice(ci * CHUNK, (ci + 1) * CHUNK)
                    o, st = _gla_chunk(q_ref[rows, cols].astype(F32), i_ref[rows, cols].astype(F32), gates[n][0],
                                       dec[:, n * LANES:(n + 1) * LANES], mk_ref[d], st, d == 1)
                    o_ref[rows, cols] = o
                states[d, h] = st
    for (d, h), st in states.items():
        dirs[d][5][h] = st


def _scan_block_maps(n_blocks, n_ctx_blocks):
    def bwd(j):
        return jnp.where(j < n_ctx_blocks, n_ctx_blocks - 1 - j, n_blocks - 1 - (j - n_ctx_blocks))
    return (lambda j: j), bwd


def _hgrn_scan(p, lbs, tb, n_ctx_blocks):
    b, t, _ = p.shape
    nb = t // tb
    fwd, bwd = _scan_block_maps(nb, n_ctx_blocks)
    mcat, masks = _gla_tables()

    hs = HGRN_STEP_HEADS
    width = hs * LANES
    groups = HGRN_HEADS // hs

    def col(group, order):
        return pl.BlockSpec((None, tb, width), lambda i, h, j: (i, order(j), group * groups + h))

    def lb_spec(d):
        return pl.BlockSpec((None, 1, width), lambda i, h, j: (d, 0, h))

    def out_spec(order):
        return pl.BlockSpec((None, tb, width), lambda i, h, j: (i, order(j), h))

    out = jax.ShapeDtypeStruct((b, t, HGRN_WIDTH), F32)
    state = pltpu.VMEM((hs, HGRN_DIM, HGRN_DIM), F32)
    return pl.pallas_call(
        functools.partial(_hgrn_kernel, n_chunks=tb // CHUNK),
        grid=(b, groups, nb),
        in_specs=[col(0, fwd), col(1, fwd), col(2, fwd), col(0, bwd), col(1, bwd), col(3, bwd),
                  lb_spec(0), lb_spec(1), _resident(mcat), _resident(masks)],
        out_specs=[out_spec(fwd), out_spec(bwd)],
        out_shape=[out, out],
        scratch_shapes=[state, state],
        compiler_params=_params(3),
        name="hgrn_scan",
    )(p, p, p, p, p, p, lbs, lbs, mcat, masks)


def _key_chunk(n_keys):
    return max(c for c in range(LANES, min(KEY_CHUNK, n_keys) + 1, LANES) if n_keys % c == 0)


def _score_steps(q, k_ref, k_cols, n_keys, s_ref, m_ref):
    kc = _key_chunk(n_keys)
    m = None
    for c in range(n_keys // kc):
        s = _dot_nt(q, k_ref[c * kc:(c + 1) * kc, k_cols])
        s_ref[:, c * kc:(c + 1) * kc] = s
        for blk in _lane_blocks(kc):
            m = s[:, blk] if m is None else jnp.maximum(m, s[:, blk])
        yield
    m_ref[...] = jnp.broadcast_to(jnp.max(m, axis=-1, keepdims=True), m.shape)


def _weight_steps(s_ref, m_ref, v_ref, n_keys, out):
    kc = _key_chunk(n_keys)
    m = m_ref[...]
    ones = jnp.ones((kc, LANES), BF16)
    acc = None
    for c in range(n_keys // kc):
        p = jnp.concatenate([jnp.exp(s_ref[:, c * kc + blk.start:c * kc + blk.stop] - m)
                             for blk in _lane_blocks(kc)], axis=1)
        part = _dot(p.astype(BF16), jnp.concatenate([v_ref[c * kc:(c + 1) * kc, :], ones], axis=1))
        acc = part if acc is None else acc + part
        yield
    out.append(acc)


def _interleave(*step_iters):
    live = list(step_iters)
    while live:
        for it in list(live):
            if next(it, StopIteration) is StopIteration:
                live.remove(it)


def _attention_tiles(n_streams, scores, weights, finish, tq, ctx_len, t):
    def weigh(buf, n_keys, row0, *extra):
        accs = []
        for s in range(n_streams):
            out = []
            _interleave(weights(s, buf, n_keys, out), *[e(s) for e in extra])
            accs.append(out[0])
        finish(row0, accs)

    for i in range(ctx_len // tq):
        for s in range(n_streams):
            _interleave(scores(s, i * tq, 0, ctx_len))
        weigh(0, ctx_len, i * tq)
    n_lat = (t - ctx_len) // tq
    for s in range(n_streams):
        _interleave(scores(s, ctx_len, 0, t))

    def next_scores(row0, buf):
        return lambda s: scores(s, row0, buf, t)

    def pair(i, carry):
        row0 = pl.multiple_of(ctx_len + 2 * i * tq, tq)
        weigh(0, t, row0, next_scores(row0 + tq, 1))
        weigh(1, t, row0 + tq, next_scores(row0 + 2 * tq, 0))
        return carry

    n_pairs = (n_lat - 1) // 2
    lax.fori_loop(0, n_pairs, pair, 0)
    row0 = ctx_len + 2 * n_pairs * tq
    if n_lat % 2 == 0:
        weigh(0, t, row0, next_scores(row0 + tq, 1))
        weigh(1, t, row0 + tq)
    else:
        weigh(0, t, row0)


def _tile_rows(row0, tq):
    return pl.ds(row0 if isinstance(row0, int) else pl.multiple_of(row0, tq), tq)


def _attention_scratch(n_streams, tq, t):
    scores = pltpu.VMEM((n_streams, tq, t), F32)
    row_max = pltpu.VMEM((n_streams, tq, LANES), F32)
    return [scores, scores, row_max, row_max]


def _mla_attn_kernel(q_ref, k_ref, v_ref, o_ref, s0_ref, s1_ref, m0_ref, m1_ref, *, tq, ctx_len):
    heads = _lane_blocks(2 * LANES)
    s_bufs, m_bufs = (s0_ref, s1_ref), (m0_ref, m1_ref)

    def scores(h, row0, buf, n_keys):
        return _score_steps(q_ref[_tile_rows(row0, tq), heads[h]], k_ref, heads[h], n_keys,
                            s_bufs[buf].at[h], m_bufs[buf].at[h])

    def weights(h, buf, n_keys, out):
        return _weight_steps(s_bufs[buf].at[h], m_bufs[buf].at[h], v_ref, n_keys, out)

    def finish(row0, accs):
        o = [a[:, :LANES] / a[:, LANES:] for a in accs]
        lane = lax.broadcasted_iota(jnp.int32, o[0].shape, 1)
        o_ref[_tile_rows(row0, tq), :] = jnp.where(lane < MLA_V, o[0], o[1]).astype(BF16)

    _attention_tiles(2, scores, weights, finish, tq, ctx_len, k_ref.shape[0])


def _mla_attention(q, k, v, tq, ctx_len):
    b, t, _ = q.shape
    return pl.pallas_call(
        functools.partial(_mla_attn_kernel, tq=tq, ctx_len=ctx_len),
        grid=(b, MLA_HEADS // 2),
        in_specs=[pl.BlockSpec((None, t, 2 * LANES), lambda i, h: (i, 0, h)),
                  pl.BlockSpec((None, t, 2 * LANES), lambda i, h: (i, 0, h)),
                  pl.BlockSpec((None, t, LANES), lambda i, h: (i, 0, h))],
        out_specs=pl.BlockSpec((None, t, LANES), lambda i, h: (i, 0, h)),
        out_shape=jax.ShapeDtypeStruct((b, t, MLA_HEADS * MLA_V), BF16),
        scratch_shapes=_attention_scratch(2, tq, t),
        compiler_params=_params(2, VMEM_LIMIT),
        name="mla_attention",
    )(q, k, v)


def _ret_chunk(q, k, v, decays, st, head_lanes):
    dmat, q_dec, k_dec, c_dec = decays
    qm = jnp.where(head_lanes, q, 0.0)
    att = _dot_nt(qm.astype(BF16), k.astype(BF16)) * dmat
    o = _dot(att.astype(BF16), v) + _dot_nt((qm * q_dec).astype(BF16), st.astype(BF16))
    st = st * c_dec + _dot_tn(v, (k * k_dec).astype(BF16))
    return o, st


def _ret_decays(lg, reverse):
    c = CHUNK
    row = lax.broadcasted_iota(jnp.int32, (c, LANES), 0)
    tau = ((c - 1 - row) if reverse else row).astype(F32)
    ti = lax.broadcasted_iota(jnp.int32, (c, c), 0)
    si = lax.broadcasted_iota(jnp.int32, (c, c), 1)
    rel = ((si - ti) if reverse else (ti - si)).astype(F32)
    dmat = jnp.where(rel >= 0, jnp.exp(jnp.maximum(rel, 0.0) * lg[:, :c]), 0.0)
    return dmat, jnp.exp((tau + 1.0) * lg), jnp.exp((c - 1.0 - tau) * lg), jnp.exp(c * lg)


def _ret_kernel(qkf_ref, vf_ref, qkb_ref, vb_ref, lg_ref, of_ref, ob_ref, sf_ref, sb_ref, *, n_chunks):
    @pl.when(pl.program_id(1) == 0)
    def _():
        sf_ref[...] = jnp.zeros_like(sf_ref)
        sb_ref[...] = jnp.zeros_like(sb_ref)

    per_block = LANES // RET_QK
    lane = lax.broadcasted_iota(jnp.int32, (CHUNK, LANES), 1)
    dirs = ((qkf_ref, vf_ref, of_ref, sf_ref), (qkb_ref, vb_ref, ob_ref, sb_ref))
    chains = [(d, h) for d in range(2) for h in range(RET_HEADS)]
    decays = {(d, h): _ret_decays(_log_sigmoid(lg_ref[d, h]), d == 1) for d, h in chains}
    states = {(d, h): dirs[d][3][h] for d, h in chains}
    for step in range(n_chunks):
        for d, h in chains:
            qk_ref, v_ref, o_ref, _ = dirs[d]
            ci = step if d == 0 else n_chunks - 1 - step
            rows = slice(ci * CHUNK, (ci + 1) * CHUNK)
            q_cols = _lane_blocks(2 * RET_QK_WIDTH)[h // per_block]
            k_cols = _lane_blocks(2 * RET_QK_WIDTH)[RET_QK_WIDTH // LANES + h // per_block]
            v_cols = _lane_blocks(RET_V_WIDTH)[h]
            o, states[d, h] = _ret_chunk(qk_ref[rows, q_cols], qk_ref[rows, k_cols], v_ref[rows, v_cols],
                                         decays[d, h], states[d, h], (lane // RET_QK) == (h % per_block))
            o_ref[rows, v_cols] = o
    for (d, h), st in states.items():
        dirs[d][3][h] = st


def _ret_scan(qk, vg, decay, tb, n_ctx_blocks):
    b, t, _ = qk.shape
    nb = t // tb
    fwd, bwd = _scan_block_maps(nb, n_ctx_blocks)

    def rows(width, order):
        return pl.BlockSpec((None, tb, width), lambda i, j: (i, order(j), 0))

    out = jax.ShapeDtypeStruct((b, t, RET_V_WIDTH), F32)
    state = pltpu.VMEM((RET_HEADS, RET_V, LANES), F32)
    return pl.pallas_call(
        functools.partial(_ret_kernel, n_chunks=tb // CHUNK),
        grid=(b, nb),
        in_specs=[rows(2 * RET_QK_WIDTH, fwd), rows(RET_V_WIDTH, fwd), rows(2 * RET_QK_WIDTH, bwd),
                  rows(RET_V_WIDTH, bwd), _resident(decay)],
        out_specs=[rows(RET_V_WIDTH, fwd), rows(RET_V_WIDTH, bwd)],
        out_shape=[out, out],
        scratch_shapes=[state, state],
        compiler_params=_params(2),
        name="retention_scan",
    )(qk, vg, qk, vg, decay)


def _diff_attn_kernel(q_ref, k_ref, v_ref, lam_ref, g_ref, o_ref, s0_ref, s1_ref, m0_ref, m1_ref,
                      *, tq, ctx_len, lambda_init):
    lp = lam_ref[...]
    lam = (jnp.exp(jnp.sum(lp[0:1] * lp[1:2], axis=-1, keepdims=True))
           - jnp.exp(jnp.sum(lp[2:3] * lp[3:4], axis=-1, keepdims=True)) + lambda_init)

    s_bufs, m_bufs = (s0_ref, s1_ref), (m0_ref, m1_ref)
    all_lanes = slice(0, LANES)

    def scores(i, row0, buf, n_keys):
        q = q_ref[_tile_rows(row0, tq), :]
        first = lax.broadcasted_iota(jnp.int32, q.shape, 1) < DIFF_QK
        q = jnp.where(first == (i == 0), q, jnp.zeros_like(q))
        return _score_steps(q, k_ref, all_lanes, n_keys, s_bufs[buf].at[i], m_bufs[buf].at[i])

    def weights(i, buf, n_keys, out):
        return _weight_steps(s_bufs[buf].at[i], m_bufs[buf].at[i], v_ref, n_keys, out)

    def finish(row0, accs):
        o = accs[0][:, :LANES] / accs[0][:, LANES:] - lam * (accs[1][:, :LANES] / accs[1][:, LANES:])
        o_ref[_tile_rows(row0, tq), :] = (_rms(o) * g_ref[...] * (1.0 - lambda_init)).astype(BF16)

    _attention_tiles(2, scores, weights, finish, tq, ctx_len, k_ref.shape[0])


def _diff_attention(q, k, v, lam_params, subln_g, tq, ctx_len, lambda_init):
    b, t, _ = q.shape
    head = pl.BlockSpec((None, t, LANES), lambda i, h: (i, 0, h))
    return pl.pallas_call(
        functools.partial(_diff_attn_kernel, tq=tq, ctx_len=ctx_len, lambda_init=lambda_init),
        grid=(b, DIFF_HEADS),
        in_specs=[head, head, head, _resident(lam_params), _resident(subln_g)],
        out_specs=head,
        out_shape=jax.ShapeDtypeStruct((b, t, DIFF_V_WIDTH), BF16),
        scratch_shapes=_attention_scratch(2, tq, t),
        compiler_params=_params(2, VMEM_LIMIT),
        name="diff_attention",
    )(q, k, v, lam_params, subln_g)


def _mix_ffn_kernel(x_ref, of_ref, ob_ref, gate_ref, mix_ref, ng_ref, mod_ref, g_ref, wo_ref, wg_ref, wu_ref, wd_ref,
                    *rest, final):
    o_ref = rest[-1]
    o = of_ref[...] + ob_ref[...]
    gate = gate_ref[...].astype(F32)
    a = jnp.concatenate([(_rms(o[:, cols]) * ng_ref[...] * _silu(gate[:, cols])).astype(BF16)
                         for cols in _lane_blocks(o.shape[1])], axis=1)
    wa = a.shape[1]
    u = _dot(a, wo_ref[0:wa, :]) + _dot(mix_ref[...], wo_ref[wa:, :])
    x = x_ref[...] + mod_ref[2:3, :] * u
    h = (_rms(x) * g_ref[1:2, :] * (1.0 + mod_ref[4:5, :]) + mod_ref[3:4, :]).astype(BF16)
    act = (_silu(_dot(h, wg_ref[...])) * _dot(h, wu_ref[...])).astype(BF16)
    x = x + mod_ref[5:6, :] * _dot(act, wd_ref[...])
    o_ref[...] = _rms(x) * rest[0][...] if final else x


def _mix_ffn(x, o_f, o_b, gates, gate_block, mix, norm_gain, mod, norm_g, w_out, wg, wu, wd, layer, tm, n_ctx_tiles,
             final_g=None):
    b, t, d = x.shape
    wa = o_f.shape[2]
    final = final_g is not None
    off = n_ctx_tiles if final else 0
    n_rows = t - off * tm
    in_specs = [_rows(tm, d, off), _rows(tm, wa, off), _rows(tm, wa, off),
                pl.BlockSpec((None, tm, wa), lambda i, j: (i, j + off, gate_block)),
                _rows(tm, mix.shape[2], off), _resident(norm_gain), _mod_spec(layer, b, n_ctx_tiles, d, off),
                _norm_spec(layer, d), _resident(w_out), _resident(wg), _resident(wu), _resident(wd)]
    args = [x, o_f, o_b, gates, mix, norm_gain, mod, norm_g, w_out, wg, wu, wd]
    if final:
        in_specs.append(_resident(final_g))
        args.append(final_g)
    return pl.pallas_call(
        functools.partial(_mix_ffn_kernel, final=final),
        grid=(b, n_rows // tm),
        in_specs=in_specs,
        out_specs=_rows(tm, d),
        out_shape=jax.ShapeDtypeStruct((b, n_rows, d), F32),
        compiler_params=_params(2, VMEM_LIMIT),
        name="mix_ffn",
    )(*args)


def _rope_tables(ctx_len, seq, kind, scale=1.0):
    lane = np.arange(LANES)
    n = jnp.arange(seq, dtype=jnp.int32)
    if kind == "ret":
        half = RET_QK // 2
        idx = lane % RET_QK
        active = np.ones(LANES, bool)
        pos = jnp.broadcast_to(n[:, None], (seq, LANES))
    else:
        width = DIFF_QK if kind == "diff" else MLA_ROPE
        half = width // 4
        if kind == "diff":
            idx = lane % (width // 2)
            by_col = (lane % width) >= width // 2
            active = np.ones(LANES, bool)
        else:
            idx = (lane - MLA_NOPE) % (width // 2)
            by_col = (lane - MLA_NOPE) >= width // 2
            active = (lane >= MLA_NOPE) & (lane < MLA_NOPE + width)
        pos = jnp.where(jnp.asarray(by_col)[None, :], (n % GRID_W)[:, None], (n // GRID_W)[:, None])
    upper = idx >= half
    inv = ROPE_BASE ** (-jnp.asarray(idx % half, F32) / half)
    ang = pos.astype(F32) * inv[None, :]
    act = jnp.asarray(active)[None, :]
    up = jnp.asarray(upper)[None, :]
    cos = jnp.where(act, jnp.cos(ang), 1.0)
    sin = jnp.where(act, jnp.sin(ang), 0.0)
    lat = jnp.stack([cos, jnp.where(up, sin, 0.0), jnp.where(up, 0.0, -sin)])
    ident = jnp.stack([jnp.ones((ctx_len, LANES), F32), jnp.zeros((ctx_len, LANES), F32),
                       jnp.zeros((ctx_len, LANES), F32)])
    return jnp.concatenate([ident, lat], axis=1) * scale


def kernel(x, c, ctx, c_ctx, ada_w, ada_b, norm_g, mix_w_out, ffn_w_gate, ffn_w_up, ffn_w_down,
           even_w_in, hgrn_lb_logits, hgrn_norm_g, mla_q_norm_g, mla_w_uq, mla_kv_norm_g, mla_w_ukv,
           odd_w_in, ret_decay_logits, diff_lambda, diff_subln_g, final_norm_g):
    batch, seq, d = x.shape
    ctx_len = ctx.shape[1]
    depth = ada_w.shape[0]
    tm = math.gcd(ROW_TILE, math.gcd(ctx_len, seq))
    n_ctx_tiles = ctx_len // tm
    assert tm % (2 * CHUNK) == 0 and seq % GRID_W == 0

    mod_rows = -(-(batch + 1) // 8) * 8
    s_rows = jnp.concatenate([c, c_ctx[None, :], jnp.zeros((mod_rows - batch - 1, d), F32)], axis=0)
    mod = _modulation(s_rows, ada_w, ada_b).reshape(depth, mod_rows, 6, d)

    lb_soft = jax.nn.softmax(hgrn_lb_logits.astype(F32), axis=0)
    lbs = jnp.cumsum(lb_soft, axis=0) - lb_soft[0:1]

    tab_mla_q = _rope_tables(ctx_len, seq, "mla", (MLA_NOPE + MLA_ROPE) ** -0.5)
    tab_mla_k = _rope_tables(ctx_len, seq, "mla")
    tab_ret = _rope_tables(ctx_len, seq, "ret")
    tab_diff = _rope_tables(ctx_len, seq, "diff")

    stream = jnp.concatenate([ctx, x], axis=1)
    for layer in range(depth):
        if layer % 2 == 0:
            e = layer // 2
            w = even_w_in[e]
            w_in = jnp.concatenate([w[:, EV_HGRN:], jnp.zeros((d, LANES - MLA_ROPE), F32), w[:, :EV_HGRN]],
                                   axis=1).astype(BF16)
            wq = mla_w_uq[e].reshape(MLA_Q_RANK, MLA_HEADS, MLA_NOPE + MLA_ROPE)
            wq = jnp.pad(wq, ((0, 0), (0, 0), (0, LANES - MLA_NOPE - MLA_ROPE))).reshape(MLA_Q_RANK, -1)
            wkv = mla_w_ukv[e].reshape(MLA_KV_RANK, MLA_HEADS, MLA_NOPE + MLA_V)
            wk = jnp.pad(wkv[:, :, :MLA_NOPE], ((0, 0), (0, 0), (0, LANES - MLA_NOPE))).reshape(MLA_KV_RANK, -1)
            wv = wkv[:, :, MLA_NOPE:].reshape(MLA_KV_RANK, -1)
            gates, q, k, v = _even_in_projection(
                stream, mod, norm_g, w_in, mla_q_norm_g[e][None, :], mla_kv_norm_g[e][None, :], wq.astype(BF16),
                jnp.concatenate([wk, wv], axis=1).astype(BF16), tab_mla_q, tab_mla_k, layer, tm, n_ctx_tiles)
            o_f, o_b = _hgrn_scan(gates, lbs[e].reshape(2, 1, HGRN_WIDTH), tm, n_ctx_tiles)
            mix = _mla_attention(q, k, v, tm, ctx_len)
            gate_block, gain = 4, hgrn_norm_g[e][None, :]
        else:
            o = layer // 2
            qk_ret, gates, q_d, k_d, v_d = _odd_in_projection(stream, mod, norm_g, odd_w_in[o].astype(BF16),
                                                              tab_ret, tab_diff, layer, tm, n_ctx_tiles)
            decay = jnp.broadcast_to(ret_decay_logits[o][:, :, None, None], (2, RET_HEADS, 1, LANES))
            o_f, o_b = _ret_scan(qk_ret, gates, decay, tm, n_ctx_tiles)
            lambda_init = 0.8 - 0.6 * math.exp(-0.3 * layer)
            mix = _diff_attention(q_d, k_d, v_d, diff_lambda[o], diff_subln_g[o][None, :], tm, ctx_len, lambda_init)
            gate_block, gain = 1, jnp.ones((1, LANES), F32)
        stream = _mix_ffn(stream, o_f, o_b, gates, gate_block, mix, gain, mod, norm_g, mix_w_out[layer].astype(BF16),
                          ffn_w_gate[layer].astype(BF16), ffn_w_up[layer].astype(BF16),
                          ffn_w_down[layer].astype(BF16), layer, tm, n_ctx_tiles,
                          final_norm_g[None, :] if layer == depth - 1 else None)
    return stream
```

```python
import functools
import math

import numpy as np
import jax
import jax.numpy as jnp
from jax import lax
from jax.experimental import pallas as pl
from jax.experimental.pallas import tpu as pltpu

F32 = jnp.float32
BF16 = jnp.bfloat16

LANES = 128
GRID_W = 64
CHUNK = 64
ROPE_BASE = 10000.0
EPS = 1e-6

HGRN_HEADS, HGRN_DIM = 4, 128
HGRN_WIDTH = HGRN_HEADS * HGRN_DIM
MLA_HEADS, MLA_Q_RANK, MLA_KV_RANK, MLA_NOPE, MLA_ROPE, MLA_V = 8, 384, 256, 64, 32, 64
RET_HEADS, RET_QK, RET_V = 4, 64, 128
DIFF_HEADS, DIFF_QK, DIFF_V = 4, 64, 128
RET_QK_WIDTH = RET_HEADS * RET_QK
RET_V_WIDTH = RET_HEADS * RET_V
DIFF_QK_WIDTH = DIFF_HEADS * 2 * DIFF_QK
DIFF_V_WIDTH = DIFF_HEADS * DIFF_V

EV_HGRN = 5 * HGRN_WIDTH
EV_CQ = 0
EV_CKV = EV_CQ + MLA_Q_RANK
EV_KROPE = EV_CKV + MLA_KV_RANK
EV_HGRN0 = EV_KROPE + LANES
OD_KR = RET_QK_WIDTH
OD_VR = OD_KR + RET_QK_WIDTH
OD_QD = OD_VR + 2 * RET_V_WIDTH
OD_KD = OD_QD + DIFF_QK_WIDTH
OD_VD = OD_KD + DIFF_QK_WIDTH

ROW_TILE = 256
KEY_CHUNK = 768
VMEM_LIMIT = 48 * 1024 * 1024


def _params(n_axes, vmem=None):
    return pltpu.CompilerParams(dimension_semantics=("arbitrary",) * n_axes, vmem_limit_bytes=vmem)


def _resident(a):
    return pl.BlockSpec(a.shape, lambda *_: (0,) * a.ndim, pipeline_mode=pl.Buffered(1))


def _dot(a, b):
    return jnp.dot(a, b, preferred_element_type=F32)


def _dot_nt(a, b):
    return lax.dot_general(a, b, (((1,), (1,)), ((), ())), preferred_element_type=F32)


def _dot_tn(a, b):
    return lax.dot_general(a, b, (((0,), (0,)), ((), ())), preferred_element_type=F32)


def _rms(x):
    return x * lax.rsqrt(jnp.mean(x * x, axis=-1, keepdims=True) + EPS)


def _silu(x):
    return x * jax.nn.sigmoid(x)


def _log_sigmoid(z):
    return jnp.minimum(z, 0.0) - jnp.log1p(jnp.exp(-jnp.abs(z)))


def _rope(x, tab_ref, shift):
    return x * tab_ref[0] + pltpu.roll(x, shift, 1) * tab_ref[1] + pltpu.roll(x, LANES - shift, 1) * tab_ref[2]


def _lane_blocks(width):
    return [slice(i * LANES, (i + 1) * LANES) for i in range(width // LANES)]


def _mod_kernel(s_ref, w_ref, b_ref, o_ref):
    s = _silu(s_ref[...])
    o_ref[...] = jnp.dot(s, w_ref[...], precision=lax.Precision.HIGHEST, preferred_element_type=F32) + b_ref[...]


def _modulation(s_rows, ada_w, ada_b):
    depth, d, d6 = ada_w.shape
    rows = s_rows.shape[0]
    tn = d
    return pl.pallas_call(
        _mod_kernel,
        grid=(depth, d6 // tn),
        in_specs=[pl.BlockSpec((rows, d), lambda l, n: (0, 0)),
                  pl.BlockSpec((None, d, tn), lambda l, n: (l, 0, n)),
                  pl.BlockSpec((None, 1, tn), lambda l, n: (l, 0, n))],
        out_specs=pl.BlockSpec((None, rows, tn), lambda l, n: (l, 0, n)),
        out_shape=jax.ShapeDtypeStruct((depth, rows, d6), F32),
        compiler_params=_params(2),
        name="modulation",
    )(s_rows, ada_w, ada_b.reshape(depth, 1, d6))


def _mod_spec(layer, batch, n_ctx_tiles, d, row_off=0):
    return pl.BlockSpec((None, None, 6, d),
                        lambda b, t: (layer, jnp.where(t + row_off < n_ctx_tiles, batch, b), 0, 0))


def _norm_spec(layer, d):
    return pl.BlockSpec((None, 2, d), lambda i, j: (layer, 0, 0))


def _rows(tm, width, row_off=0):
    return pl.BlockSpec((None, tm, width), lambda i, j: (i, j + row_off, 0))


def _even_in_kernel(x_ref, mod_ref, g_ref, w_ref, gq_ref, gkv_ref, wq_ref, wkv_ref, tq_ref, tk_ref,
                    hg_ref, q_ref, k_ref, v_ref):
    shift = MLA_ROPE // 4
    h = (_rms(x_ref[...]) * g_ref[0:1, :] * (1.0 + mod_ref[1:2, :]) + mod_ref[0:1, :]).astype(BF16)
    p = _dot(h, w_ref[:, :EV_HGRN0])
    hg_ref[...] = _dot(h, w_ref[:, EV_HGRN0:]).astype(BF16)
    q = _dot((_rms(p[:, EV_CQ:EV_CKV]) * gq_ref[...]).astype(BF16), wq_ref[...])
    kv = _dot((_rms(p[:, EV_CKV:EV_KROPE]) * gkv_ref[...]).astype(BF16), wkv_ref[...])
    k_rope = _rope(pltpu.roll(p[:, EV_KROPE:], MLA_NOPE, 1), tk_ref, shift)
    for cols in _lane_blocks(MLA_HEADS * LANES):
        q_ref[:, cols] = _rope(q[:, cols], tq_ref, shift).astype(BF16)
        k_ref[:, cols] = (kv[:, cols] + k_rope).astype(BF16)
    v_ref[...] = kv[:, MLA_HEADS * LANES:].astype(BF16)


def _even_in_projection(x, mod, norm_g, w, gq, gkv, wq, wkv, tab_q, tab_k, layer, tm, n_ctx_tiles):
    b, t, d = x.shape
    hq = MLA_HEADS * LANES
    tab = pl.BlockSpec((3, tm, LANES), lambda i, j: (0, j, 0))
    shapes = [(EV_HGRN, BF16), (hq, BF16), (hq, BF16), (MLA_HEADS * MLA_V, BF16)]
    return pl.pallas_call(
        _even_in_kernel,
        grid=(b, t // tm),
        in_specs=[_rows(tm, d), _mod_spec(layer, b, n_ctx_tiles, d), _norm_spec(layer, d), _resident(w),
                  _resident(gq), _resident(gkv), _resident(wq), _resident(wkv), tab, tab],
        out_specs=[_rows(tm, width) for width, _ in shapes],
        out_shape=[jax.ShapeDtypeStruct((b, t, width), dt) for width, dt in shapes],
        compiler_params=_params(2, VMEM_LIMIT),
        name="even_in_projection",
    )(x, mod, norm_g, w, gq, gkv, wq, wkv, tab_q, tab_k)


def _odd_in_kernel(x_ref, mod_ref, g_ref, w_ref, tr_ref, td_ref, qkr_ref, vg_ref, qd_ref, kd_ref, vd_ref):
    h = _rms(x_ref[...]) * g_ref[0:1, :] * (1.0 + mod_ref[1:2, :]) + mod_ref[0:1, :]
    p = _dot(h.astype(BF16), w_ref[...])
    for cols in _lane_blocks(RET_QK_WIDTH):
        qkr_ref[:, cols] = _rope(p[:, cols], tr_ref, RET_QK // 2)
        kc = slice(cols.start + OD_KR, cols.stop + OD_KR)
        qkr_ref[:, kc] = _rope(p[:, kc], tr_ref, RET_QK // 2) * (RET_QK ** -0.5)
    vg_ref[...] = p[:, OD_VR:OD_QD].astype(BF16)
    for cols in _lane_blocks(DIFF_QK_WIDTH):
        qc = slice(cols.start + OD_QD, cols.stop + OD_QD)
        kc = slice(cols.start + OD_KD, cols.stop + OD_KD)
        qd_ref[:, cols] = (_rope(p[:, qc], td_ref, DIFF_QK // 4) * (DIFF_QK ** -0.5)).astype(BF16)
        kd_ref[:, cols] = _rope(p[:, kc], td_ref, DIFF_QK // 4).astype(BF16)
    vd_ref[...] = p[:, OD_VD:].astype(BF16)


def _odd_in_projection(x, mod, norm_g, w, tab_ret, tab_diff, layer, tm, n_ctx_tiles):
    b, t, d = x.shape
    tab = pl.BlockSpec((3, tm, LANES), lambda i, j: (0, j, 0))
    shapes = [(2 * RET_QK_WIDTH, F32), (2 * RET_V_WIDTH, BF16), (DIFF_QK_WIDTH, BF16), (DIFF_QK_WIDTH, BF16),
              (DIFF_V_WIDTH, BF16)]
    return pl.pallas_call(
        _odd_in_kernel,
        grid=(b, t // tm),
        in_specs=[_rows(tm, d), _mod_spec(layer, b, n_ctx_tiles, d), _norm_spec(layer, d), _resident(w), tab, tab],
        out_specs=[_rows(tm, width) for width, _ in shapes],
        out_shape=[jax.ShapeDtypeStruct((b, t, width), dt) for width, dt in shapes],
        compiler_params=_params(2, VMEM_LIMIT),
        name="odd_in_projection",
    )(x, mod, norm_g, w, tab_ret, tab_diff)


_LEVELS = 6
HGRN_STEP_HEADS = 4


def _gla_masks():
    c = CHUNK
    masks = np.zeros((_LEVELS + 1, c, c), np.float32)
    masks[0] = np.eye(c)
    for lv in range(_LEVELS):
        b = 32 >> lv
        for tau in range(c):
            mid = (tau // (2 * b)) * 2 * b + b
            if tau >= mid:
                masks[lv + 1, tau, mid - b:mid] = 1.0
    mk = np.stack([masks, masks[:, ::-1, ::-1]]).reshape(2, (_LEVELS + 1) * c, c)
    return jnp.asarray(mk, F32)


def _gla_gates(z, lb):
    k = (1.0 - lb) * jax.nn.sigmoid(-z)
    return k, (_log_sigmoid(z) + jnp.log1p(lb * jnp.exp(-z))) * math.log2(math.e)


SUBLANES = 8


def _scan_positions(reverse):
    sub = lax.broadcasted_iota(jnp.int32, (SUBLANES, LANES), 0)
    return (SUBLANES - 1 - sub) if reverse else sub


def _gla_tile_consts(reverse):
    pos = _scan_positions(reverse)
    consts = {("from", s): (pos >= s).astype(F32) for s in (1, 2, 4)}
    consts["right", 1] = (pos % 2 == 1).astype(F32)
    for b in (2, 4):
        consts["sign", b] = jnp.where(pos % (2 * b) >= b, 1.0, -1.0)
    return consts


def _gla_decays(g2, reverse, consts):
    n_tiles = CHUNK // SUBLANES
    pos = _scan_positions(reverse)
    tiles = [g2[v * SUBLANES:(v + 1) * SUBLANES] for v in range(n_tiles)]

    def from_earlier(x, s):
        return pltpu.roll(x, (SUBLANES - s) if reverse else s, 0)

    def at(x, p):
        j = SUBLANES - 1 - p if reverse else p
        return jnp.broadcast_to(x[j:j + 1], x.shape)

    scan = []
    for x in tiles:
        for s in (1, 2, 4):
            x = x + from_earlier(x, s) * consts["from", s]
        scan.append(x)
    order = list(reversed(range(n_tiles))) if reverse else list(range(n_tiles))
    prefix, carry = {}, None
    for v in order:
        prefix[v] = scan[v] if carry is None else scan[v] + carry
        carry = at(prefix[v], SUBLANES - 1)
    total = carry
    p_tiles = [prefix[v] for v in order]

    levels = []
    for lv in range(_LEVELS):
        b = 32 >> lv
        out = []
        for n, p in enumerate(p_tiles):
            if b >= SUBLANES:
                span = 2 * b // SUBLANES
                left_last = (n // span) * span + span // 2 - 1
                edge = at(p_tiles[left_last], SUBLANES - 1)
                out.append(p - edge if n % span >= span // 2 else edge - p)
            elif b == 1:
                out.append(tiles[order[n]] * consts["right", 1])
            else:
                edge = at(p, b - 1)
                for first in range(2 * b, SUBLANES, 2 * b):
                    edge = jnp.where(pos >= first, at(p, first + b - 1), edge)
                out.append((p - edge) * consts["sign", b])
        levels.append(out)

    def rows(tile_list):
        return jnp.concatenate(list(reversed(tile_list)) if reverse else tile_list, axis=0)

    dec = [jnp.exp2(rows(t)) for t in levels]
    return dec, jnp.exp2(rows(p_tiles)), jnp.exp2(rows([total - p for p in p_tiles]))


def _sibling_select(qs, k, b, reverse):
    c = CHUNK
    if b % 8 == 0:
        right = lambda r0: (((c - b - r0) if reverse else r0) // b) % 2 == 1
        return jnp.concatenate([(qs if right(r0) else k)[r0:r0 + b] for r0 in range(0, c, b)], axis=0)
    row = lax.broadcasted_iota(jnp.int32, (c, LANES), 0)
    tau = (c - 1 - row) if reverse else row
    return jnp.where((tau & b) != 0, qs, k)


def _gla_chunk(qs, v, k, g2, masks, st, reverse, consts):
    c = CHUNK
    tile_rows = range(0, c, SUBLANES)
    levels, dec_q, dec_k = _gla_decays(g2, reverse, consts)
    att = masks[0:c] * _dot_nt(qs.astype(BF16), k.astype(BF16))
    att = [att[r0:r0 + SUBLANES] for r0 in tile_rows]
    for lv in range(_LEVELS):
        b = 32 >> lv
        zl = (_sibling_select(qs, k, b, reverse) * levels[lv]).astype(BF16)
        term = _dot_nt(zl, zl)
        for n, r0 in enumerate(tile_rows):
            scan_row = (c - SUBLANES - r0) if reverse else r0
            if b < SUBLANES or (scan_row // b) % 2 == 1:
                att[n] = att[n] + masks[(lv + 1) * c + r0:(lv + 1) * c + r0 + SUBLANES] * term[r0:r0 + SUBLANES]
    att = jnp.concatenate(att, axis=0)
    vb = v.astype(BF16)
    o = _dot(att.astype(BF16), vb) + _dot_nt((qs * dec_q).astype(BF16), st.astype(BF16))
    last = 0 if reverse else c - 1
    st = st * dec_q[last:last + 1, :] + _dot_tn(vb, (k * dec_k).astype(BF16))
    return o, st


def _hgrn_kernel(qf_ref, if_ref, zf_ref, qb_ref, ib_ref, zb_ref, lbf_ref, lbb_ref, mk_ref,
                 of_ref, ob_ref, sf_ref, sb_ref, *, n_chunks):
    @pl.when(pl.program_id(2) == 0)
    def _():
        sf_ref[...] = jnp.zeros_like(sf_ref)
        sb_ref[...] = jnp.zeros_like(sb_ref)

    dirs = ((qf_ref, if_ref, zf_ref, lbf_ref, of_ref, sf_ref), (qb_ref, ib_ref, zb_ref, lbb_ref, ob_ref, sb_ref))
    n_heads = sf_ref.shape[0]
    states = {(d, h): dirs[d][5][h] for d in range(2) for h in range(n_heads)}
    consts = [_gla_tile_consts(d == 1) for d in range(2)]
    for step in range(n_chunks):
        for d, (q_ref, i_ref, z_ref, lb_ref, o_ref, _) in enumerate(dirs):
            ci = step if d == 0 else n_chunks - 1 - step
            rows = slice(ci * CHUNK, (ci + 1) * CHUNK)
            for h, cols in enumerate(_lane_blocks(n_heads * LANES)):
                k, g2 = _gla_gates(z_ref[rows, cols].astype(F32), lb_ref[:, cols])
                o, states[d, h] = _gla_chunk(q_ref[rows, cols].astype(F32), i_ref[rows, cols].astype(F32), k, g2,
                                             mk_ref[d], states[d, h], d == 1, consts[d])
                o_ref[rows, cols] = o
    for (d, h), st in states.items():
        dirs[d][5][h] = st


def _scan_block_maps(n_blocks, n_ctx_blocks):
    def bwd(j):
        return jnp.where(j < n_ctx_blocks, n_ctx_blocks - 1 - j, n_blocks - 1 - (j - n_ctx_blocks))
    return (lambda j: j), bwd


def _hgrn_scan(p, lbs, tb, n_ctx_blocks):
    b, t, _ = p.shape
    nb = t // tb
    fwd, bwd = _scan_block_maps(nb, n_ctx_blocks)
    masks = _gla_masks()

    hs = HGRN_STEP_HEADS
    width = hs * LANES
    groups = HGRN_HEADS // hs

    def col(group, order):
        return pl.BlockSpec((None, tb, width), lambda i, h, j: (i, order(j), group * groups + h))

    def lb_spec(d):
        return pl.BlockSpec((None, 1, width), lambda i, h, j: (d, 0, h))

    def out_spec(order):
        return pl.BlockSpec((None, tb, width), lambda i, h, j: (i, order(j), h))

    out = jax.ShapeDtypeStruct((b, t, HGRN_WIDTH), F32)
    state = pltpu.VMEM((hs, HGRN_DIM, HGRN_DIM), F32)
    return pl.pallas_call(
        functools.partial(_hgrn_kernel, n_chunks=tb // CHUNK),
        grid=(b, groups, nb),
        in_specs=[col(0, fwd), col(1, fwd), col(2, fwd), col(0, bwd), col(1, bwd), col(3, bwd),
                  lb_spec(0), lb_spec(1), _resident(masks)],
        out_specs=[out_spec(fwd), out_spec(bwd)],
        out_shape=[out, out],
        scratch_shapes=[state, state],
        compiler_params=_params(3),
        name="hgrn_scan",
    )(p, p, p, p, p, p, lbs, lbs, masks)


def _key_chunk(n_keys):
    return max(c for c in range(LANES, min(KEY_CHUNK, n_keys) + 1, LANES) if n_keys % c == 0)


def _score_steps(q, k_ref, k_cols, n_keys, s_ref, m_ref):
    kc = _key_chunk(n_keys)
    m = None
    for c in range(n_keys // kc):
        s = _dot_nt(q, k_ref[c * kc:(c + 1) * kc, k_cols])
        s_ref[:, c * kc:(c + 1) * kc] = s
        for blk in _lane_blocks(kc):
            m = s[:, blk] if m is None else jnp.maximum(m, s[:, blk])
        yield
    m_ref[...] = jnp.broadcast_to(jnp.max(m, axis=-1, keepdims=True), m.shape)


def _weight_steps(s_ref, m_ref, v_ref, n_keys, out):
    kc = _key_chunk(n_keys)
    m = m_ref[...]
    ones = jnp.ones((kc, LANES), BF16)
    acc = None
    for c in range(n_keys // kc):
        p = jnp.concatenate([jnp.exp(s_ref[:, c * kc + blk.start:c * kc + blk.stop] - m)
                             for blk in _lane_blocks(kc)], axis=1)
        part = _dot(p.astype(BF16), jnp.concatenate([v_ref[c * kc:(c + 1) * kc, :], ones], axis=1))
        acc = part if acc is None else acc + part
        yield
    out.append(acc)


def _interleave(*step_iters):
    live = list(step_iters)
    while live:
        for it in list(live):
            if next(it, StopIteration) is StopIteration:
                live.remove(it)


def _attention_tiles(n_streams, scores, weights, finish, tq, ctx_len, t):
    def weigh(buf, n_keys, row0, *extra):
        accs = []
        for s in range(n_streams):
            out = []
            _interleave(weights(s, buf, n_keys, out), *[e(s) for e in extra])
            accs.append(out[0])
        finish(row0, accs)

    for i in range(ctx_len // tq):
        for s in range(n_streams):
            _interleave(scores(s, i * tq, 0, ctx_len))
        weigh(0, ctx_len, i * tq)
    n_lat = (t - ctx_len) // tq
    for s in range(n_streams):
        _interleave(scores(s, ctx_len, 0, t))

    def next_scores(row0, buf):
        return lambda s: scores(s, row0, buf, t)

    def pair(i, carry):
        row0 = pl.multiple_of(ctx_len + 2 * i * tq, tq)
        weigh(0, t, row0, next_scores(row0 + tq, 1))
        weigh(1, t, row0 + tq, next_scores(row0 + 2 * tq, 0))
        return carry

    n_pairs = (n_lat - 1) // 2
    lax.fori_loop(0, n_pairs, pair, 0)
    row0 = ctx_len + 2 * n_pairs * tq
    if n_lat % 2 == 0:
        weigh(0, t, row0, next_scores(row0 + tq, 1))
        weigh(1, t, row0 + tq)
    else:
        weigh(0, t, row0)


def _tile_rows(row0, tq):
    return pl.ds(row0 if isinstance(row0, int) else pl.multiple_of(row0, tq), tq)


def _attention_scratch(n_streams, tq, t):
    scores = pltpu.VMEM((n_streams, tq, t), F32)
    row_max = pltpu.VMEM((n_streams, tq, LANES), F32)
    return [scores, scores, row_max, row_max]


def _mla_attn_kernel(q_ref, k_ref, v_ref, o_ref, s0_ref, s1_ref, m0_ref, m1_ref, *, tq, ctx_len):
    heads = _lane_blocks(2 * LANES)
    s_bufs, m_bufs = (s0_ref, s1_ref), (m0_ref, m1_ref)

    def scores(h, row0, buf, n_keys):
        return _score_steps(q_ref[_tile_rows(row0, tq), heads[h]], k_ref, heads[h], n_keys,
                            s_bufs[buf].at[h], m_bufs[buf].at[h])

    def weights(h, buf, n_keys, out):
        return _weight_steps(s_bufs[buf].at[h], m_bufs[buf].at[h], v_ref, n_keys, out)

    def finish(row0, accs):
        o = [a[:, :LANES] / a[:, LANES:] for a in accs]
        lane = lax.broadcasted_iota(jnp.int32, o[0].shape, 1)
        o_ref[_tile_rows(row0, tq), :] = jnp.where(lane < MLA_V, o[0], o[1]).astype(BF16)

    _attention_tiles(2, scores, weights, finish, tq, ctx_len, k_ref.shape[0])


def _mla_attention(q, k, v, tq, ctx_len):
    b, t, _ = q.shape
    return pl.pallas_call(
        functools.partial(_mla_attn_kernel, tq=tq, ctx_len=ctx_len),
        grid=(b, MLA_HEADS // 2),
        in_specs=[pl.BlockSpec((None, t, 2 * LANES), lambda i, h: (i, 0, h)),
                  pl.BlockSpec((None, t, 2 * LANES), lambda i, h: (i, 0, h)),
                  pl.BlockSpec((None, t, LANES), lambda i, h: (i, 0, h))],
        out_specs=pl.BlockSpec((None, t, LANES), lambda i, h: (i, 0, h)),
        out_shape=jax.ShapeDtypeStruct((b, t, MLA_HEADS * MLA_V), BF16),
        scratch_shapes=_attention_scratch(2, tq, t),
        compiler_params=_params(2, VMEM_LIMIT),
        name="mla_attention",
    )(q, k, v)


def _ret_chunk(q, k, v, decays, st, head_lanes):
    dmat, q_dec, k_dec, c_dec = decays
    qm = jnp.where(head_lanes, q, 0.0)
    att = _dot_nt(qm.astype(BF16), k.astype(BF16)) * dmat
    o = _dot(att.astype(BF16), v) + _dot_nt((qm * q_dec).astype(BF16), st.astype(BF16))
    st = st * c_dec + _dot_tn(v, (k * k_dec).astype(BF16))
    return o, st


def _ret_decays(lg, reverse):
    c = CHUNK
    row = lax.broadcasted_iota(jnp.int32, (c, LANES), 0)
    tau = ((c - 1 - row) if reverse else row).astype(F32)
    ti = lax.broadcasted_iota(jnp.int32, (c, c), 0)
    si = lax.broadcasted_iota(jnp.int32, (c, c), 1)
    rel = ((si - ti) if reverse else (ti - si)).astype(F32)
    dmat = jnp.where(rel >= 0, jnp.exp(jnp.maximum(rel, 0.0) * lg[:, :c]), 0.0)
    return dmat, jnp.exp((tau + 1.0) * lg), jnp.exp((c - 1.0 - tau) * lg), jnp.exp(c * lg)


def _ret_kernel(qkf_ref, vf_ref, qkb_ref, vb_ref, lg_ref, of_ref, ob_ref, sf_ref, sb_ref, *, n_chunks):
    @pl.when(pl.program_id(1) == 0)
    def _():
        sf_ref[...] = jnp.zeros_like(sf_ref)
        sb_ref[...] = jnp.zeros_like(sb_ref)

    per_block = LANES // RET_QK
    lane = lax.broadcasted_iota(jnp.int32, (CHUNK, LANES), 1)
    dirs = ((qkf_ref, vf_ref, of_ref, sf_ref), (qkb_ref, vb_ref, ob_ref, sb_ref))
    chains = [(d, h) for d in range(2) for h in range(RET_HEADS)]
    decays = {(d, h): _ret_decays(_log_sigmoid(lg_ref[d, h]), d == 1) for d, h in chains}
    states = {(d, h): dirs[d][3][h] for d, h in chains}
    for step in range(n_chunks):
        for d, h in chains:
            qk_ref, v_ref, o_ref, _ = dirs[d]
            ci = step if d == 0 else n_chunks - 1 - step
            rows = slice(ci * CHUNK, (ci + 1) * CHUNK)
            q_cols = _lane_blocks(2 * RET_QK_WIDTH)[h // per_block]
            k_cols = _lane_blocks(2 * RET_QK_WIDTH)[RET_QK_WIDTH // LANES + h // per_block]
            v_cols = _lane_blocks(RET_V_WIDTH)[h]
            o, states[d, h] = _ret_chunk(qk_ref[rows, q_cols], qk_ref[rows, k_cols], v_ref[rows, v_cols],
                                         decays[d, h], states[d, h], (lane // RET_QK) == (h % per_block))
            o_ref[rows, v_cols] = o
    for (d, h), st in states.items():
        dirs[d][3][h] = st


def _ret_scan(qk, vg, decay, tb, n_ctx_blocks):
    b, t, _ = qk.shape
    nb = t // tb
    fwd, bwd = _scan_block_maps(nb, n_ctx_blocks)

    def rows(width, order):
        return pl.BlockSpec((None, tb, width), lambda i, j: (i, order(j), 0))

    out = jax.ShapeDtypeStruct((b, t, RET_V_WIDTH), F32)
    state = pltpu.VMEM((RET_HEADS, RET_V, LANES), F32)
    return pl.pallas_call(
        functools.partial(_ret_kernel, n_chunks=tb // CHUNK),
        grid=(b, nb),
        in_specs=[rows(2 * RET_QK_WIDTH, fwd), rows(RET_V_WIDTH, fwd), rows(2 * RET_QK_WIDTH, bwd),
                  rows(RET_V_WIDTH, bwd), _resident(decay)],
        out_specs=[rows(RET_V_WIDTH, fwd), rows(RET_V_WIDTH, bwd)],
        out_shape=[out, out],
        scratch_shapes=[state, state],
        compiler_params=_params(2),
        name="retention_scan",
    )(qk, vg, qk, vg, decay)


def _diff_attn_kernel(q_ref, k_ref, v_ref, lam_ref, g_ref, o_ref, s0_ref, s1_ref, m0_ref, m1_ref,
                      *, tq, ctx_len, lambda_init):
    lp = lam_ref[...]
    lam = (jnp.exp(jnp.sum(lp[0:1] * lp[1:2], axis=-1, keepdims=True))
           - jnp.exp(jnp.sum(lp[2:3] * lp[3:4], axis=-1, keepdims=True)) + lambda_init)

    s_bufs, m_bufs = (s0_ref, s1_ref), (m0_ref, m1_ref)
    all_lanes = slice(0, LANES)

    def scores(i, row0, buf, n_keys):
        q = q_ref[_tile_rows(row0, tq), :]
        first = lax.broadcasted_iota(jnp.int32, q.shape, 1) < DIFF_QK
        q = jnp.where(first == (i == 0), q, jnp.zeros_like(q))
        return _score_steps(q, k_ref, all_lanes, n_keys, s_bufs[buf].at[i], m_bufs[buf].at[i])

    def weights(i, buf, n_keys, out):
        return _weight_steps(s_bufs[buf].at[i], m_bufs[buf].at[i], v_ref, n_keys, out)

    def finish(row0, accs):
        o = accs[0][:, :LANES] / accs[0][:, LANES:] - lam * (accs[1][:, :LANES] / accs[1][:, LANES:])
        o_ref[_tile_rows(row0, tq), :] = (_rms(o) * g_ref[...] * (1.0 - lambda_init)).astype(BF16)

    _attention_tiles(2, scores, weights, finish, tq, ctx_len, k_ref.shape[0])


def _diff_attention(q, k, v, lam_params, subln_g, tq, ctx_len, lambda_init):
    b, t, _ = q.shape
    head = pl.BlockSpec((None, t, LANES), lambda i, h: (i, 0, h))
    return pl.pallas_call(
        functools.partial(_diff_attn_kernel, tq=tq, ctx_len=ctx_len, lambda_init=lambda_init),
        grid=(b, DIFF_HEADS),
        in_specs=[head, head, head, _resident(lam_params), _resident(subln_g)],
        out_specs=head,
        out_shape=jax.ShapeDtypeStruct((b, t, DIFF_V_WIDTH), BF16),
        scratch_shapes=_attention_scratch(2, tq, t),
        compiler_params=_params(2, VMEM_LIMIT),
        name="diff_attention",
    )(q, k, v, lam_params, subln_g)


def _mix_ffn_kernel(x_ref, of_ref, ob_ref, gate_ref, mix_ref, ng_ref, mod_ref, g_ref, wo_ref, wg_ref, wu_ref, wd_ref,
                    *rest, final):
    o_ref = rest[-1]
    o = of_ref[...] + ob_ref[...]
    gate = gate_ref[...].astype(F32)
    a = jnp.concatenate([(_rms(o[:, cols]) * ng_ref[...] * _silu(gate[:, cols])).astype(BF16)
                         for cols in _lane_blocks(o.shape[1])], axis=1)
    wa = a.shape[1]
    u = _dot(a, wo_ref[0:wa, :]) + _dot(mix_ref[...], wo_ref[wa:, :])
    x = x_ref[...] + mod_ref[2:3, :] * u
    h = (_rms(x) * g_ref[1:2, :] * (1.0 + mod_ref[4:5, :]) + mod_ref[3:4, :]).astype(BF16)
    act = (_silu(_dot(h, wg_ref[...])) * _dot(h, wu_ref[...])).astype(BF16)
    x = x + mod_ref[5:6, :] * _dot(act, wd_ref[...])
    o_ref[...] = _rms(x) * rest[0][...] if final else x


def _mix_ffn(x, o_f, o_b, gates, gate_block, mix, norm_gain, mod, norm_g, w_out, wg, wu, wd, layer, tm, n_ctx_tiles,
             final_g=None):
    b, t, d = x.shape
    wa = o_f.shape[2]
    final = final_g is not None
    off = n_ctx_tiles if final else 0
    n_rows = t - off * tm
    in_specs = [_rows(tm, d, off), _rows(tm, wa, off), _rows(tm, wa, off),
                pl.BlockSpec((None, tm, wa), lambda i, j: (i, j + off, gate_block)),
                _rows(tm, mix.shape[2], off), _resident(norm_gain), _mod_spec(layer, b, n_ctx_tiles, d, off),
                _norm_spec(layer, d), _resident(w_out), _resident(wg), _resident(wu), _resident(wd)]
    args = [x, o_f, o_b, gates, mix, norm_gain, mod, norm_g, w_out, wg, wu, wd]
    if final:
        in_specs.append(_resident(final_g))
        args.append(final_g)
    return pl.pallas_call(
        functools.partial(_mix_ffn_kernel, final=final),
        grid=(b, n_rows // tm),
        in_specs=in_specs,
        out_specs=_rows(tm, d),
        out_shape=jax.ShapeDtypeStruct((b, n_rows, d), F32),
        compiler_params=_params(2, VMEM_LIMIT),
        name="mix_ffn",
    )(*args)


def _rope_tables(ctx_len, seq, kind, scale=1.0):
    lane = np.arange(LANES)
    n = jnp.arange(seq, dtype=jnp.int32)
    if kind == "ret":
        half = RET_QK // 2
        idx = lane % RET_QK
        active = np.ones(LANES, bool)
        pos = jnp.broadcast_to(n[:, None], (seq, LANES))
    else:
        width = DIFF_QK if kind == "diff" else MLA_ROPE
        half = width // 4
        if kind == "diff":
            idx = lane % (width // 2)
            by_col = (lane % width) >= width // 2
            active = np.ones(LANES, bool)
        else:
            idx = (lane - MLA_NOPE) % (width // 2)
            by_col = (lane - MLA_NOPE) >= width // 2
            active = (lane >= MLA_NOPE) & (lane < MLA_NOPE + width)
        pos = jnp.where(jnp.asarray(by_col)[None, :], (n % GRID_W)[:, None], (n // GRID_W)[:, None])
    upper = idx >= half
    inv = ROPE_BASE ** (-jnp.asarray(idx % half, F32) / half)
    ang = pos.astype(F32) * inv[None, :]
    act = jnp.asarray(active)[None, :]
    up = jnp.asarray(upper)[None, :]
    cos = jnp.where(act, jnp.cos(ang), 1.0)
    sin = jnp.where(act, jnp.sin(ang), 0.0)
    lat = jnp.stack([cos, jnp.where(up, sin, 0.0), jnp.where(up, 0.0, -sin)])
    ident = jnp.stack([jnp.ones((ctx_len, LANES), F32), jnp.zeros((ctx_len, LANES), F32),
                       jnp.zeros((ctx_len, LANES), F32)])
    return jnp.concatenate([ident, lat], axis=1) * scale


def kernel(x, c, ctx, c_ctx, ada_w, ada_b, norm_g, mix_w_out, ffn_w_gate, ffn_w_up, ffn_w_down,
           even_w_in, hgrn_lb_logits, hgrn_norm_g, mla_q_norm_g, mla_w_uq, mla_kv_norm_g, mla_w_ukv,
           odd_w_in, ret_decay_logits, diff_lambda, diff_subln_g, final_norm_g):
    batch, seq, d = x.shape
    ctx_len = ctx.shape[1]
    depth = ada_w.shape[0]
    tm = math.gcd(ROW_TILE, math.gcd(ctx_len, seq))
    n_ctx_tiles = ctx_len // tm
    assert tm % (2 * CHUNK) == 0 and seq % GRID_W == 0

    mod_rows = -(-(batch + 1) // 8) * 8
    s_rows = jnp.concatenate([c, c_ctx[None, :], jnp.zeros((mod_rows - batch - 1, d), F32)], axis=0)
    mod = _modulation(s_rows, ada_w, ada_b).reshape(depth, mod_rows, 6, d)

    lb_soft = jax.nn.softmax(hgrn_lb_logits.astype(F32), axis=0)
    lbs = jnp.cumsum(lb_soft, axis=0) - lb_soft[0:1]

    tab_mla_q = _rope_tables(ctx_len, seq, "mla", (MLA_NOPE + MLA_ROPE) ** -0.5)
    tab_mla_k = _rope_tables(ctx_len, seq, "mla")
    tab_ret = _rope_tables(ctx_len, seq, "ret")
    tab_diff = _rope_tables(ctx_len, seq, "diff")

    stream = jnp.concatenate([ctx, x], axis=1)
    for layer in range(depth):
        if layer % 2 == 0:
            e = layer // 2
            w = even_w_in[e]
            w_in = jnp.concatenate([w[:, EV_HGRN:], jnp.zeros((d, LANES - MLA_ROPE), F32),
                                    w[:, :HGRN_WIDTH] * (HGRN_DIM ** -0.5), w[:, HGRN_WIDTH:EV_HGRN]],
                                   axis=1).astype(BF16)
            wq = mla_w_uq[e].reshape(MLA_Q_RANK, MLA_HEADS, MLA_NOPE + MLA_ROPE)
            wq = jnp.pad(wq, ((0, 0), (0, 0), (0, LANES - MLA_NOPE - MLA_ROPE))).reshape(MLA_Q_RANK, -1)
            wkv = mla_w_ukv[e].reshape(MLA_KV_RANK, MLA_HEADS, MLA_NOPE + MLA_V)
            wk = jnp.pad(wkv[:, :, :MLA_NOPE], ((0, 0), (0, 0), (0, LANES - MLA_NOPE))).reshape(MLA_KV_RANK, -1)
            wv = wkv[:, :, MLA_NOPE:].reshape(MLA_KV_RANK, -1)
            gates, q, k, v = _even_in_projection(
                stream, mod, norm_g, w_in, mla_q_norm_g[e][None, :], mla_kv_norm_g[e][None, :], wq.astype(BF16),
                jnp.concatenate([wk, wv], axis=1).astype(BF16), tab_mla_q, tab_mla_k, layer, tm, n_ctx_tiles)
            o_f, o_b = _hgrn_scan(gates, lbs[e].reshape(2, 1, HGRN_WIDTH), tm, n_ctx_tiles)
            mix = _mla_attention(q, k, v, tm, ctx_len)
            gate_block, gain = 4, hgrn_norm_g[e][None, :]
        else:
            o = layer // 2
            qk_ret, gates, q_d, k_d, v_d = _odd_in_projection(stream, mod, norm_g, odd_w_in[o].astype(BF16),
                                                              tab_ret, tab_diff, layer, tm, n_ctx_tiles)
            decay = jnp.broadcast_to(ret_decay_logits[o][:, :, None, None], (2, RET_HEADS, 1, LANES))
            o_f, o_b = _ret_scan(qk_ret, gates, decay, tm, n_ctx_tiles)
            lambda_init = 0.8 - 0.6 * math.exp(-0.3 * layer)
            mix = _diff_attention(q_d, k_d, v_d, diff_lambda[o], diff_subln_g[o][None, :], tm, ctx_len, lambda_init)
            gate_block, gain = 1, jnp.ones((1, LANES), F32)
        stream = _mix_ffn(stream, o_f, o_b, gates, gate_block, mix, gain, mod, norm_g, mix_w_out[layer].astype(BF16),
                          ffn_w_gate[layer].astype(BF16), ffn_w_up[layer].astype(BF16),
                          ffn_w_down[layer].astype(BF16), layer, tm, n_ctx_tiles,
                          final_norm_g[None, :] if layer == depth - 1 else None)
    return stream
```

```python
import functools
import math

import numpy as np
import jax
import jax.numpy as jnp
from jax import lax
from jax.experimental import pallas as pl
from jax.experimental.pallas import tpu as pltpu

F32 = jnp.float32
BF16 = jnp.bfloat16

LANES = 128
GRID_W = 64
CHUNK = 64
ROPE_BASE = 10000.0
EPS = 1e-6

HGRN_HEADS, HGRN_DIM = 4, 128
HGRN_WIDTH = HGRN_HEADS * HGRN_DIM
MLA_HEADS, MLA_Q_RANK, MLA_KV_RANK, MLA_NOPE, MLA_ROPE, MLA_V = 8, 384, 256, 64, 32, 64
RET_HEADS, RET_QK, RET_V = 4, 64, 128
DIFF_HEADS, DIFF_QK, DIFF_V = 4, 64, 128
RET_QK_WIDTH = RET_HEADS * RET_QK
RET_V_WIDTH = RET_HEADS * RET_V
DIFF_QK_WIDTH = DIFF_HEADS * 2 * DIFF_QK
DIFF_V_WIDTH = DIFF_HEADS * DIFF_V

EV_HGRN = 5 * HGRN_WIDTH
EV_CQ = 0
EV_CKV = EV_CQ + MLA_Q_RANK
EV_KROPE = EV_CKV + MLA_KV_RANK
EV_HGRN0 = EV_KROPE + LANES
OD_KR = RET_QK_WIDTH
OD_VR = OD_KR + RET_QK_WIDTH
OD_QD = OD_VR + 2 * RET_V_WIDTH
OD_KD = OD_QD + DIFF_QK_WIDTH
OD_VD = OD_KD + DIFF_QK_WIDTH

ROW_TILE = 256
KEY_CHUNK = 768
VMEM_LIMIT = 48 * 1024 * 1024


def _params(n_axes, vmem=None):
    return pltpu.CompilerParams(dimension_semantics=("arbitrary",) * n_axes, vmem_limit_bytes=vmem)


def _resident(a):
    return pl.BlockSpec(a.shape, lambda *_: (0,) * a.ndim, pipeline_mode=pl.Buffered(1))


def _dot(a, b):
    return jnp.dot(a, b, preferred_element_type=F32)


def _dot_nt(a, b):
    return lax.dot_general(a, b, (((1,), (1,)), ((), ())), preferred_element_type=F32)


def _dot_tn(a, b):
    return lax.dot_general(a, b, (((0,), (0,)), ((), ())), preferred_element_type=F32)


def _rms(x):
    return x * lax.rsqrt(jnp.mean(x * x, axis=-1, keepdims=True) + EPS)


def _silu(x):
    return x * jax.nn.sigmoid(x)


def _log_sigmoid(z):
    return jnp.minimum(z, 0.0) - jnp.log1p(jnp.exp(-jnp.abs(z)))


def _rope(x, tab_ref, shift):
    return x * tab_ref[0] + pltpu.roll(x, shift, 1) * tab_ref[1] + pltpu.roll(x, LANES - shift, 1) * tab_ref[2]


def _lane_blocks(width):
    return [slice(i * LANES, (i + 1) * LANES) for i in range(width // LANES)]


def _mod_kernel(s_ref, w_ref, b_ref, o_ref):
    s = _silu(s_ref[...])
    o_ref[...] = jnp.dot(s, w_ref[...], precision=lax.Precision.HIGHEST, preferred_element_type=F32) + b_ref[...]


def _modulation(s_rows, ada_w, ada_b):
    depth, d, d6 = ada_w.shape
    rows = s_rows.shape[0]
    tn = d
    return pl.pallas_call(
        _mod_kernel,
        grid=(depth, d6 // tn),
        in_specs=[pl.BlockSpec((rows, d), lambda l, n: (0, 0)),
                  pl.BlockSpec((None, d, tn), lambda l, n: (l, 0, n)),
                  pl.BlockSpec((None, 1, tn), lambda l, n: (l, 0, n))],
        out_specs=pl.BlockSpec((None, rows, tn), lambda l, n: (l, 0, n)),
        out_shape=jax.ShapeDtypeStruct((depth, rows, d6), F32),
        compiler_params=_params(2),
        name="modulation",
    )(s_rows, ada_w, ada_b.reshape(depth, 1, d6))


def _mod_spec(layer, batch, n_ctx_tiles, d, row_off=0):
    return pl.BlockSpec((None, None, 6, d),
                        lambda b, t: (layer, jnp.where(t + row_off < n_ctx_tiles, batch, b), 0, 0))


def _norm_spec(layer, d):
    return pl.BlockSpec((None, 2, d), lambda i, j: (layer, 0, 0))


def _rows(tm, width, row_off=0):
    return pl.BlockSpec((None, tm, width), lambda i, j: (i, j + row_off, 0))


def _stream_specs(stream, tm, n_ctx_tiles, row_off=0):
    if not isinstance(stream, tuple):
        return [_rows(tm, stream.shape[2], row_off)], [stream]
    d = stream[0].shape[2]
    return ([pl.BlockSpec((None, tm, d), lambda i, j: (i, jnp.minimum(j + row_off, n_ctx_tiles - 1), 0)),
             pl.BlockSpec((None, tm, d), lambda i, j: (i, jnp.maximum(j + row_off - n_ctx_tiles, 0), 0))],
            list(stream))


def _stream_tile(refs, n_ctx_tiles, row_off=0):
    if len(refs) == 1:
        return refs[0][...]
    return jnp.where(pl.program_id(1) + row_off < n_ctx_tiles, refs[0][...], refs[1][...])


def _even_in_kernel(*refs, n_ctx_tiles):
    (mod_ref, g_ref, w_ref, gq_ref, gkv_ref, wq_ref, wkv_ref, tq_ref, tk_ref, hg_ref, q_ref, k_ref, v_ref) = refs[-13:]
    shift = MLA_ROPE // 4
    x = _stream_tile(refs[:-13], n_ctx_tiles)
    h = (_rms(x) * g_ref[0:1, :] * (1.0 + mod_ref[1:2, :]) + mod_ref[0:1, :]).astype(BF16)
    p = _dot(h, w_ref[:, :EV_HGRN0])
    hg_ref[...] = _dot(h, w_ref[:, EV_HGRN0:]).astype(BF16)
    q = _dot((_rms(p[:, EV_CQ:EV_CKV]) * gq_ref[...]).astype(BF16), wq_ref[...])
    kv = _dot((_rms(p[:, EV_CKV:EV_KROPE]) * gkv_ref[...]).astype(BF16), wkv_ref[...])
    k_rope = _rope(pltpu.roll(p[:, EV_KROPE:], MLA_NOPE, 1), tk_ref, shift)
    for cols in _lane_blocks(MLA_HEADS * LANES):
        q_ref[:, cols] = _rope(q[:, cols], tq_ref, shift).astype(BF16)
        k_ref[:, cols] = (kv[:, cols] + k_rope).astype(BF16)
    v_ref[...] = kv[:, MLA_HEADS * LANES:].astype(BF16)


def _even_in_projection(stream, t, mod, norm_g, w, gq, gkv, wq, wkv, tab_q, tab_k, layer, tm, n_ctx_tiles):
    d = w.shape[0]
    b = stream[0].shape[0] if isinstance(stream, tuple) else stream.shape[0]
    hq = MLA_HEADS * LANES
    tab = pl.BlockSpec((3, tm, LANES), lambda i, j: (0, j, 0))
    shapes = [(EV_HGRN, BF16), (hq, BF16), (hq, BF16), (MLA_HEADS * MLA_V, BF16)]
    stream_specs, stream_args = _stream_specs(stream, tm, n_ctx_tiles)
    return pl.pallas_call(
        functools.partial(_even_in_kernel, n_ctx_tiles=n_ctx_tiles),
        grid=(b, t // tm),
        in_specs=stream_specs + [_mod_spec(layer, b, n_ctx_tiles, d), _norm_spec(layer, d), _resident(w),
                                 _resident(gq), _resident(gkv), _resident(wq), _resident(wkv), tab, tab],
        out_specs=[_rows(tm, width) for width, _ in shapes],
        out_shape=[jax.ShapeDtypeStruct((b, t, width), dt) for width, dt in shapes],
        compiler_params=_params(2, VMEM_LIMIT),
        name="even_in_projection",
    )(*stream_args, mod, norm_g, w, gq, gkv, wq, wkv, tab_q, tab_k)


def _even_w_kernel(w_ref, o_ref):
    n_low = w_ref.shape[1] - EV_HGRN
    o_ref[:, :n_low] = w_ref[:, EV_HGRN:].astype(BF16)
    o_ref[:, n_low:EV_HGRN0] = jnp.zeros((o_ref.shape[0], EV_HGRN0 - n_low), BF16)
    o_ref[:, EV_HGRN0:EV_HGRN0 + HGRN_WIDTH] = (w_ref[:, :HGRN_WIDTH] * (HGRN_DIM ** -0.5)).astype(BF16)
    o_ref[:, EV_HGRN0 + HGRN_WIDTH:] = w_ref[:, HGRN_WIDTH:EV_HGRN].astype(BF16)


def _even_weights(even_w_in, tm):
    n, d, cols = even_w_in.shape
    return pl.pallas_call(
        _even_w_kernel,
        grid=(n, d // tm),
        in_specs=[pl.BlockSpec((None, tm, cols), lambda e, r: (e, r, 0))],
        out_specs=pl.BlockSpec((None, tm, EV_HGRN0 + EV_HGRN), lambda e, r: (e, r, 0)),
        out_shape=jax.ShapeDtypeStruct((n, d, EV_HGRN0 + EV_HGRN), BF16),
        compiler_params=_params(2),
        name="even_weights",
    )(even_w_in)


def _odd_in_kernel(x_ref, mod_ref, g_ref, w_ref, tr_ref, td_ref, qkr_ref, vg_ref, qd_ref, kd_ref, vd_ref):
    h = _rms(x_ref[...]) * g_ref[0:1, :] * (1.0 + mod_ref[1:2, :]) + mod_ref[0:1, :]
    p = _dot(h.astype(BF16), w_ref[...])
    for cols in _lane_blocks(RET_QK_WIDTH):
        qkr_ref[:, cols] = _rope(p[:, cols], tr_ref, RET_QK // 2)
        kc = slice(cols.start + OD_KR, cols.stop + OD_KR)
        qkr_ref[:, kc] = _rope(p[:, kc], tr_ref, RET_QK // 2) * (RET_QK ** -0.5)
    vg_ref[...] = p[:, OD_VR:OD_QD].astype(BF16)
    for cols in _lane_blocks(DIFF_QK_WIDTH):
        qc = slice(cols.start + OD_QD, cols.stop + OD_QD)
        kc = slice(cols.start + OD_KD, cols.stop + OD_KD)
        qd_ref[:, cols] = (_rope(p[:, qc], td_ref, DIFF_QK // 4) * (DIFF_QK ** -0.5)).astype(BF16)
        kd_ref[:, cols] = _rope(p[:, kc], td_ref, DIFF_QK // 4).astype(BF16)
    vd_ref[...] = p[:, OD_VD:].astype(BF16)


def _odd_in_projection(x, mod, norm_g, w, tab_ret, tab_diff, layer, tm, n_ctx_tiles):
    b, t, d = x.shape
    tab = pl.BlockSpec((3, tm, LANES), lambda i, j: (0, j, 0))
    shapes = [(2 * RET_QK_WIDTH, F32), (2 * RET_V_WIDTH, BF16), (DIFF_QK_WIDTH, BF16), (DIFF_QK_WIDTH, BF16),
              (DIFF_V_WIDTH, BF16)]
    return pl.pallas_call(
        _odd_in_kernel,
        grid=(b, t // tm),
        in_specs=[_rows(tm, d), _mod_spec(layer, b, n_ctx_tiles, d), _norm_spec(layer, d), _resident(w), tab, tab],
        out_specs=[_rows(tm, width) for width, _ in shapes],
        out_shape=[jax.ShapeDtypeStruct((b, t, width), dt) for width, dt in shapes],
        compiler_params=_params(2, VMEM_LIMIT),
        name="odd_in_projection",
    )(x, mod, norm_g, w, tab_ret, tab_diff)


_LEVELS = 6
HGRN_STEP_HEADS = 4


def _gla_masks():
    c = CHUNK
    masks = np.zeros((_LEVELS + 1, c, c), np.float32)
    masks[0] = np.eye(c)
    for lv in range(_LEVELS):
        b = 32 >> lv
        for tau in range(c):
            mid = (tau // (2 * b)) * 2 * b + b
            if tau >= mid:
                masks[lv + 1, tau, mid - b:mid] = 1.0
    mk = np.stack([masks, masks[:, ::-1, ::-1]]).reshape(2, (_LEVELS + 1) * c, c)
    return jnp.asarray(mk, F32)


def _gla_gates(z, lb):
    k = (1.0 - lb) * jax.nn.sigmoid(-z)
    return k, (_log_sigmoid(z) + jnp.log1p(lb * jnp.exp(-z))) * math.log2(math.e)


SUBLANES = 8


def _scan_positions(reverse):
    sub = lax.broadcasted_iota(jnp.int32, (SUBLANES, LANES), 0)
    return (SUBLANES - 1 - sub) if reverse else sub


def _gla_tile_consts(reverse):
    pos = _scan_positions(reverse)
    consts = {("from", s): (pos >= s).astype(F32) for s in (1, 2, 4)}
    consts["right", 1] = (pos % 2 == 1).astype(F32)
    for b in (2, 4):
        consts["sign", b] = jnp.where(pos % (2 * b) >= b, 1.0, -1.0)
    return consts


def _gla_decays(g2, reverse, consts):
    n_tiles = CHUNK // SUBLANES
    pos = _scan_positions(reverse)
    tiles = [g2[v * SUBLANES:(v + 1) * SUBLANES] for v in range(n_tiles)]

    def from_earlier(x, s):
        return pltpu.roll(x, (SUBLANES - s) if reverse else s, 0)

    def at(x, p):
        j = SUBLANES - 1 - p if reverse else p
        return jnp.broadcast_to(x[j:j + 1], x.shape)

    scan = []
    for x in tiles:
        for s in (1, 2, 4):
            x = x + from_earlier(x, s) * consts["from", s]
        scan.append(x)
    order = list(reversed(range(n_tiles))) if reverse else list(range(n_tiles))
    prefix, carry = {}, None
    for v in order:
        prefix[v] = scan[v] if carry is None else scan[v] + carry
        carry = at(prefix[v], SUBLANES - 1)
    total = carry
    p_tiles = [prefix[v] for v in order]

    levels = []
    for lv in range(_LEVELS):
        b = 32 >> lv
        out = []
        for n, p in enumerate(p_tiles):
            if b >= SUBLANES:
                span = 2 * b // SUBLANES
                left_last = (n // span) * span + span // 2 - 1
                edge = at(p_tiles[left_last], SUBLANES - 1)
                out.append(p - edge if n % span >= span // 2 else edge - p)
            elif b == 1:
                out.append(tiles[order[n]] * consts["right", 1])
            else:
                edge = at(p, b - 1)
                for first in range(2 * b, SUBLANES, 2 * b):
                    edge = jnp.where(pos >= first, at(p, first + b - 1), edge)
                out.append((p - edge) * consts["sign", b])
        levels.append(out)

    def rows(tile_list):
        return jnp.concatenate(list(reversed(tile_list)) if reverse else tile_list, axis=0)

    dec = [jnp.exp2(rows(t)) for t in levels]
    return dec, jnp.exp2(rows(p_tiles)), jnp.exp2(rows([total - p for p in p_tiles]))


def _sibling_select(qs, k, b, reverse):
    c = CHUNK
    if b % 8 == 0:
        right = lambda r0: (((c - b - r0) if reverse else r0) // b) % 2 == 1
        return jnp.concatenate([(qs if right(r0) else k)[r0:r0 + b] for r0 in range(0, c, b)], axis=0)
    row = lax.broadcasted_iota(jnp.int32, (c, LANES), 0)
    tau = (c - 1 - row) if reverse else row
    return jnp.where((tau & b) != 0, qs, k)


def _gla_chunk(qs, v, k, g2, masks, st, reverse, consts):
    c = CHUNK
    tile_rows = range(0, c, SUBLANES)
    levels, dec_q, dec_k = _gla_decays(g2, reverse, consts)
    att = masks[0:c] * _dot_nt(qs.astype(BF16), k.astype(BF16))
    att = [att[r0:r0 + SUBLANES] for r0 in tile_rows]
    for lv in range(_LEVELS):
        b = 32 >> lv
        zl = (_sibling_select(qs, k, b, reverse) * levels[lv]).astype(BF16)
        term = _dot_nt(zl, zl)
        for n, r0 in enumerate(tile_rows):
            scan_row = (c - SUBLANES - r0) if reverse else r0
            if b < SUBLANES or (scan_row // b) % 2 == 1:
                att[n] = att[n] + masks[(lv + 1) * c + r0:(lv + 1) * c + r0 + SUBLANES] * term[r0:r0 + SUBLANES]
    att = jnp.concatenate(att, axis=0)
    vb = v.astype(BF16)
    o = _dot(att.astype(BF16), vb) + _dot_nt((qs * dec_q).astype(BF16), st.astype(BF16))
    last = 0 if reverse else c - 1
    st = st * dec_q[last:last + 1, :] + _dot_tn(vb, (k * dec_k).astype(BF16))
    return o, st


def _hgrn_kernel(qf_ref, if_ref, zf_ref, qb_ref, ib_ref, zb_ref, lbf_ref, lbb_ref, mk_ref,
                 of_ref, ob_ref, sf_ref, sb_ref, *, n_chunks):
    @pl.when(pl.program_id(2) == 0)
    def _():
        sf_ref[...] = jnp.zeros_like(sf_ref)
        sb_ref[...] = jnp.zeros_like(sb_ref)

    dirs = ((qf_ref, if_ref, zf_ref, lbf_ref, of_ref, sf_ref), (qb_ref, ib_ref, zb_ref, lbb_ref, ob_ref, sb_ref))
    n_heads = sf_ref.shape[0]
    states = {(d, h): dirs[d][5][h] for d in range(2) for h in range(n_heads)}
    consts = [_gla_tile_consts(d == 1) for d in range(2)]
    for step in range(n_chunks):
        for d, (q_ref, i_ref, z_ref, lb_ref, o_ref, _) in enumerate(dirs):
            ci = step if d == 0 else n_chunks - 1 - step
            rows = slice(ci * CHUNK, (ci + 1) * CHUNK)
            for h, cols in enumerate(_lane_blocks(n_heads * LANES)):
                k, g2 = _gla_gates(z_ref[rows, cols].astype(F32), lb_ref[:, cols])
                o, states[d, h] = _gla_chunk(q_ref[rows, cols].astype(F32), i_ref[rows, cols].astype(F32), k, g2,
                                             mk_ref[d], states[d, h], d == 1, consts[d])
                o_ref[rows, cols] = o
    for (d, h), st in states.items():
        dirs[d][5][h] = st


def _scan_block_maps(n_blocks, n_ctx_blocks):
    def bwd(j):
        return jnp.where(j < n_ctx_blocks, n_ctx_blocks - 1 - j, n_blocks - 1 - (j - n_ctx_blocks))
    return (lambda j: j), bwd


def _hgrn_scan(p, lbs, tb, n_ctx_blocks):
    b, t, _ = p.shape
    nb = t // tb
    fwd, bwd = _scan_block_maps(nb, n_ctx_blocks)
    masks = _gla_masks()

    hs = HGRN_STEP_HEADS
    width = hs * LANES
    groups = HGRN_HEADS // hs

    def col(group, order):
        return pl.BlockSpec((None, tb, width), lambda i, h, j: (i, order(j), group * groups + h))

    def lb_spec(d):
        return pl.BlockSpec((None, 1, width), lambda i, h, j: (d, 0, h))

    def out_spec(order):
        return pl.BlockSpec((None, tb, width), lambda i, h, j: (i, order(j), h))

    out = jax.ShapeDtypeStruct((b, t, HGRN_WIDTH), F32)
    state = pltpu.VMEM((hs, HGRN_DIM, HGRN_DIM), F32)
    return pl.pallas_call(
        functools.partial(_hgrn_kernel, n_chunks=tb // CHUNK),
        grid=(b, groups, nb),
        in_specs=[col(0, fwd), col(1, fwd), col(2, fwd), col(0, bwd), col(1, bwd), col(3, bwd),
                  lb_spec(0), lb_spec(1), _resident(masks)],
        out_specs=[out_spec(fwd), out_spec(bwd)],
        out_shape=[out, out],
        scratch_shapes=[state, state],
        compiler_params=_params(3),
        name="hgrn_scan",
    )(p, p, p, p, p, p, lbs, lbs, masks)


def _key_chunk(n_keys):
    return max(c for c in range(LANES, min(KEY_CHUNK, n_keys) + 1, LANES) if n_keys % c == 0)


def _score_steps(q, k_ref, k_cols, n_keys, s_ref, m_ref):
    kc = _key_chunk(n_keys)
    m = None
    for c in range(n_keys // kc):
        s = _dot_nt(q, k_ref[c * kc:(c + 1) * kc, k_cols])
        s_ref[:, c * kc:(c + 1) * kc] = s
        for blk in _lane_blocks(kc):
            m = s[:, blk] if m is None else jnp.maximum(m, s[:, blk])
        yield
    m_ref[...] = jnp.broadcast_to(jnp.max(m, axis=-1, keepdims=True), m.shape)


def _weight_steps(s_ref, m_ref, v_ref, n_keys, out):
    kc = _key_chunk(n_keys)
    m = m_ref[...]
    ones = jnp.ones((kc, LANES), BF16)
    acc = None
    for c in range(n_keys // kc):
        p = jnp.concatenate([jnp.exp(s_ref[:, c * kc + blk.start:c * kc + blk.stop] - m)
                             for blk in _lane_blocks(kc)], axis=1)
        part = _dot(p.astype(BF16), jnp.concatenate([v_ref[c * kc:(c + 1) * kc, :], ones], axis=1))
        acc = part if acc is None else acc + part
        yield
    out.append(acc)


def _interleave(*step_iters):
    live = list(step_iters)
    while live:
        for it in list(live):
            if next(it, StopIteration) is StopIteration:
                live.remove(it)


def _attention_tiles(n_streams, scores, weights, finish, tq, ctx_len, t):
    def weigh(buf, n_keys, row0, *extra):
        accs = []
        for s in range(n_streams):
            out = []
            _interleave(weights(s, buf, n_keys, out), *[e(s) for e in extra])
            accs.append(out[0])
        finish(row0, accs)

    for i in range(ctx_len // tq):
        for s in range(n_streams):
            _interleave(scores(s, i * tq, 0, ctx_len))
        weigh(0, ctx_len, i * tq)
    n_lat = (t - ctx_len) // tq
    for s in range(n_streams):
        _interleave(scores(s, ctx_len, 0, t))

    def next_scores(row0, buf):
        return lambda s: scores(s, row0, buf, t)

    def pair(i, carry):
        row0 = pl.multiple_of(ctx_len + 2 * i * tq, tq)
        weigh(0, t, row0, next_scores(row0 + tq, 1))
        weigh(1, t, row0 + tq, next_scores(row0 + 2 * tq, 0))
        return carry

    n_pairs = (n_lat - 1) // 2
    lax.fori_loop(0, n_pairs, pair, 0)
    row0 = ctx_len + 2 * n_pairs * tq
    if n_lat % 2 == 0:
        weigh(0, t, row0, next_scores(row0 + tq, 1))
        weigh(1, t, row0 + tq)
    else:
        weigh(0, t, row0)


def _tile_rows(row0, tq):
    return pl.ds(row0 if isinstance(row0, int) else pl.multiple_of(row0, tq), tq)


def _attention_scratch(n_streams, tq, t):
    scores = pltpu.VMEM((n_streams, tq, t), F32)
    row_max = pltpu.VMEM((n_streams, tq, LANES), F32)
    return [scores, scores, row_max, row_max]


def _mla_attn_kernel(q_ref, k_ref, v_ref, o_ref, s0_ref, s1_ref, m0_ref, m1_ref, *, tq, ctx_len):
    heads = _lane_blocks(2 * LANES)
    s_bufs, m_bufs = (s0_ref, s1_ref), (m0_ref, m1_ref)

    def scores(h, row0, buf, n_keys):
        return _score_steps(q_ref[_tile_rows(row0, tq), heads[h]], k_ref, heads[h], n_keys,
                            s_bufs[buf].at[h], m_bufs[buf].at[h])

    def weights(h, buf, n_keys, out):
        return _weight_steps(s_bufs[buf].at[h], m_bufs[buf].at[h], v_ref, n_keys, out)

    def finish(row0, accs):
        o = [a[:, :LANES] / a[:, LANES:] for a in accs]
        lane = lax.broadcasted_iota(jnp.int32, o[0].shape, 1)
        o_ref[_tile_rows(row0, tq), :] = jnp.where(lane < MLA_V, o[0], o[1]).astype(BF16)

    _attention_tiles(2, scores, weights, finish, tq, ctx_len, k_ref.shape[0])


def _mla_attention(q, k, v, tq, ctx_len):
    b, t, _ = q.shape
    return pl.pallas_call(
        functools.partial(_mla_attn_kernel, tq=tq, ctx_len=ctx_len),
        grid=(b, MLA_HEADS // 2),
        in_specs=[pl.BlockSpec((None, t, 2 * LANES), lambda i, h: (i, 0, h)),
                  pl.BlockSpec((None, t, 2 * LANES), lambda i, h: (i, 0, h)),
                  pl.BlockSpec((None, t, LANES), lambda i, h: (i, 0, h))],
        out_specs=pl.BlockSpec((None, t, LANES), lambda i, h: (i, 0, h)),
        out_shape=jax.ShapeDtypeStruct((b, t, MLA_HEADS * MLA_V), BF16),
        scratch_shapes=_attention_scratch(2, tq, t),
        compiler_params=_params(2, VMEM_LIMIT),
        name="mla_attention",
    )(q, k, v)


def _ret_chunk(q, k, v, decays, st, head_lanes):
    dmat, q_dec, k_dec, c_dec = decays
    qm = jnp.where(head_lanes, q, 0.0)
    att = _dot_nt(qm.astype(BF16), k.astype(BF16)) * dmat
    o = _dot(att.astype(BF16), v) + _dot_nt((qm * q_dec).astype(BF16), st.astype(BF16))
    st = st * c_dec + _dot_tn(v, (k * k_dec).astype(BF16))
    return o, st


def _ret_decays(lg, reverse):
    c = CHUNK
    row = lax.broadcasted_iota(jnp.int32, (c, LANES), 0)
    tau = ((c - 1 - row) if reverse else row).astype(F32)
    ti = lax.broadcasted_iota(jnp.int32, (c, c), 0)
    si = lax.broadcasted_iota(jnp.int32, (c, c), 1)
    rel = ((si - ti) if reverse else (ti - si)).astype(F32)
    dmat = jnp.where(rel >= 0, jnp.exp(jnp.maximum(rel, 0.0) * lg[:, :c]), 0.0)
    return dmat, jnp.exp((tau + 1.0) * lg), jnp.exp((c - 1.0 - tau) * lg), jnp.exp(c * lg)


def _ret_kernel(qkf_ref, vf_ref, qkb_ref, vb_ref, lg_ref, of_ref, ob_ref, sf_ref, sb_ref, *, n_chunks):
    @pl.when(pl.program_id(1) == 0)
    def _():
        sf_ref[...] = jnp.zeros_like(sf_ref)
        sb_ref[...] = jnp.zeros_like(sb_ref)

    per_block = LANES // RET_QK
    lane = lax.broadcasted_iota(jnp.int32, (CHUNK, LANES), 1)
    dirs = ((qkf_ref, vf_ref, of_ref, sf_ref), (qkb_ref, vb_ref, ob_ref, sb_ref))
    chains = [(d, h) for d in range(2) for h in range(RET_HEADS)]
    decays = {(d, h): _ret_decays(_log_sigmoid(lg_ref[d, h]), d == 1) for d, h in chains}
    states = {(d, h): dirs[d][3][h] for d, h in chains}
    for step in range(n_chunks):
        for d, h in chains:
            qk_ref, v_ref, o_ref, _ = dirs[d]
            ci = step if d == 0 else n_chunks - 1 - step
            rows = slice(ci * CHUNK, (ci + 1) * CHUNK)
            q_cols = _lane_blocks(2 * RET_QK_WIDTH)[h // per_block]
            k_cols = _lane_blocks(2 * RET_QK_WIDTH)[RET_QK_WIDTH // LANES + h // per_block]
            v_cols = _lane_blocks(RET_V_WIDTH)[h]
            o, states[d, h] = _ret_chunk(qk_ref[rows, q_cols], qk_ref[rows, k_cols], v_ref[rows, v_cols],
                                         decays[d, h], states[d, h], (lane // RET_QK) == (h % per_block))
            o_ref[rows, v_cols] = o
    for (d, h), st in states.items():
        dirs[d][3][h] = st


def _ret_scan(qk, vg, decay, tb, n_ctx_blocks):
    b, t, _ = qk.shape
    nb = t // tb
    fwd, bwd = _scan_block_maps(nb, n_ctx_blocks)

    def rows(width, order):
        return pl.BlockSpec((None, tb, width), lambda i, j: (i, order(j), 0))

    out = jax.ShapeDtypeStruct((b, t, RET_V_WIDTH), F32)
    state = pltpu.VMEM((RET_HEADS, RET_V, LANES), F32)
    return pl.pallas_call(
        functools.partial(_ret_kernel, n_chunks=tb // CHUNK),
        grid=(b, nb),
        in_specs=[rows(2 * RET_QK_WIDTH, fwd), rows(RET_V_WIDTH, fwd), rows(2 * RET_QK_WIDTH, bwd),
                  rows(RET_V_WIDTH, bwd), _resident(decay)],
        out_specs=[rows(RET_V_WIDTH, fwd), rows(RET_V_WIDTH, bwd)],
        out_shape=[out, out],
        scratch_shapes=[state, state],
        compiler_params=_params(2),
        name="retention_scan",
    )(qk, vg, qk, vg, decay)


def _diff_attn_kernel(q_ref, k_ref, v_ref, lam_ref, g_ref, o_ref, s0_ref, s1_ref, m0_ref, m1_ref,
                      *, tq, ctx_len, lambda_init):
    lp = lam_ref[...]
    lam = (jnp.exp(jnp.sum(lp[0:1] * lp[1:2], axis=-1, keepdims=True))
           - jnp.exp(jnp.sum(lp[2:3] * lp[3:4], axis=-1, keepdims=True)) + lambda_init)

    s_bufs, m_bufs = (s0_ref, s1_ref), (m0_ref, m1_ref)
    all_lanes = slice(0, LANES)

    def scores(i, row0, buf, n_keys):
        q = q_ref[_tile_rows(row0, tq), :]
        first = lax.broadcasted_iota(jnp.int32, q.shape, 1) < DIFF_QK
        q = jnp.where(first == (i == 0), q, jnp.zeros_like(q))
        return _score_steps(q, k_ref, all_lanes, n_keys, s_bufs[buf].at[i], m_bufs[buf].at[i])

    def weights(i, buf, n_keys, out):
        return _weight_steps(s_bufs[buf].at[i], m_bufs[buf].at[i], v_ref, n_keys, out)

    def finish(row0, accs):
        o = accs[0][:, :LANES] / accs[0][:, LANES:] - lam * (accs[1][:, :LANES] / accs[1][:, LANES:])
        o_ref[_tile_rows(row0, tq), :] = (_rms(o) * g_ref[...] * (1.0 - lambda_init)).astype(BF16)

    _attention_tiles(2, scores, weights, finish, tq, ctx_len, k_ref.shape[0])


def _diff_attention(q, k, v, lam_params, subln_g, tq, ctx_len, lambda_init):
    b, t, _ = q.shape
    head = pl.BlockSpec((None, t, LANES), lambda i, h: (i, 0, h))
    return pl.pallas_call(
        functools.partial(_diff_attn_kernel, tq=tq, ctx_len=ctx_len, lambda_init=lambda_init),
        grid=(b, DIFF_HEADS),
        in_specs=[head, head, head, _resident(lam_params), _resident(subln_g)],
        out_specs=head,
        out_shape=jax.ShapeDtypeStruct((b, t, DIFF_V_WIDTH), BF16),
        scratch_shapes=_attention_scratch(2, tq, t),
        compiler_params=_params(2, VMEM_LIMIT),
        name="diff_attention",
    )(q, k, v, lam_params, subln_g)


def _mix_ffn_kernel(*refs, n_stream, n_ctx_tiles, row_off, final):
    (of_ref, ob_ref, gate_ref, mix_ref, ng_ref, mod_ref, g_ref, wo_ref, wg_ref, wu_ref, wd_ref) = refs[n_stream:n_stream + 11]
    rest = refs[n_stream + 11:]
    o_ref = rest[-1]
    x_in = _stream_tile(refs[:n_stream], n_ctx_tiles, row_off)
    o = of_ref[...] + ob_ref[...]
    gate = gate_ref[...].astype(F32)
    a = jnp.concatenate([(_rms(o[:, cols]) * ng_ref[...] * _silu(gate[:, cols])).astype(BF16)
                         for cols in _lane_blocks(o.shape[1])], axis=1)
    wa = a.shape[1]
    u = _dot(a, wo_ref[0:wa, :]) + _dot(mix_ref[...], wo_ref[wa:, :])
    x = x_in + mod_ref[2:3, :] * u
    h = (_rms(x) * g_ref[1:2, :] * (1.0 + mod_ref[4:5, :]) + mod_ref[3:4, :]).astype(BF16)
    act = (_silu(_dot(h, wg_ref[...])) * _dot(h, wu_ref[...])).astype(BF16)
    x = x + mod_ref[5:6, :] * _dot(act, wd_ref[...])
    o_ref[...] = _rms(x) * rest[0][...] if final else x


def _mix_ffn(stream, o_f, o_b, gates, gate_block, mix, norm_gain, mod, norm_g, w_out, wg, wu, wd, layer, tm,
             n_ctx_tiles, final_g=None):
    b, t, wa = o_f.shape
    d = w_out.shape[1]
    final = final_g is not None
    off = n_ctx_tiles if final else 0
    n_rows = t - off * tm
    stream_specs, stream_args = _stream_specs(stream, tm, n_ctx_tiles, off)
    in_specs = stream_specs + [
        _rows(tm, wa, off), _rows(tm, wa, off), pl.BlockSpec((None, tm, wa), lambda i, j: (i, j + off, gate_block)),
        _rows(tm, mix.shape[2], off), _resident(norm_gain), _mod_spec(layer, b, n_ctx_tiles, d, off),
        _norm_spec(layer, d), _resident(w_out), _resident(wg), _resident(wu), _resident(wd)]
    args = stream_args + [o_f, o_b, gates, mix, norm_gain, mod, norm_g, w_out, wg, wu, wd]
    if final:
        in_specs.append(_resident(final_g))
        args.append(final_g)
    return pl.pallas_call(
        functools.partial(_mix_ffn_kernel, n_stream=len(stream_args), n_ctx_tiles=n_ctx_tiles, row_off=off,
                          final=final),
        grid=(b, n_rows // tm),
        in_specs=in_specs,
        out_specs=_rows(tm, d),
        out_shape=jax.ShapeDtypeStruct((b, n_rows, d), F32),
        compiler_params=_params(2, VMEM_LIMIT),
        name="mix_ffn",
    )(*args)


def _rope_tables(ctx_len, seq, kind, scale=1.0):
    f32 = np.float32
    lane = np.arange(LANES)
    n = np.arange(seq)
    if kind == "ret":
        half = RET_QK // 2
        idx = lane % RET_QK
        active = np.ones(LANES, bool)
        pos = np.broadcast_to(n[:, None], (seq, LANES))
    else:
        width = DIFF_QK if kind == "diff" else MLA_ROPE
        half = width // 4
        if kind == "diff":
            idx = lane % (width // 2)
            by_col = (lane % width) >= width // 2
            active = np.ones(LANES, bool)
        else:
            idx = (lane - MLA_NOPE) % (width // 2)
            by_col = (lane - MLA_NOPE) >= width // 2
            active = (lane >= MLA_NOPE) & (lane < MLA_NOPE + width)
        pos = np.where(by_col[None, :], (n % GRID_W)[:, None], (n // GRID_W)[:, None])
    upper = (idx >= half)[None, :]
    inv = f32(ROPE_BASE) ** (-(idx % half).astype(f32) / f32(half))
    ang = pos.astype(f32) * inv[None, :].astype(f32)
    cos = np.where(active[None, :], np.cos(ang), f32(1.0)).astype(f32)
    sin = np.where(active[None, :], np.sin(ang), f32(0.0)).astype(f32)
    lat = np.stack([cos, np.where(upper, sin, f32(0.0)), np.where(upper, f32(0.0), -sin)])
    ident = np.stack([np.ones((ctx_len, LANES), f32), np.zeros((ctx_len, LANES), f32),
                      np.zeros((ctx_len, LANES), f32)])
    return jnp.asarray(np.concatenate([ident, lat], axis=1) * f32(scale), F32)


def kernel(x, c, ctx, c_ctx, ada_w, ada_b, norm_g, mix_w_out, ffn_w_gate, ffn_w_up, ffn_w_down,
           even_w_in, hgrn_lb_logits, hgrn_norm_g, mla_q_norm_g, mla_w_uq, mla_kv_norm_g, mla_w_ukv,
           odd_w_in, ret_decay_logits, diff_lambda, diff_subln_g, final_norm_g):
    batch, seq, d = x.shape
    ctx_len = ctx.shape[1]
    depth = ada_w.shape[0]
    tm = math.gcd(ROW_TILE, math.gcd(ctx_len, seq))
    n_ctx_tiles = ctx_len // tm
    assert tm % (2 * CHUNK) == 0 and seq % GRID_W == 0

    mod_rows = -(-(batch + 1) // 8) * 8
    s_rows = jnp.concatenate([c, c_ctx[None, :], jnp.zeros((mod_rows - batch - 1, d), F32)], axis=0)
    mod = _modulation(s_rows, ada_w, ada_b).reshape(depth, mod_rows, 6, d)

    lb_soft = jax.nn.softmax(hgrn_lb_logits.astype(F32), axis=0)
    lbs = jnp.cumsum(lb_soft, axis=0) - lb_soft[0:1]

    tab_mla_q = _rope_tables(ctx_len, seq, "mla", (MLA_NOPE + MLA_ROPE) ** -0.5)
    tab_mla_k = _rope_tables(ctx_len, seq, "mla")
    tab_ret = _rope_tables(ctx_len, seq, "ret")
    tab_diff = _rope_tables(ctx_len, seq, "diff")

    stream = (ctx, x)
    even_w = _even_weights(even_w_in, tm)
    for layer in range(depth):
        if layer % 2 == 0:
            e = layer // 2
            wq = mla_w_uq[e].reshape(MLA_Q_RANK, MLA_HEADS, MLA_NOPE + MLA_ROPE)
            wq = jnp.pad(wq, ((0, 0), (0, 0), (0, LANES - MLA_NOPE - MLA_ROPE))).reshape(MLA_Q_RANK, -1)
            wkv = mla_w_ukv[e].reshape(MLA_KV_RANK, MLA_HEADS, MLA_NOPE + MLA_V)
            wk = jnp.pad(wkv[:, :, :MLA_NOPE], ((0, 0), (0, 0), (0, LANES - MLA_NOPE))).reshape(MLA_KV_RANK, -1)
            wv = wkv[:, :, MLA_NOPE:].reshape(MLA_KV_RANK, -1)
            gates, q, k, v = _even_in_projection(
                stream, ctx_len + seq, mod, norm_g, even_w[e], mla_q_norm_g[e][None, :], mla_kv_norm_g[e][None, :],
                wq.astype(BF16),
                jnp.concatenate([wk, wv], axis=1).astype(BF16), tab_mla_q, tab_mla_k, layer, tm, n_ctx_tiles)
            o_f, o_b = _hgrn_scan(gates, lbs[e].reshape(2, 1, HGRN_WIDTH), tm, n_ctx_tiles)
            mix = _mla_attention(q, k, v, tm, ctx_len)
            gate_block, gain = 4, hgrn_norm_g[e][None, :]
        else:
            o = layer // 2
            qk_ret, gates, q_d, k_d, v_d = _odd_in_projection(stream, mod, norm_g, odd_w_in[o].astype(BF16),
                                                              tab_ret, tab_diff, layer, tm, n_ctx_tiles)
            decay = jnp.broadcast_to(ret_decay_logits[o][:, :, None, None], (2, RET_HEADS, 1, LANES))
            o_f, o_b = _ret_scan(qk_ret, gates, decay, tm, n_ctx_tiles)
            lambda_init = 0.8 - 0.6 * math.exp(-0.3 * layer)
            mix = _diff_attention(q_d, k_d, v_d, diff_lambda[o], diff_subln_g[o][None, :], tm, ctx_len, lambda_init)
            gate_block, gain = 1, jnp.ones((1, LANES), F32)
        stream = _mix_ffn(stream, o_f, o_b, gates, gate_block, mix, gain, mod, norm_g, mix_w_out[layer].astype(BF16),
                          ffn_w_gate[layer].astype(BF16), ffn_w_up[layer].astype(BF16),
                          ffn_w_down[layer].astype(BF16), layer, tm, n_ctx_tiles,
                          final_norm_g[None, :] if layer == depth - 1 else None)
    return stream
```

```python
import functools
import math

import numpy as np
import jax
import jax.numpy as jnp
from jax import lax
from jax.experimental import pallas as pl
from jax.experimental.pallas import tpu as pltpu

F32 = jnp.float32
BF16 = jnp.bfloat16

LANES = 128
GRID_W = 64
CHUNK = 64
ROPE_BASE = 10000.0
EPS = 1e-6

HGRN_HEADS, HGRN_DIM = 4, 128
HGRN_WIDTH = HGRN_HEADS * HGRN_DIM
MLA_HEADS, MLA_Q_RANK, MLA_KV_RANK, MLA_NOPE, MLA_ROPE, MLA_V = 8, 384, 256, 64, 32, 64
RET_HEADS, RET_QK, RET_V = 4, 64, 128
DIFF_HEADS, DIFF_QK, DIFF_V = 4, 64, 128
RET_QK_WIDTH = RET_HEADS * RET_QK
RET_V_WIDTH = RET_HEADS * RET_V
DIFF_QK_WIDTH = DIFF_HEADS * 2 * DIFF_QK
DIFF_V_WIDTH = DIFF_HEADS * DIFF_V

EV_HGRN = 5 * HGRN_WIDTH
EV_CQ = 0
EV_CKV = EV_CQ + MLA_Q_RANK
EV_KROPE = EV_CKV + MLA_KV_RANK
EV_HGRN0 = EV_KROPE + LANES
OD_KR = RET_QK_WIDTH
OD_VR = OD_KR + RET_QK_WIDTH
OD_QD = OD_VR + 2 * RET_V_WIDTH
OD_KD = OD_QD + DIFF_QK_WIDTH
OD_VD = OD_KD + DIFF_QK_WIDTH

ROW_TILE = 256
KEY_CHUNK = 768
VMEM_LIMIT = 48 * 1024 * 1024


def _params(n_axes, vmem=None):
    return pltpu.CompilerParams(dimension_semantics=("arbitrary",) * n_axes, vmem_limit_bytes=vmem)


def _resident(a, index=None):
    if index is None:
        return pl.BlockSpec(a.shape, lambda *_: (0,) * a.ndim, pipeline_mode=pl.Buffered(1))
    return pl.BlockSpec((None,) + a.shape[1:], lambda *_: (index,) + (0,) * (a.ndim - 1),
                        pipeline_mode=pl.Buffered(1))


def _dot(a, b):
    return jnp.dot(a, b, preferred_element_type=F32)


def _dot_nt(a, b):
    return lax.dot_general(a, b, (((1,), (1,)), ((), ())), preferred_element_type=F32)


def _dot_tn(a, b):
    return lax.dot_general(a, b, (((0,), (0,)), ((), ())), preferred_element_type=F32)


def _rms(x):
    return x * lax.rsqrt(jnp.mean(x * x, axis=-1, keepdims=True) + EPS)


def _silu(x):
    return x * jax.nn.sigmoid(x)


def _log_sigmoid(z):
    return jnp.minimum(z, 0.0) - jnp.log1p(jnp.exp(-jnp.abs(z)))


def _rope(x, tab_ref, shift):
    return x * tab_ref[0] + pltpu.roll(x, shift, 1) * tab_ref[1] + pltpu.roll(x, LANES - shift, 1) * tab_ref[2]


def _lane_blocks(width):
    return [slice(i * LANES, (i + 1) * LANES) for i in range(width // LANES)]


def _mod_kernel(s_ref, w_ref, b_ref, o_ref):
    s = _silu(s_ref[...])
    rows = s.shape[0]
    s_hi = s.astype(BF16)
    rest = s - s_hi.astype(F32)
    s_mid = rest.astype(BF16)
    pieces = jnp.concatenate([s_hi, s_mid, (rest - s_mid.astype(F32)).astype(BF16)], axis=0)
    w = w_ref[...]
    w_hi = w.astype(BF16)
    acc = _dot(pieces, w_hi) + _dot(pieces, (w - w_hi.astype(F32)).astype(BF16))
    o_ref[...] = acc[:rows] + acc[rows:2 * rows] + acc[2 * rows:] + b_ref[...]


def _modulation(s_rows, ada_w, ada_b):
    depth, d, d6 = ada_w.shape
    rows = s_rows.shape[0]
    tn = d
    return pl.pallas_call(
        _mod_kernel,
        grid=(depth, d6 // tn),
        in_specs=[pl.BlockSpec((rows, d), lambda l, n: (0, 0)),
                  pl.BlockSpec((None, d, tn), lambda l, n: (l, 0, n)),
                  pl.BlockSpec((None, 1, tn), lambda l, n: (l, 0, n))],
        out_specs=pl.BlockSpec((None, rows, tn), lambda l, n: (l, 0, n)),
        out_shape=jax.ShapeDtypeStruct((depth, rows, d6), F32),
        compiler_params=_params(2),
        name="modulation",
    )(s_rows, ada_w, ada_b.reshape(depth, 1, d6))


def _mod_spec(layer, batch, n_ctx_tiles, d, row_off=0):
    return pl.BlockSpec((None, None, 6, d),
                        lambda b, t: (layer, jnp.where(t + row_off < n_ctx_tiles, batch, b), 0, 0))


def _norm_spec(layer, d):
    return pl.BlockSpec((None, 2, d), lambda i, j: (layer, 0, 0))


def _rows(tm, width, row_off=0):
    return pl.BlockSpec((None, tm, width), lambda i, j: (i, j + row_off, 0))


def _stream_specs(stream, tm, n_ctx_tiles, row_off=0):
    if not isinstance(stream, tuple):
        return [_rows(tm, stream.shape[2], row_off)], [stream]
    d = stream[0].shape[2]
    return ([pl.BlockSpec((None, tm, d), lambda i, j: (i, jnp.minimum(j + row_off, n_ctx_tiles - 1), 0)),
             pl.BlockSpec((None, tm, d), lambda i, j: (i, jnp.maximum(j + row_off - n_ctx_tiles, 0), 0))],
            list(stream))


def _stream_tile(refs, n_ctx_tiles, row_off=0):
    if len(refs) == 1:
        return refs[0][...]
    return jnp.where(pl.program_id(1) + row_off < n_ctx_tiles, refs[0][...], refs[1][...])


def _even_in_kernel(*refs, n_ctx_tiles):
    (mod_ref, g_ref, w_ref, gq_ref, gkv_ref, wq_ref, wkv_ref, tq_ref, tk_ref, hg_ref, q_ref, k_ref, v_ref) = refs[-13:]
    shift = MLA_ROPE // 4
    x = _stream_tile(refs[:-13], n_ctx_tiles)
    h = (_rms(x) * g_ref[0:1, :] * (1.0 + mod_ref[1:2, :]) + mod_ref[0:1, :]).astype(BF16)
    p = _dot(h, w_ref[:, :EV_HGRN0])
    hg_ref[...] = _dot(h, w_ref[:, EV_HGRN0:]).astype(BF16)
    q = _dot((_rms(p[:, EV_CQ:EV_CKV]) * gq_ref[...]).astype(BF16), wq_ref[...])
    kv = _dot((_rms(p[:, EV_CKV:EV_KROPE]) * gkv_ref[...]).astype(BF16), wkv_ref[...])
    k_rope = _rope(pltpu.roll(p[:, EV_KROPE:], MLA_NOPE, 1), tk_ref, shift)
    for cols in _lane_blocks(MLA_HEADS * LANES):
        q_ref[:, cols] = _rope(q[:, cols], tq_ref, shift).astype(BF16)
        k_ref[:, cols] = (kv[:, cols] + k_rope).astype(BF16)
    v_ref[...] = kv[:, MLA_HEADS * LANES:].astype(BF16)


def _even_in_projection(stream, t, mod, norm_g, w, w_index, gq, gkv, wq, wkv, tab_q, tab_k, layer, tm, n_ctx_tiles):
    d = w.shape[1]
    b = stream[0].shape[0] if isinstance(stream, tuple) else stream.shape[0]
    hq = MLA_HEADS * LANES
    tab = pl.BlockSpec((3, tm, LANES), lambda i, j: (0, j, 0))
    shapes = [(EV_HGRN, BF16), (hq, BF16), (hq, BF16), (MLA_HEADS * MLA_V, BF16)]
    stream_specs, stream_args = _stream_specs(stream, tm, n_ctx_tiles)
    return pl.pallas_call(
        functools.partial(_even_in_kernel, n_ctx_tiles=n_ctx_tiles),
        grid=(b, t // tm),
        in_specs=stream_specs + [_mod_spec(layer, b, n_ctx_tiles, d), _norm_spec(layer, d), _resident(w, w_index),
                                 _resident(gq), _resident(gkv), _resident(wq), _resident(wkv), tab, tab],
        out_specs=[_rows(tm, width) for width, _ in shapes],
        out_shape=[jax.ShapeDtypeStruct((b, t, width), dt) for width, dt in shapes],
        compiler_params=_params(2, VMEM_LIMIT),
        name="even_in_projection",
    )(*stream_args, mod, norm_g, w, gq, gkv, wq, wkv, tab_q, tab_k)


def _even_w_kernel(w_ref, o_ref):
    n_low = w_ref.shape[1] - EV_HGRN
    o_ref[:, :n_low] = w_ref[:, EV_HGRN:].astype(BF16)
    o_ref[:, n_low:EV_HGRN0] = jnp.zeros((o_ref.shape[0], EV_HGRN0 - n_low), BF16)
    o_ref[:, EV_HGRN0:EV_HGRN0 + HGRN_WIDTH] = (w_ref[:, :HGRN_WIDTH] * (HGRN_DIM ** -0.5)).astype(BF16)
    o_ref[:, EV_HGRN0 + HGRN_WIDTH:] = w_ref[:, HGRN_WIDTH:EV_HGRN].astype(BF16)


def _even_weights(even_w_in, tm):
    n, d, cols = even_w_in.shape
    return pl.pallas_call(
        _even_w_kernel,
        grid=(n, d // tm),
        in_specs=[pl.BlockSpec((None, tm, cols), lambda e, r: (e, r, 0))],
        out_specs=pl.BlockSpec((None, tm, EV_HGRN0 + EV_HGRN), lambda e, r: (e, r, 0)),
        out_shape=jax.ShapeDtypeStruct((n, d, EV_HGRN0 + EV_HGRN), BF16),
        compiler_params=_params(2),
        name="even_weights",
    )(even_w_in)


def _odd_in_kernel(x_ref, mod_ref, g_ref, w_ref, tr_ref, td_ref, qkr_ref, vg_ref, qd_ref, kd_ref, vd_ref):
    h = _rms(x_ref[...]) * g_ref[0:1, :] * (1.0 + mod_ref[1:2, :]) + mod_ref[0:1, :]
    p = _dot(h.astype(BF16), w_ref[...])
    for cols in _lane_blocks(RET_QK_WIDTH):
        qkr_ref[:, cols] = _rope(p[:, cols], tr_ref, RET_QK // 2)
        kc = slice(cols.start + OD_KR, cols.stop + OD_KR)
        qkr_ref[:, kc] = _rope(p[:, kc], tr_ref, RET_QK // 2) * (RET_QK ** -0.5)
    vg_ref[...] = p[:, OD_VR:OD_QD].astype(BF16)
    for cols in _lane_blocks(DIFF_QK_WIDTH):
        qc = slice(cols.start + OD_QD, cols.stop + OD_QD)
        kc = slice(cols.start + OD_KD, cols.stop + OD_KD)
        qd_ref[:, cols] = (_rope(p[:, qc], td_ref, DIFF_QK // 4) * (DIFF_QK ** -0.5)).astype(BF16)
        kd_ref[:, cols] = _rope(p[:, kc], td_ref, DIFF_QK // 4).astype(BF16)
    vd_ref[...] = p[:, OD_VD:].astype(BF16)


def _odd_in_projection(x, mod, norm_g, w, w_index, tab_ret, tab_diff, layer, tm, n_ctx_tiles):
    b, t, d = x.shape
    tab = pl.BlockSpec((3, tm, LANES), lambda i, j: (0, j, 0))
    shapes = [(2 * RET_QK_WIDTH, F32), (2 * RET_V_WIDTH, BF16), (DIFF_QK_WIDTH, BF16), (DIFF_QK_WIDTH, BF16),
              (DIFF_V_WIDTH, BF16)]
    return pl.pallas_call(
        _odd_in_kernel,
        grid=(b, t // tm),
        in_specs=[_rows(tm, d), _mod_spec(layer, b, n_ctx_tiles, d), _norm_spec(layer, d), _resident(w, w_index),
                  tab, tab],
        out_specs=[_rows(tm, width) for width, _ in shapes],
        out_shape=[jax.ShapeDtypeStruct((b, t, width), dt) for width, dt in shapes],
        compiler_params=_params(2, VMEM_LIMIT),
        name="odd_in_projection",
    )(x, mod, norm_g, w, tab_ret, tab_diff)


_LEVELS = 6
HGRN_STEP_HEADS = 4


def _gla_masks():
    c = CHUNK
    masks = np.zeros((_LEVELS + 1, c, c), np.float32)
    masks[0] = np.eye(c)
    for lv in range(_LEVELS):
        b = 32 >> lv
        for tau in range(c):
            mid = (tau // (2 * b)) * 2 * b + b
            if tau >= mid:
                masks[lv + 1, tau, mid - b:mid] = 1.0
    mk = np.stack([masks, masks[:, ::-1, ::-1]]).reshape(2, (_LEVELS + 1) * c, c)
    return jnp.asarray(mk, F32)


def _gla_gates(z, lb):
    k = (1.0 - lb) * jax.nn.sigmoid(-z)
    return k, (_log_sigmoid(z) + jnp.log1p(lb * jnp.exp(-z))) * math.log2(math.e)


SUBLANES = 8


def _scan_positions(reverse):
    sub = lax.broadcasted_iota(jnp.int32, (SUBLANES, LANES), 0)
    return (SUBLANES - 1 - sub) if reverse else sub


def _gla_tile_consts(reverse):
    pos = _scan_positions(reverse)
    consts = {("from", s): (pos >= s).astype(F32) for s in (1, 2, 4)}
    consts["right", 1] = (pos % 2 == 1).astype(F32)
    for b in (2, 4):
        consts["sign", b] = jnp.where(pos % (2 * b) >= b, 1.0, -1.0)
    return consts


def _gla_decays(g2, reverse, consts):
    n_tiles = CHUNK // SUBLANES
    pos = _scan_positions(reverse)
    tiles = [g2[v * SUBLANES:(v + 1) * SUBLANES] for v in range(n_tiles)]

    def from_earlier(x, s):
        return pltpu.roll(x, (SUBLANES - s) if reverse else s, 0)

    def at(x, p):
        j = SUBLANES - 1 - p if reverse else p
        return jnp.broadcast_to(x[j:j + 1], x.shape)

    scan = []
    for x in tiles:
        for s in (1, 2, 4):
            x = x + from_earlier(x, s) * consts["from", s]
        scan.append(x)
    order = list(reversed(range(n_tiles))) if reverse else list(range(n_tiles))
    prefix, carry = {}, None
    for v in order:
        prefix[v] = scan[v] if carry is None else scan[v] + carry
        carry = at(prefix[v], SUBLANES - 1)
    total = carry
    p_tiles = [prefix[v] for v in order]

    levels = []
    for lv in range(_LEVELS):
        b = 32 >> lv
        out = []
        for n, p in enumerate(p_tiles):
            if b >= SUBLANES:
                span = 2 * b // SUBLANES
                left_last = (n // span) * span + span // 2 - 1
                edge = at(p_tiles[left_last], SUBLANES - 1)
                out.append(p - edge if n % span >= span // 2 else edge - p)
            elif b == 1:
                out.append(tiles[order[n]] * consts["right", 1])
            else:
                edge = at(p, b - 1)
                for first in range(2 * b, SUBLANES, 2 * b):
                    edge = jnp.where(pos >= first, at(p, first + b - 1), edge)
                out.append((p - edge) * consts["sign", b])
        levels.append(out)

    def rows(tile_list):
        return jnp.concatenate(list(reversed(tile_list)) if reverse else tile_list, axis=0)

    dec = [jnp.exp2(rows(t)) for t in levels]
    return dec, jnp.exp2(rows(p_tiles)), jnp.exp2(rows([total - p for p in p_tiles]))


def _sibling_select(qs, k, b, reverse):
    c = CHUNK
    if b % 8 == 0:
        right = lambda r0: (((c - b - r0) if reverse else r0) // b) % 2 == 1
        return jnp.concatenate([(qs if right(r0) else k)[r0:r0 + b] for r0 in range(0, c, b)], axis=0)
    row = lax.broadcasted_iota(jnp.int32, (c, LANES), 0)
    tau = (c - 1 - row) if reverse else row
    return jnp.where((tau & b) != 0, qs, k)


def _gla_chunk(qs, v, k, g2, masks, st, reverse, consts):
    c = CHUNK
    tile_rows = range(0, c, SUBLANES)
    levels, dec_q, dec_k = _gla_decays(g2, reverse, consts)
    att = masks[0:c] * _dot_nt(qs.astype(BF16), k.astype(BF16))
    att = [att[r0:r0 + SUBLANES] for r0 in tile_rows]
    for lv in range(_LEVELS):
        b = 32 >> lv
        zl = (_sibling_select(qs, k, b, reverse) * levels[lv]).astype(BF16)
        term = _dot_nt(zl, zl)
        for n, r0 in enumerate(tile_rows):
            scan_row = (c - SUBLANES - r0) if reverse else r0
            if b < SUBLANES or (scan_row // b) % 2 == 1:
                att[n] = att[n] + masks[(lv + 1) * c + r0:(lv + 1) * c + r0 + SUBLANES] * term[r0:r0 + SUBLANES]
    att = jnp.concatenate(att, axis=0)
    vb = v.astype(BF16)
    o = _dot(att.astype(BF16), vb) + _dot_nt((qs * dec_q).astype(BF16), st.astype(BF16))
    last = 0 if reverse else c - 1
    st = st * dec_q[last:last + 1, :] + _dot_tn(vb, (k * dec_k).astype(BF16))
    return o, st


def _hgrn_kernel(qf_ref, if_ref, zf_ref, qb_ref, ib_ref, zb_ref, lbf_ref, lbb_ref, mk_ref,
                 of_ref, ob_ref, sf_ref, sb_ref, *, n_chunks):
    @pl.when(pl.program_id(2) == 0)
    def _():
        sf_ref[...] = jnp.zeros_like(sf_ref)
        sb_ref[...] = jnp.zeros_like(sb_ref)

    dirs = ((qf_ref, if_ref, zf_ref, lbf_ref, of_ref, sf_ref), (qb_ref, ib_ref, zb_ref, lbb_ref, ob_ref, sb_ref))
    n_heads = sf_ref.shape[0]
    states = {(d, h): dirs[d][5][h] for d in range(2) for h in range(n_heads)}
    consts = [_gla_tile_consts(d == 1) for d in range(2)]
    for step in range(n_chunks):
        for d, (q_ref, i_ref, z_ref, lb_ref, o_ref, _) in enumerate(dirs):
            ci = step if d == 0 else n_chunks - 1 - step
            rows = slice(ci * CHUNK, (ci + 1) * CHUNK)
            for h, cols in enumerate(_lane_blocks(n_heads * LANES)):
                k, g2 = _gla_gates(z_ref[rows, cols].astype(F32), lb_ref[:, cols])
                o, states[d, h] = _gla_chunk(q_ref[rows, cols].astype(F32), i_ref[rows, cols].astype(F32), k, g2,
                                             mk_ref[d], states[d, h], d == 1, consts[d])
                o_ref[rows, cols] = o
    for (d, h), st in states.items():
        dirs[d][5][h] = st


def _scan_block_maps(n_blocks, n_ctx_blocks):
    def bwd(j):
        return jnp.where(j < n_ctx_blocks, n_ctx_blocks - 1 - j, n_blocks - 1 - (j - n_ctx_blocks))
    return (lambda j: j), bwd


def _hgrn_scan(p, lbs, tb, n_ctx_blocks):
    b, t, _ = p.shape
    nb = t // tb
    fwd, bwd = _scan_block_maps(nb, n_ctx_blocks)
    masks = _gla_masks()

    hs = HGRN_STEP_HEADS
    width = hs * LANES
    groups = HGRN_HEADS // hs

    def col(group, order):
        return pl.BlockSpec((None, tb, width), lambda i, h, j: (i, order(j), group * groups + h))

    def lb_spec(d):
        return pl.BlockSpec((None, 1, width), lambda i, h, j: (d, 0, h))

    def out_spec(order):
        return pl.BlockSpec((None, tb, width), lambda i, h, j: (i, order(j), h))

    out = jax.ShapeDtypeStruct((b, t, HGRN_WIDTH), F32)
    state = pltpu.VMEM((hs, HGRN_DIM, HGRN_DIM), F32)
    return pl.pallas_call(
        functools.partial(_hgrn_kernel, n_chunks=tb // CHUNK),
        grid=(b, groups, nb),
        in_specs=[col(0, fwd), col(1, fwd), col(2, fwd), col(0, bwd), col(1, bwd), col(3, bwd),
                  lb_spec(0), lb_spec(1), _resident(masks)],
        out_specs=[out_spec(fwd), out_spec(bwd)],
        out_shape=[out, out],
        scratch_shapes=[state, state],
        compiler_params=_params(3),
        name="hgrn_scan",
    )(p, p, p, p, p, p, lbs, lbs, masks)


def _key_chunk(n_keys):
    return max(c for c in range(LANES, min(KEY_CHUNK, n_keys) + 1, LANES) if n_keys % c == 0)


def _score_steps(q, k_ref, k_cols, n_keys, s_ref, m_ref):
    kc = _key_chunk(n_keys)
    m = None
    for c in range(n_keys // kc):
        s = _dot_nt(q, k_ref[c * kc:(c + 1) * kc, k_cols])
        s_ref[:, c * kc:(c + 1) * kc] = s
        for blk in _lane_blocks(kc):
            m = s[:, blk] if m is None else jnp.maximum(m, s[:, blk])
        yield
    m_ref[...] = jnp.broadcast_to(jnp.max(m, axis=-1, keepdims=True), m.shape)


def _weight_steps(s_ref, m_ref, v_ref, n_keys, out):
    kc = _key_chunk(n_keys)
    m = m_ref[...]
    ones = jnp.ones((kc, LANES), BF16)
    acc = None
    for c in range(n_keys // kc):
        p = jnp.concatenate([jnp.exp(s_ref[:, c * kc + blk.start:c * kc + blk.stop] - m)
                             for blk in _lane_blocks(kc)], axis=1)
        part = _dot(p.astype(BF16), jnp.concatenate([v_ref[c * kc:(c + 1) * kc, :], ones], axis=1))
        acc = part if acc is None else acc + part
        yield
    out.append(acc)


def _interleave(*step_iters):
    live = list(step_iters)
    while live:
        for it in list(live):
            if next(it, StopIteration) is StopIteration:
                live.remove(it)


def _attention_tiles(n_streams, scores, weights, finish, tq, ctx_len, t):
    def weigh(buf, n_keys, row0, *extra):
        accs = []
        for s in range(n_streams):
            out = []
            _interleave(weights(s, buf, n_keys, out), *[e(s) for e in extra])
            accs.append(out[0])
        finish(row0, accs)

    for i in range(ctx_len // tq):
        for s in range(n_streams):
            _interleave(scores(s, i * tq, 0, ctx_len))
        weigh(0, ctx_len, i * tq)
    n_lat = (t - ctx_len) // tq
    for s in range(n_streams):
        _interleave(scores(s, ctx_len, 0, t))

    def next_scores(row0, buf):
        return lambda s: scores(s, row0, buf, t)

    def pair(i, carry):
        row0 = pl.multiple_of(ctx_len + 2 * i * tq, tq)
        weigh(0, t, row0, next_scores(row0 + tq, 1))
        weigh(1, t, row0 + tq, next_scores(row0 + 2 * tq, 0))
        return carry

    n_pairs = (n_lat - 1) // 2
    lax.fori_loop(0, n_pairs, pair, 0)
    row0 = ctx_len + 2 * n_pairs * tq
    if n_lat % 2 == 0:
        weigh(0, t, row0, next_scores(row0 + tq, 1))
        weigh(1, t, row0 + tq)
    else:
        weigh(0, t, row0)


def _tile_rows(row0, tq):
    return pl.ds(row0 if isinstance(row0, int) else pl.multiple_of(row0, tq), tq)


def _attention_scratch(n_streams, tq, t):
    scores = pltpu.VMEM((n_streams, tq, t), F32)
    row_max = pltpu.VMEM((n_streams, tq, LANES), F32)
    return [scores, scores, row_max, row_max]


def _mla_attn_kernel(q_ref, k_ref, v_ref, o_ref, s0_ref, s1_ref, m0_ref, m1_ref, *, tq, ctx_len):
    heads = _lane_blocks(2 * LANES)
    s_bufs, m_bufs = (s0_ref, s1_ref), (m0_ref, m1_ref)

    def scores(h, row0, buf, n_keys):
        return _score_steps(q_ref[_tile_rows(row0, tq), heads[h]], k_ref, heads[h], n_keys,
                            s_bufs[buf].at[h], m_bufs[buf].at[h])

    def weights(h, buf, n_keys, out):
        return _weight_steps(s_bufs[buf].at[h], m_bufs[buf].at[h], v_ref, n_keys, out)

    def finish(row0, accs):
        o = [a[:, :LANES] / a[:, LANES:] for a in accs]
        lane = lax.broadcasted_iota(jnp.int32, o[0].shape, 1)
        o_ref[_tile_rows(row0, tq), :] = jnp.where(lane < MLA_V, o[0], o[1]).astype(BF16)

    _attention_tiles(2, scores, weights, finish, tq, ctx_len, k_ref.shape[0])


def _mla_attention(q, k, v, tq, ctx_len):
    b, t, _ = q.shape
    return pl.pallas_call(
        functools.partial(_mla_attn_kernel, tq=tq, ctx_len=ctx_len),
        grid=(b, MLA_HEADS // 2),
        in_specs=[pl.BlockSpec((None, t, 2 * LANES), lambda i, h: (i, 0, h)),
                  pl.BlockSpec((None, t, 2 * LANES), lambda i, h: (i, 0, h)),
                  pl.BlockSpec((None, t, LANES), lambda i, h: (i, 0, h))],
        out_specs=pl.BlockSpec((None, t, LANES), lambda i, h: (i, 0, h)),
        out_shape=jax.ShapeDtypeStruct((b, t, MLA_HEADS * MLA_V), BF16),
        scratch_shapes=_attention_scratch(2, tq, t),
        compiler_params=_params(2, VMEM_LIMIT),
        name="mla_attention",
    )(q, k, v)


def _ret_chunk(q, k, v, decays, st, head_lanes):
    dmat, q_dec, k_dec, c_dec = decays
    qm = jnp.where(head_lanes, q, 0.0)
    att = _dot_nt(qm.astype(BF16), k.astype(BF16)) * dmat
    o = _dot(att.astype(BF16), v) + _dot_nt((qm * q_dec).astype(BF16), st.astype(BF16))
    st = st * c_dec + _dot_tn(v, (k * k_dec).astype(BF16))
    return o, st


def _ret_decays(lg, reverse):
    c = CHUNK
    row = lax.broadcasted_iota(jnp.int32, (c, LANES), 0)
    tau = ((c - 1 - row) if reverse else row).astype(F32)
    ti = lax.broadcasted_iota(jnp.int32, (c, c), 0)
    si = lax.broadcasted_iota(jnp.int32, (c, c), 1)
    rel = ((si - ti) if reverse else (ti - si)).astype(F32)
    dmat = jnp.where(rel >= 0, jnp.exp(jnp.maximum(rel, 0.0) * lg[:, :c]), 0.0)
    return dmat, jnp.exp((tau + 1.0) * lg), jnp.exp((c - 1.0 - tau) * lg), jnp.exp(c * lg)


def _ret_kernel(qkf_ref, vf_ref, qkb_ref, vb_ref, lg_ref, of_ref, ob_ref, sf_ref, sb_ref, *, n_chunks):
    @pl.when(pl.program_id(1) == 0)
    def _():
        sf_ref[...] = jnp.zeros_like(sf_ref)
        sb_ref[...] = jnp.zeros_like(sb_ref)

    per_block = LANES // RET_QK
    lane = lax.broadcasted_iota(jnp.int32, (CHUNK, LANES), 1)
    dirs = ((qkf_ref, vf_ref, of_ref, sf_ref), (qkb_ref, vb_ref, ob_ref, sb_ref))
    chains = [(d, h) for d in range(2) for h in range(RET_HEADS)]
    decays = {(d, h): _ret_decays(_log_sigmoid(lg_ref[d, h]), d == 1) for d, h in chains}
    states = {(d, h): dirs[d][3][h] for d, h in chains}
    for step in range(n_chunks):
        for d, h in chains:
            qk_ref, v_ref, o_ref, _ = dirs[d]
            ci = step if d == 0 else n_chunks - 1 - step
            rows = slice(ci * CHUNK, (ci + 1) * CHUNK)
            q_cols = _lane_blocks(2 * RET_QK_WIDTH)[h // per_block]
            k_cols = _lane_blocks(2 * RET_QK_WIDTH)[RET_QK_WIDTH // LANES + h // per_block]
            v_cols = _lane_blocks(RET_V_WIDTH)[h]
            o, states[d, h] = _ret_chunk(qk_ref[rows, q_cols], qk_ref[rows, k_cols], v_ref[rows, v_cols],
                                         decays[d, h], states[d, h], (lane // RET_QK) == (h % per_block))
            o_ref[rows, v_cols] = o
    for (d, h), st in states.items():
        dirs[d][3][h] = st


def _ret_scan(qk, vg, decay, tb, n_ctx_blocks):
    b, t, _ = qk.shape
    nb = t // tb
    fwd, bwd = _scan_block_maps(nb, n_ctx_blocks)

    def rows(width, order):
        return pl.BlockSpec((None, tb, width), lambda i, j: (i, order(j), 0))

    out = jax.ShapeDtypeStruct((b, t, RET_V_WIDTH), F32)
    state = pltpu.VMEM((RET_HEADS, RET_V, LANES), F32)
    return pl.pallas_call(
        functools.partial(_ret_kernel, n_chunks=tb // CHUNK),
        grid=(b, nb),
        in_specs=[rows(2 * RET_QK_WIDTH, fwd), rows(RET_V_WIDTH, fwd), rows(2 * RET_QK_WIDTH, bwd),
                  rows(RET_V_WIDTH, bwd), _resident(decay)],
        out_specs=[rows(RET_V_WIDTH, fwd), rows(RET_V_WIDTH, bwd)],
        out_shape=[out, out],
        scratch_shapes=[state, state],
        compiler_params=_params(2),
        name="retention_scan",
    )(qk, vg, qk, vg, decay)


def _diff_attn_kernel(q_ref, k_ref, v_ref, lam_ref, g_ref, o_ref, s0_ref, s1_ref, m0_ref, m1_ref,
                      *, tq, ctx_len, lambda_init):
    lp = lam_ref[...]
    lam = (jnp.exp(jnp.sum(lp[0:1] * lp[1:2], axis=-1, keepdims=True))
           - jnp.exp(jnp.sum(lp[2:3] * lp[3:4], axis=-1, keepdims=True)) + lambda_init)

    s_bufs, m_bufs = (s0_ref, s1_ref), (m0_ref, m1_ref)
    all_lanes = slice(0, LANES)

    def scores(i, row0, buf, n_keys):
        q = q_ref[_tile_rows(row0, tq), :]
        first = lax.broadcasted_iota(jnp.int32, q.shape, 1) < DIFF_QK
        q = jnp.where(first == (i == 0), q, jnp.zeros_like(q))
        return _score_steps(q, k_ref, all_lanes, n_keys, s_bufs[buf].at[i], m_bufs[buf].at[i])

    def weights(i, buf, n_keys, out):
        return _weight_steps(s_bufs[buf].at[i], m_bufs[buf].at[i], v_ref, n_keys, out)

    def finish(row0, accs):
        o = accs[0][:, :LANES] / accs[0][:, LANES:] - lam * (accs[1][:, :LANES] / accs[1][:, LANES:])
        o_ref[_tile_rows(row0, tq), :] = (_rms(o) * g_ref[...] * (1.0 - lambda_init)).astype(BF16)

    _attention_tiles(2, scores, weights, finish, tq, ctx_len, k_ref.shape[0])


def _diff_attention(q, k, v, lam_params, subln_g, tq, ctx_len, lambda_init):
    b, t, _ = q.shape
    head = pl.BlockSpec((None, t, LANES), lambda i, h: (i, 0, h))
    return pl.pallas_call(
        functools.partial(_diff_attn_kernel, tq=tq, ctx_len=ctx_len, lambda_init=lambda_init),
        grid=(b, DIFF_HEADS),
        in_specs=[head, head, head, _resident(lam_params), _resident(subln_g)],
        out_specs=head,
        out_shape=jax.ShapeDtypeStruct((b, t, DIFF_V_WIDTH), BF16),
        scratch_shapes=_attention_scratch(2, tq, t),
        compiler_params=_params(2, VMEM_LIMIT),
        name="diff_attention",
    )(q, k, v, lam_params, subln_g)


def _mix_ffn_kernel(*refs, n_stream, n_ctx_tiles, row_off, final):
    (of_ref, ob_ref, gate_ref, mix_ref, ng_ref, mod_ref, g_ref, wo_ref, wg_ref, wu_ref, wd_ref) = refs[n_stream:n_stream + 11]
    rest = refs[n_stream + 11:]
    o_ref = rest[-1]
    x_in = _stream_tile(refs[:n_stream], n_ctx_tiles, row_off)
    o = of_ref[...] + ob_ref[...]
    gate = gate_ref[...].astype(F32)
    a = jnp.concatenate([(_rms(o[:, cols]) * ng_ref[...] * _silu(gate[:, cols])).astype(BF16)
                         for cols in _lane_blocks(o.shape[1])], axis=1)
    wa = a.shape[1]
    u = _dot(a, wo_ref[0:wa, :]) + _dot(mix_ref[...], wo_ref[wa:, :])
    x = x_in + mod_ref[2:3, :] * u
    h = (_rms(x) * g_ref[1:2, :] * (1.0 + mod_ref[4:5, :]) + mod_ref[3:4, :]).astype(BF16)
    act = (_silu(_dot(h, wg_ref[...])) * _dot(h, wu_ref[...])).astype(BF16)
    x = x + mod_ref[5:6, :] * _dot(act, wd_ref[...])
    o_ref[...] = _rms(x) * rest[0][...] if final else x


def _mix_ffn(stream, o_f, o_b, gates, gate_block, mix, norm_gain, mod, norm_g, w_out, wg, wu, wd, layer, tm,
             n_ctx_tiles, final_g=None):
    b, t, wa = o_f.shape
    d = w_out.shape[2]
    final = final_g is not None
    off = n_ctx_tiles if final else 0
    n_rows = t - off * tm
    stream_specs, stream_args = _stream_specs(stream, tm, n_ctx_tiles, off)
    in_specs = stream_specs + [
        _rows(tm, wa, off), _rows(tm, wa, off), pl.BlockSpec((None, tm, wa), lambda i, j: (i, j + off, gate_block)),
        _rows(tm, mix.shape[2], off), _resident(norm_gain), _mod_spec(layer, b, n_ctx_tiles, d, off),
        _norm_spec(layer, d), _resident(w_out, layer), _resident(wg, layer), _resident(wu, layer),
        _resident(wd, layer)]
    args = stream_args + [o_f, o_b, gates, mix, norm_gain, mod, norm_g, w_out, wg, wu, wd]
    if final:
        in_specs.append(_resident(final_g))
        args.append(final_g)
    return pl.pallas_call(
        functools.partial(_mix_ffn_kernel, n_stream=len(stream_args), n_ctx_tiles=n_ctx_tiles, row_off=off,
                          final=final),
        grid=(b, n_rows // tm),
        in_specs=in_specs,
        out_specs=_rows(tm, d),
        out_shape=jax.ShapeDtypeStruct((b, n_rows, d), F32),
        compiler_params=_params(2, VMEM_LIMIT),
        name="mix_ffn",
    )(*args)


def _rope_tables(ctx_len, seq, kind, scale=1.0):
    f32 = np.float32
    lane = np.arange(LANES)
    n = np.arange(seq)
    if kind == "ret":
        half = RET_QK // 2
        idx = lane % RET_QK
        active = np.ones(LANES, bool)
        pos = np.broadcast_to(n[:, None], (seq, LANES))
    else:
        width = DIFF_QK if kind == "diff" else MLA_ROPE
        half = width // 4
        if kind == "diff":
            idx = lane % (width // 2)
            by_col = (lane % width) >= width // 2
            active = np.ones(LANES, bool)
        else:
            idx = (lane - MLA_NOPE) % (width // 2)
            by_col = (lane - MLA_NOPE) >= width // 2
            active = (lane >= MLA_NOPE) & (lane < MLA_NOPE + width)
        pos = np.where(by_col[None, :], (n % GRID_W)[:, None], (n // GRID_W)[:, None])
    upper = (idx >= half)[None, :]
    inv = f32(ROPE_BASE) ** (-(idx % half).astype(f32) / f32(half))
    ang = pos.astype(f32) * inv[None, :].astype(f32)
    cos = np.where(active[None, :], np.cos(ang), f32(1.0)).astype(f32)
    sin = np.where(active[None, :], np.sin(ang), f32(0.0)).astype(f32)
    lat = np.stack([cos, np.where(upper, sin, f32(0.0)), np.where(upper, f32(0.0), -sin)])
    ident = np.stack([np.ones((ctx_len, LANES), f32), np.zeros((ctx_len, LANES), f32),
                      np.zeros((ctx_len, LANES), f32)])
    return jnp.asarray(np.concatenate([ident, lat], axis=1) * f32(scale), F32)


def kernel(x, c, ctx, c_ctx, ada_w, ada_b, norm_g, mix_w_out, ffn_w_gate, ffn_w_up, ffn_w_down,
           even_w_in, hgrn_lb_logits, hgrn_norm_g, mla_q_norm_g, mla_w_uq, mla_kv_norm_g, mla_w_ukv,
           odd_w_in, ret_decay_logits, diff_lambda, diff_subln_g, final_norm_g):
    batch, seq, d = x.shape
    ctx_len = ctx.shape[1]
    depth = ada_w.shape[0]
    tm = math.gcd(ROW_TILE, math.gcd(ctx_len, seq))
    n_ctx_tiles = ctx_len // tm
    assert tm % (2 * CHUNK) == 0 and seq % GRID_W == 0

    mod_rows = -(-(batch + 1) // 8) * 8
    s_rows = jnp.concatenate([c, c_ctx[None, :], jnp.zeros((mod_rows - batch - 1, d), F32)], axis=0)
    mod = _modulation(s_rows, ada_w, ada_b).reshape(depth, mod_rows, 6, d)

    lb_soft = jax.nn.softmax(hgrn_lb_logits.astype(F32), axis=0)
    lbs = jnp.cumsum(lb_soft, axis=0) - lb_soft[0:1]

    tab_mla_q = _rope_tables(ctx_len, seq, "mla", (MLA_NOPE + MLA_ROPE) ** -0.5)
    tab_mla_k = _rope_tables(ctx_len, seq, "mla")
    tab_ret = _rope_tables(ctx_len, seq, "ret")
    tab_diff = _rope_tables(ctx_len, seq, "diff")

    stream = (ctx, x)
    even_w = _even_weights(even_w_in, tm)
    odd_w, w_out, w_gate, w_up, w_down = (a.astype(BF16) for a in (odd_w_in, mix_w_out, ffn_w_gate, ffn_w_up,
                                                                    ffn_w_down))
    for layer in range(depth):
        if layer % 2 == 0:
            e = layer // 2
            wq = mla_w_uq[e].reshape(MLA_Q_RANK, MLA_HEADS, MLA_NOPE + MLA_ROPE)
            wq = jnp.pad(wq, ((0, 0), (0, 0), (0, LANES - MLA_NOPE - MLA_ROPE))).reshape(MLA_Q_RANK, -1)
            wkv = mla_w_ukv[e].reshape(MLA_KV_RANK, MLA_HEADS, MLA_NOPE + MLA_V)
            wk = jnp.pad(wkv[:, :, :MLA_NOPE], ((0, 0), (0, 0), (0, LANES - MLA_NOPE))).reshape(MLA_KV_RANK, -1)
            wv = wkv[:, :, MLA_NOPE:].reshape(MLA_KV_RANK, -1)
            gates, q, k, v = _even_in_projection(
                stream, ctx_len + seq, mod, norm_g, even_w, e, mla_q_norm_g[e][None, :], mla_kv_norm_g[e][None, :],
                wq.astype(BF16),
                jnp.concatenate([wk, wv], axis=1).astype(BF16), tab_mla_q, tab_mla_k, layer, tm, n_ctx_tiles)
            o_f, o_b = _hgrn_scan(gates, lbs[e].reshape(2, 1, HGRN_WIDTH), tm, n_ctx_tiles)
            mix = _mla_attention(q, k, v, tm, ctx_len)
            gate_block, gain = 4, hgrn_norm_g[e][None, :]
        else:
            o = layer // 2
            qk_ret, gates, q_d, k_d, v_d = _odd_in_projection(stream, mod, norm_g, odd_w, o, tab_ret, tab_diff, layer,
                                                              tm, n_ctx_tiles)
            decay = jnp.broadcast_to(ret_decay_logits[o][:, :, None, None], (2, RET_HEADS, 1, LANES))
            o_f, o_b = _ret_scan(qk_ret, gates, decay, tm, n_ctx_tiles)
            lambda_init = 0.8 - 0.6 * math.exp(-0.3 * layer)
            mix = _diff_attention(q_d, k_d, v_d, diff_lambda[o], diff_subln_g[o][None, :], tm, ctx_len, lambda_init)
            gate_block, gain = 1, jnp.ones((1, LANES), F32)
        stream = _mix_ffn(stream, o_f, o_b, gates, gate_block, mix, gain, mod, norm_g, w_out, w_gate, w_up, w_down,
                          layer, tm, n_ctx_tiles, final_norm_g[None, :] if layer == depth - 1 else None)
    return stream
```

```python
import functools
import math

import numpy as np
import jax
import jax.numpy as jnp
from jax import lax
from jax.experimental import pallas as pl
from jax.experimental.pallas import tpu as pltpu

F32 = jnp.float32
BF16 = jnp.bfloat16

LANES = 128
GRID_W = 64
CHUNK = 64
ROPE_BASE = 10000.0
EPS = 1e-6

HGRN_HEADS, HGRN_DIM = 4, 128
HGRN_WIDTH = HGRN_HEADS * HGRN_DIM
MLA_HEADS, MLA_Q_RANK, MLA_KV_RANK, MLA_NOPE, MLA_ROPE, MLA_V = 8, 384, 256, 64, 32, 64
RET_HEADS, RET_QK, RET_V = 4, 64, 128
DIFF_HEADS, DIFF_QK, DIFF_V = 4, 64, 128
RET_QK_WIDTH = RET_HEADS * RET_QK
RET_V_WIDTH = RET_HEADS * RET_V
DIFF_QK_WIDTH = DIFF_HEADS * 2 * DIFF_QK
DIFF_V_WIDTH = DIFF_HEADS * DIFF_V

EV_HGRN = 5 * HGRN_WIDTH
EV_CQ = 0
EV_CKV = EV_CQ + MLA_Q_RANK
EV_KROPE = EV_CKV + MLA_KV_RANK
EV_HGRN0 = EV_KROPE + LANES
OD_KR = RET_QK_WIDTH
OD_VR = OD_KR + RET_QK_WIDTH
OD_QD = OD_VR + 2 * RET_V_WIDTH
OD_KD = OD_QD + DIFF_QK_WIDTH
OD_VD = OD_KD + DIFF_QK_WIDTH

ROW_TILE = 256
KEY_CHUNK = 768
VMEM_LIMIT = 48 * 1024 * 1024
MIX_VMEM_LIMIT = 56 * 1024 * 1024
MIX_GROUP = 2
MXU_WIDTH = 256


def _params(n_axes, vmem=None):
    return pltpu.CompilerParams(dimension_semantics=("arbitrary",) * n_axes, vmem_limit_bytes=vmem)


def _resident(a, index=None):
    if index is None:
        return pl.BlockSpec(a.shape, lambda *_: (0,) * a.ndim, pipeline_mode=pl.Buffered(1))
    return pl.BlockSpec((None,) + a.shape[1:], lambda *_: (index,) + (0,) * (a.ndim - 1),
                        pipeline_mode=pl.Buffered(1))


def _dot(a, b):
    return jnp.dot(a, b, preferred_element_type=F32)


def _dot_nt(a, b):
    return lax.dot_general(a, b, (((1,), (1,)), ((), ())), preferred_element_type=F32)


def _dot_tn(a, b):
    return lax.dot_general(a, b, (((0,), (0,)), ((), ())), preferred_element_type=F32)


def _rms(x):
    return x * lax.rsqrt(jnp.mean(x * x, axis=-1, keepdims=True) + EPS)


def _silu(x):
    return x * jax.nn.sigmoid(x)


def _log_sigmoid(z):
    return jnp.minimum(z, 0.0) - jnp.log1p(jnp.exp(-jnp.abs(z)))


def _rope(x, tab_ref, shift):
    return x * tab_ref[0] + pltpu.roll(x, shift, 1) * tab_ref[1] + pltpu.roll(x, LANES - shift, 1) * tab_ref[2]


def _lane_blocks(width):
    return [slice(i * LANES, (i + 1) * LANES) for i in range(width // LANES)]


def _mod_kernel(s_ref, w_ref, b_ref, o_ref):
    s = _silu(s_ref[...])
    rows = s.shape[0]
    s_hi = s.astype(BF16)
    rest = s - s_hi.astype(F32)
    s_mid = rest.astype(BF16)
    pieces = jnp.concatenate([s_hi, s_mid, (rest - s_mid.astype(F32)).astype(BF16)], axis=0)
    w = w_ref[...]
    w_hi = w.astype(BF16)
    acc = _dot(pieces, w_hi) + _dot(pieces, (w - w_hi.astype(F32)).astype(BF16))
    o_ref[...] = acc[:rows] + acc[rows:2 * rows] + acc[2 * rows:] + b_ref[...]


def _modulation(s_rows, ada_w, ada_b):
    depth, d, d6 = ada_w.shape
    rows = s_rows.shape[0]
    tn = d
    return pl.pallas_call(
        _mod_kernel,
        grid=(depth, d6 // tn),
        in_specs=[pl.BlockSpec((rows, d), lambda l, n: (0, 0)),
                  pl.BlockSpec((None, d, tn), lambda l, n: (l, 0, n)),
                  pl.BlockSpec((None, 1, tn), lambda l, n: (l, 0, n))],
        out_specs=pl.BlockSpec((None, rows, tn), lambda l, n: (l, 0, n)),
        out_shape=jax.ShapeDtypeStruct((depth, rows, d6), F32),
        compiler_params=_params(2),
        name="modulation",
    )(s_rows, ada_w, ada_b.reshape(depth, 1, d6))


def _mod_spec(layer, batch, n_ctx_tiles, d, row_off=0, group=None):
    return pl.BlockSpec((None, group, 6, d),
                        lambda b, t: (layer, jnp.where(t + row_off < n_ctx_tiles, batch // (group or 1), b), 0, 0))


def _norm_spec(layer, d):
    return pl.BlockSpec((None, 2, d), lambda i, j: (layer, 0, 0))


def _rows(tm, width, row_off=0, group=None, col_block=0):
    return pl.BlockSpec((group, tm, width), lambda i, j: (i, j + row_off, col_block))


def _stream_specs(stream, tm, n_ctx_tiles, row_off=0, group=None):
    if not isinstance(stream, tuple):
        return [_rows(tm, stream.shape[2], row_off, group)], [stream]
    d = stream[0].shape[2]
    return ([pl.BlockSpec((group, tm, d), lambda i, j: (i, jnp.minimum(j + row_off, n_ctx_tiles - 1), 0)),
             pl.BlockSpec((group, tm, d), lambda i, j: (i, jnp.maximum(j + row_off - n_ctx_tiles, 0), 0))],
            list(stream))


def _stream_tile(refs, n_ctx_tiles, row_off=0):
    if len(refs) == 1:
        return refs[0][...]
    return jnp.where(pl.program_id(1) + row_off < n_ctx_tiles, refs[0][...], refs[1][...])


def _even_in_kernel(*refs, n_ctx_tiles):
    (mod_ref, g_ref, w_ref, gq_ref, gkv_ref, wq_ref, wkv_ref, tq_ref, tk_ref, hg_ref, q_ref, k_ref, v_ref) = refs[-13:]
    shift = MLA_ROPE // 4
    n_samples, tm, _ = hg_ref.shape
    x = _stream_tile(refs[:-13], n_ctx_tiles)
    h = jnp.concatenate([(_rms(x[s]) * g_ref[0:1, :] * (1.0 + mod_ref[s, 1:2, :]) + mod_ref[s, 0:1, :])
                         .astype(BF16) for s in range(n_samples)], axis=0)
    p = _dot(h, w_ref[:, :EV_HGRN0])
    hg = _dot(h, w_ref[:, EV_HGRN0:]).astype(BF16)
    q_all = _dot((_rms(p[:, EV_CQ:EV_CKV]) * gq_ref[...]).astype(BF16), wq_ref[...])
    kv_all = _dot((_rms(p[:, EV_CKV:EV_KROPE]) * gkv_ref[...]).astype(BF16), wkv_ref[...])
    for s in range(n_samples):
        rows = slice(s * tm, (s + 1) * tm)
        hg_ref[s] = hg[rows]
        q, kv = q_all[rows], kv_all[rows]
        k_rope = _rope(pltpu.roll(p[rows, EV_KROPE:], MLA_NOPE, 1), tk_ref, shift)
        for cols in _lane_blocks(MLA_HEADS * LANES):
            q_ref[s, :, cols] = _rope(q[:, cols], tq_ref, shift).astype(BF16)
            k_ref[s, :, cols] = (kv[:, cols] + k_rope).astype(BF16)
        v_ref[s] = kv[:, MLA_HEADS * LANES:].astype(BF16)


def _even_in_projection(stream, t, mod, norm_g, w, w_index, gq, gkv, wq, wkv, tab_q, tab_k, layer, tm, n_ctx_tiles):
    d = w.shape[1]
    b = stream[0].shape[0] if isinstance(stream, tuple) else stream.shape[0]
    hq = MLA_HEADS * LANES
    tab = pl.BlockSpec((3, tm, LANES), lambda i, j: (0, j, 0))
    shapes = [(EV_HGRN, BF16), (hq, BF16), (hq, BF16), (MLA_HEADS * MLA_V, BF16)]
    g = MIX_GROUP if b % MIX_GROUP == 0 else 1
    stream_specs, stream_args = _stream_specs(stream, tm, n_ctx_tiles, group=g)
    return pl.pallas_call(
        functools.partial(_even_in_kernel, n_ctx_tiles=n_ctx_tiles),
        grid=(b // g, t // tm),
        in_specs=stream_specs + [_mod_spec(layer, b, n_ctx_tiles, d, group=g), _norm_spec(layer, d),
                                 _resident(w, w_index), _resident(gq), _resident(gkv), _resident(wq),
                                 _resident(wkv), tab, tab],
        out_specs=[_rows(tm, width, group=g) for width, _ in shapes],
        out_shape=[jax.ShapeDtypeStruct((b, t, width), dt) for width, dt in shapes],
        compiler_params=_params(2, VMEM_LIMIT),
        name="even_in_projection",
    )(*stream_args, mod, norm_g, w, gq, gkv, wq, wkv, tab_q, tab_k)


def _even_w_kernel(w_ref, o_ref):
    n_low = w_ref.shape[1] - EV_HGRN
    o_ref[:, :n_low] = w_ref[:, EV_HGRN:].astype(BF16)
    o_ref[:, n_low:EV_HGRN0] = jnp.zeros((o_ref.shape[0], EV_HGRN0 - n_low), BF16)
    o_ref[:, EV_HGRN0:EV_HGRN0 + HGRN_WIDTH] = (w_ref[:, :HGRN_WIDTH] * (HGRN_DIM ** -0.5)).astype(BF16)
    o_ref[:, EV_HGRN0 + HGRN_WIDTH:] = w_ref[:, HGRN_WIDTH:EV_HGRN].astype(BF16)


def _even_weights(even_w_in, tm):
    n, d, cols = even_w_in.shape
    return pl.pallas_call(
        _even_w_kernel,
        grid=(n, d // tm),
        in_specs=[pl.BlockSpec((None, tm, cols), lambda e, r: (e, r, 0))],
        out_specs=pl.BlockSpec((None, tm, EV_HGRN0 + EV_HGRN), lambda e, r: (e, r, 0)),
        out_shape=jax.ShapeDtypeStruct((n, d, EV_HGRN0 + EV_HGRN), BF16),
        compiler_params=_params(2),
        name="even_weights",
    )(even_w_in)


def _odd_in_kernel(x_ref, mod_ref, g_ref, w_ref, tr_ref, td_ref, qkr_ref, vg_ref, qd_ref, kd_ref, vd_ref):
    n_samples, tm, _ = x_ref.shape
    h = jnp.concatenate([(_rms(x_ref[s]) * g_ref[0:1, :] * (1.0 + mod_ref[s, 1:2, :]) + mod_ref[s, 0:1, :])
                         .astype(BF16) for s in range(n_samples)], axis=0)
    p_all = _dot(h, w_ref[...])
    for s in range(n_samples):
        p = p_all[s * tm:(s + 1) * tm]
        for cols in _lane_blocks(RET_QK_WIDTH):
            qkr_ref[s, :, cols] = _rope(p[:, cols], tr_ref, RET_QK // 2)
            kc = slice(cols.start + OD_KR, cols.stop + OD_KR)
            qkr_ref[s, :, kc] = _rope(p[:, kc], tr_ref, RET_QK // 2) * (RET_QK ** -0.5)
        vg_ref[s] = p[:, OD_VR:OD_QD].astype(BF16)
        for cols in _lane_blocks(DIFF_QK_WIDTH):
            qc = slice(cols.start + OD_QD, cols.stop + OD_QD)
            kc = slice(cols.start + OD_KD, cols.stop + OD_KD)
            qd_ref[s, :, cols] = (_rope(p[:, qc], td_ref, DIFF_QK // 4) * (DIFF_QK ** -0.5)).astype(BF16)
            kd_ref[s, :, cols] = _rope(p[:, kc], td_ref, DIFF_QK // 4).astype(BF16)
        vd_ref[s] = p[:, OD_VD:].astype(BF16)


def _odd_in_projection(x, mod, norm_g, w, w_index, tab_ret, tab_diff, layer, tm, n_ctx_tiles):
    b, t, d = x.shape
    tab = pl.BlockSpec((3, tm, LANES), lambda i, j: (0, j, 0))
    shapes = [(2 * RET_QK_WIDTH, F32), (2 * RET_V_WIDTH, BF16), (DIFF_QK_WIDTH, BF16), (DIFF_QK_WIDTH, BF16),
              (DIFF_V_WIDTH, BF16)]
    g = MIX_GROUP if b % MIX_GROUP == 0 else 1
    return pl.pallas_call(
        _odd_in_kernel,
        grid=(b // g, t // tm),
        in_specs=[_rows(tm, d, group=g), _mod_spec(layer, b, n_ctx_tiles, d, group=g), _norm_spec(layer, d),
                  _resident(w, w_index), tab, tab],
        out_specs=[_rows(tm, width, group=g) for width, _ in shapes],
        out_shape=[jax.ShapeDtypeStruct((b, t, width), dt) for width, dt in shapes],
        compiler_params=_params(2, VMEM_LIMIT),
        name="odd_in_projection",
    )(x, mod, norm_g, w, tab_ret, tab_diff)


_LEVELS = 6
HGRN_STEP_HEADS = 4


def _gla_masks():
    c = CHUNK
    masks = np.zeros((_LEVELS + 1, c, c), np.float32)
    masks[0] = np.eye(c)
    for lv in range(_LEVELS):
        b = 32 >> lv
        for tau in range(c):
            mid = (tau // (2 * b)) * 2 * b + b
            if tau >= mid:
                masks[lv + 1, tau, mid - b:mid] = 1.0
    mk = np.stack([masks, masks[:, ::-1, ::-1]]).reshape(2, (_LEVELS + 1) * c, c)
    return jnp.asarray(mk, F32)


def _gla_gates(z, lb):
    k = (1.0 - lb) * jax.nn.sigmoid(-z)
    return k, (_log_sigmoid(z) + jnp.log1p(lb * jnp.exp(-z))) * math.log2(math.e)


SUBLANES = 8


def _scan_positions(reverse):
    sub = lax.broadcasted_iota(jnp.int32, (SUBLANES, LANES), 0)
    return (SUBLANES - 1 - sub) if reverse else sub


def _gla_tile_consts(reverse):
    pos = _scan_positions(reverse)
    consts = {("from", s): (pos >= s).astype(F32) for s in (1, 2, 4)}
    consts["right", 1] = (pos % 2 == 1).astype(F32)
    for b in (2, 4):
        consts["sign", b] = jnp.where(pos % (2 * b) >= b, 1.0, -1.0)
    return consts


def _gla_decays(g2, reverse, consts):
    n_tiles = CHUNK // SUBLANES
    pos = _scan_positions(reverse)
    tiles = [g2[v * SUBLANES:(v + 1) * SUBLANES] for v in range(n_tiles)]

    def from_earlier(x, s):
        return pltpu.roll(x, (SUBLANES - s) if reverse else s, 0)

    def at(x, p):
        j = SUBLANES - 1 - p if reverse else p
        return jnp.broadcast_to(x[j:j + 1], x.shape)

    scan = []
    for x in tiles:
        for s in (1, 2, 4):
            x = x + from_earlier(x, s) * consts["from", s]
        scan.append(x)
    order = list(reversed(range(n_tiles))) if reverse else list(range(n_tiles))
    prefix, carry = {}, None
    for v in order:
        prefix[v] = scan[v] if carry is None else scan[v] + carry
        carry = at(prefix[v], SUBLANES - 1)
    total = carry
    p_tiles = [prefix[v] for v in order]

    levels = []
    for lv in range(_LEVELS):
        b = 32 >> lv
        out = []
        for n, p in enumerate(p_tiles):
            if b >= SUBLANES:
                span = 2 * b // SUBLANES
                left_last = (n // span) * span + span // 2 - 1
                edge = at(p_tiles[left_last], SUBLANES - 1)
                out.append(p - edge if n % span >= span // 2 else edge - p)
            elif b == 1:
                out.append(tiles[order[n]] * consts["right", 1])
            else:
                edge = at(p, b - 1)
                for first in range(2 * b, SUBLANES, 2 * b):
                    edge = jnp.where(pos >= first, at(p, first + b - 1), edge)
                out.append((p - edge) * consts["sign", b])
        levels.append(out)

    def rows(tile_list):
        return jnp.concatenate(list(reversed(tile_list)) if reverse else tile_list, axis=0)

    dec = [jnp.exp2(rows(t)) for t in levels]
    return dec, jnp.exp2(rows(p_tiles)), jnp.exp2(rows([total - p for p in p_tiles]))


def _sibling_select(qs, k, b, reverse):
    c = CHUNK
    if b % 8 == 0:
        right = lambda r0: (((c - b - r0) if reverse else r0) // b) % 2 == 1
        return jnp.concatenate([(qs if right(r0) else k)[r0:r0 + b] for r0 in range(0, c, b)], axis=0)
    row = lax.broadcasted_iota(jnp.int32, (c, LANES), 0)
    tau = (c - 1 - row) if reverse else row
    return jnp.where((tau & b) != 0, qs, k)


def _gla_chunk(qs, v, k, g2, masks, st, reverse, consts):
    c = CHUNK
    tile_rows = range(0, c, SUBLANES)
    levels, dec_q, dec_k = _gla_decays(g2, reverse, consts)
    att = masks[0:c] * _dot_nt(qs.astype(BF16), k.astype(BF16))
    att = [att[r0:r0 + SUBLANES] for r0 in tile_rows]
    for lv in range(_LEVELS):
        b = 32 >> lv
        zl = (_sibling_select(qs, k, b, reverse) * levels[lv]).astype(BF16)
        term = _dot_nt(zl, zl)
        for n, r0 in enumerate(tile_rows):
            scan_row = (c - SUBLANES - r0) if reverse else r0
            if b < SUBLANES or (scan_row // b) % 2 == 1:
                att[n] = att[n] + masks[(lv + 1) * c + r0:(lv + 1) * c + r0 + SUBLANES] * term[r0:r0 + SUBLANES]
    att = jnp.concatenate(att, axis=0)
    vb = v.astype(BF16)
    o = _dot(att.astype(BF16), vb) + _dot_nt((qs * dec_q).astype(BF16), st.astype(BF16))
    last = 0 if reverse else c - 1
    st = st * dec_q[last:last + 1, :] + _dot_tn(vb, (k * dec_k).astype(BF16))
    return o, st


def _hgrn_kernel(qf_ref, if_ref, zf_ref, qb_ref, ib_ref, zb_ref, lbf_ref, lbb_ref, mk_ref,
                 of_ref, ob_ref, sf_ref, sb_ref, *, n_chunks):
    @pl.when(pl.program_id(2) == 0)
    def _():
        sf_ref[...] = jnp.zeros_like(sf_ref)
        sb_ref[...] = jnp.zeros_like(sb_ref)

    dirs = ((qf_ref, if_ref, zf_ref, lbf_ref, of_ref, sf_ref), (qb_ref, ib_ref, zb_ref, lbb_ref, ob_ref, sb_ref))
    n_heads = sf_ref.shape[0]
    states = {(d, h): dirs[d][5][h] for d in range(2) for h in range(n_heads)}
    consts = [_gla_tile_consts(d == 1) for d in range(2)]
    for step in range(n_chunks):
        for d, (q_ref, i_ref, z_ref, lb_ref, o_ref, _) in enumerate(dirs):
            ci = step if d == 0 else n_chunks - 1 - step
            rows = slice(ci * CHUNK, (ci + 1) * CHUNK)
            for h, cols in enumerate(_lane_blocks(n_heads * LANES)):
                k, g2 = _gla_gates(z_ref[rows, cols].astype(F32), lb_ref[:, cols])
                o, states[d, h] = _gla_chunk(q_ref[rows, cols].astype(F32), i_ref[rows, cols].astype(F32), k, g2,
                                             mk_ref[d], states[d, h], d == 1, consts[d])
                o_ref[rows, cols] = o
    for (d, h), st in states.items():
        dirs[d][5][h] = st


def _scan_block_maps(n_blocks, n_ctx_blocks):
    def bwd(j):
        return jnp.where(j < n_ctx_blocks, n_ctx_blocks - 1 - j, n_blocks - 1 - (j - n_ctx_blocks))
    return (lambda j: j), bwd


def _hgrn_scan(p, lbs, tb, n_ctx_blocks):
    b, t, _ = p.shape
    nb = t // tb
    fwd, bwd = _scan_block_maps(nb, n_ctx_blocks)
    masks = _gla_masks()

    hs = HGRN_STEP_HEADS
    width = hs * LANES
    groups = HGRN_HEADS // hs

    def col(group, order):
        return pl.BlockSpec((None, tb, width), lambda i, h, j: (i, order(j), group * groups + h))

    def lb_spec(d):
        return pl.BlockSpec((None, 1, width), lambda i, h, j: (d, 0, h))

    def out_spec(order):
        return pl.BlockSpec((None, tb, width), lambda i, h, j: (i, order(j), h))

    out = jax.ShapeDtypeStruct((b, t, HGRN_WIDTH), F32)
    state = pltpu.VMEM((hs, HGRN_DIM, HGRN_DIM), F32)
    return pl.pallas_call(
        functools.partial(_hgrn_kernel, n_chunks=tb // CHUNK),
        grid=(b, groups, nb),
        in_specs=[col(0, fwd), col(1, fwd), col(2, fwd), col(0, bwd), col(1, bwd), col(3, bwd),
                  lb_spec(0), lb_spec(1), _resident(masks)],
        out_specs=[out_spec(fwd), out_spec(bwd)],
        out_shape=[out, out],
        scratch_shapes=[state, state],
        compiler_params=_params(3),
        name="hgrn_scan",
    )(p, p, p, p, p, p, lbs, lbs, masks)


def _key_chunk(n_keys):
    return max(c for c in range(LANES, min(KEY_CHUNK, n_keys) + 1, LANES) if n_keys % c == 0)


def _score_steps(q, k_ref, k_cols, n_keys, s_ref, m_ref):
    kc = _key_chunk(n_keys)
    m = None
    for c in range(n_keys // kc):
        s = _dot_nt(q, k_ref[c * kc:(c + 1) * kc, k_cols])
        s_ref[:, c * kc:(c + 1) * kc] = s
        for blk in _lane_blocks(kc):
            m = s[:, blk] if m is None else jnp.maximum(m, s[:, blk])
        yield
    m_ref[...] = jnp.broadcast_to(jnp.max(m, axis=-1, keepdims=True), m.shape)


def _weight_steps(s_ref, m_ref, v_ref, n_keys, out):
    kc = _key_chunk(n_keys)
    m = m_ref[...]
    ones = jnp.ones((kc, LANES), BF16)
    acc = None
    for c in range(n_keys // kc):
        p = jnp.concatenate([jnp.exp(s_ref[:, c * kc + blk.start:c * kc + blk.stop] - m)
                             for blk in _lane_blocks(kc)], axis=1)
        part = _dot(p.astype(BF16), jnp.concatenate([v_ref[c * kc:(c + 1) * kc, :], ones], axis=1))
        acc = part if acc is None else acc + part
        yield
    out.append(acc)


def _interleave(*step_iters):
    live = list(step_iters)
    while live:
        for it in list(live):
            if next(it, StopIteration) is StopIteration:
                live.remove(it)


def _attention_tiles(n_streams, scores, weights, finish, tq, ctx_len, t):
    def weigh(buf, n_keys, row0, *extra):
        accs = []
        for s in range(n_streams):
            out = []
            _interleave(weights(s, buf, n_keys, out), *[e(s) for e in extra])
            accs.append(out[0])
        finish(row0, accs)

    for i in range(ctx_len // tq):
        for s in range(n_streams):
            _interleave(scores(s, i * tq, 0, ctx_len))
        weigh(0, ctx_len, i * tq)
    n_lat = (t - ctx_len) // tq
    for s in range(n_streams):
        _interleave(scores(s, ctx_len, 0, t))

    def next_scores(row0, buf):
        return lambda s: scores(s, row0, buf, t)

    def pair(i, carry):
        row0 = pl.multiple_of(ctx_len + 2 * i * tq, tq)
        weigh(0, t, row0, next_scores(row0 + tq, 1))
        weigh(1, t, row0 + tq, next_scores(row0 + 2 * tq, 0))
        return carry

    n_pairs = (n_lat - 1) // 2
    lax.fori_loop(0, n_pairs, pair, 0)
    row0 = ctx_len + 2 * n_pairs * tq
    if n_lat % 2 == 0:
        weigh(0, t, row0, next_scores(row0 + tq, 1))
        weigh(1, t, row0 + tq)
    else:
        weigh(0, t, row0)


def _tile_rows(row0, tq):
    return pl.ds(row0 if isinstance(row0, int) else pl.multiple_of(row0, tq), tq)


def _attention_scratch(n_streams, tq, t):
    scores = pltpu.VMEM((n_streams, tq, t), F32)
    row_max = pltpu.VMEM((n_streams, tq, LANES), F32)
    return [scores, scores, row_max, row_max]


def _mla_attn_kernel(q_ref, k_ref, v_ref, o_ref, s0_ref, s1_ref, m0_ref, m1_ref, *, tq, ctx_len):
    heads = _lane_blocks(2 * LANES)
    s_bufs, m_bufs = (s0_ref, s1_ref), (m0_ref, m1_ref)

    def scores(h, row0, buf, n_keys):
        return _score_steps(q_ref[_tile_rows(row0, tq), heads[h]], k_ref, heads[h], n_keys,
                            s_bufs[buf].at[h], m_bufs[buf].at[h])

    def weights(h, buf, n_keys, out):
        return _weight_steps(s_bufs[buf].at[h], m_bufs[buf].at[h], v_ref, n_keys, out)

    def finish(row0, accs):
        o = [a[:, :LANES] / a[:, LANES:] for a in accs]
        lane = lax.broadcasted_iota(jnp.int32, o[0].shape, 1)
        o_ref[_tile_rows(row0, tq), :] = jnp.where(lane < MLA_V, o[0], o[1]).astype(BF16)

    _attention_tiles(2, scores, weights, finish, tq, ctx_len, k_ref.shape[0])


def _mla_attention(q, k, v, tq, ctx_len):
    b, t, _ = q.shape
    return pl.pallas_call(
        functools.partial(_mla_attn_kernel, tq=tq, ctx_len=ctx_len),
        grid=(b, MLA_HEADS // 2),
        in_specs=[pl.BlockSpec((None, t, 2 * LANES), lambda i, h: (i, 0, h)),
                  pl.BlockSpec((None, t, 2 * LANES), lambda i, h: (i, 0, h)),
                  pl.BlockSpec((None, t, LANES), lambda i, h: (i, 0, h))],
        out_specs=pl.BlockSpec((None, t, LANES), lambda i, h: (i, 0, h)),
        out_shape=jax.ShapeDtypeStruct((b, t, MLA_HEADS * MLA_V), BF16),
        scratch_shapes=_attention_scratch(2, tq, t),
        compiler_params=_params(2, VMEM_LIMIT),
        name="mla_attention",
    )(q, k, v)


def _ret_chunk(q, k, v, decays, st, head_lanes):
    dmat, q_dec, k_dec, c_dec = decays
    qm = jnp.where(head_lanes, q, 0.0)
    att = _dot_nt(qm.astype(BF16), k.astype(BF16)) * dmat
    o = _dot(att.astype(BF16), v) + _dot_nt((qm * q_dec).astype(BF16), st.astype(BF16))
    st = st * c_dec + _dot_tn(v, (k * k_dec).astype(BF16))
    return o, st


def _ret_decays(lg, reverse):
    c = CHUNK
    row = lax.broadcasted_iota(jnp.int32, (c, LANES), 0)
    tau = ((c - 1 - row) if reverse else row).astype(F32)
    ti = lax.broadcasted_iota(jnp.int32, (c, c), 0)
    si = lax.broadcasted_iota(jnp.int32, (c, c), 1)
    rel = ((si - ti) if reverse else (ti - si)).astype(F32)
    dmat = jnp.where(rel >= 0, jnp.exp(jnp.maximum(rel, 0.0) * lg[:, :c]), 0.0)
    return dmat, jnp.exp((tau + 1.0) * lg), jnp.exp((c - 1.0 - tau) * lg), jnp.exp(c * lg)


def _ret_kernel(qkf_ref, vf_ref, qkb_ref, vb_ref, lg_ref, of_ref, ob_ref, sf_ref, sb_ref, *, n_chunks):
    @pl.when(pl.program_id(1) == 0)
    def _():
        sf_ref[...] = jnp.zeros_like(sf_ref)
        sb_ref[...] = jnp.zeros_like(sb_ref)

    per_block = LANES // RET_QK
    lane = lax.broadcasted_iota(jnp.int32, (CHUNK, LANES), 1)
    dirs = ((qkf_ref, vf_ref, of_ref, sf_ref), (qkb_ref, vb_ref, ob_ref, sb_ref))
    chains = [(d, h) for d in range(2) for h in range(RET_HEADS)]
    decays = {(d, h): _ret_decays(_log_sigmoid(lg_ref[d, h]), d == 1) for d, h in chains}
    states = {(d, h): dirs[d][3][h] for d, h in chains}
    for step in range(n_chunks):
        for d, h in chains:
            qk_ref, v_ref, o_ref, _ = dirs[d]
            ci = step if d == 0 else n_chunks - 1 - step
            rows = slice(ci * CHUNK, (ci + 1) * CHUNK)
            q_cols = _lane_blocks(2 * RET_QK_WIDTH)[h // per_block]
            k_cols = _lane_blocks(2 * RET_QK_WIDTH)[RET_QK_WIDTH // LANES + h // per_block]
            v_cols = _lane_blocks(RET_V_WIDTH)[h]
            o, states[d, h] = _ret_chunk(qk_ref[rows, q_cols], qk_ref[rows, k_cols], v_ref[rows, v_cols],
                                         decays[d, h], states[d, h], (lane // RET_QK) == (h % per_block))
            o_ref[rows, v_cols] = o
    for (d, h), st in states.items():
        dirs[d][3][h] = st


def _ret_scan(qk, vg, decay, tb, n_ctx_blocks):
    b, t, _ = qk.shape
    nb = t // tb
    fwd, bwd = _scan_block_maps(nb, n_ctx_blocks)

    def rows(width, order):
        return pl.BlockSpec((None, tb, width), lambda i, j: (i, order(j), 0))

    out = jax.ShapeDtypeStruct((b, t, RET_V_WIDTH), F32)
    state = pltpu.VMEM((RET_HEADS, RET_V, LANES), F32)
    return pl.pallas_call(
        functools.partial(_ret_kernel, n_chunks=tb // CHUNK),
        grid=(b, nb),
        in_specs=[rows(2 * RET_QK_WIDTH, fwd), rows(RET_V_WIDTH, fwd), rows(2 * RET_QK_WIDTH, bwd),
                  rows(RET_V_WIDTH, bwd), _resident(decay)],
        out_specs=[rows(RET_V_WIDTH, fwd), rows(RET_V_WIDTH, bwd)],
        out_shape=[out, out],
        scratch_shapes=[state, state],
        compiler_params=_params(2),
        name="retention_scan",
    )(qk, vg, qk, vg, decay)


def _diff_attn_kernel(q_ref, k_ref, v_ref, lam_ref, g_ref, o_ref, s0_ref, s1_ref, m0_ref, m1_ref,
                      *, tq, ctx_len, lambda_init):
    lp = lam_ref[...]
    lam = (jnp.exp(jnp.sum(lp[0:1] * lp[1:2], axis=-1, keepdims=True))
           - jnp.exp(jnp.sum(lp[2:3] * lp[3:4], axis=-1, keepdims=True)) + lambda_init)

    s_bufs, m_bufs = (s0_ref, s1_ref), (m0_ref, m1_ref)
    all_lanes = slice(0, LANES)

    def scores(i, row0, buf, n_keys):
        q = q_ref[_tile_rows(row0, tq), :]
        first = lax.broadcasted_iota(jnp.int32, q.shape, 1) < DIFF_QK
        q = jnp.where(first == (i == 0), q, jnp.zeros_like(q))
        return _score_steps(q, k_ref, all_lanes, n_keys, s_bufs[buf].at[i], m_bufs[buf].at[i])

    def weights(i, buf, n_keys, out):
        return _weight_steps(s_bufs[buf].at[i], m_bufs[buf].at[i], v_ref, n_keys, out)

    def finish(row0, accs):
        o = accs[0][:, :LANES] / accs[0][:, LANES:] - lam * (accs[1][:, :LANES] / accs[1][:, LANES:])
        o_ref[_tile_rows(row0, tq), :] = (_rms(o) * g_ref[...] * (1.0 - lambda_init)).astype(BF16)

    _attention_tiles(2, scores, weights, finish, tq, ctx_len, k_ref.shape[0])


def _diff_attention(q, k, v, lam_params, subln_g, tq, ctx_len, lambda_init):
    b, t, _ = q.shape
    head = pl.BlockSpec((None, t, LANES), lambda i, h: (i, 0, h))
    return pl.pallas_call(
        functools.partial(_diff_attn_kernel, tq=tq, ctx_len=ctx_len, lambda_init=lambda_init),
        grid=(b, DIFF_HEADS),
        in_specs=[head, head, head, _resident(lam_params), _resident(subln_g)],
        out_specs=head,
        out_shape=jax.ShapeDtypeStruct((b, t, DIFF_V_WIDTH), BF16),
        scratch_shapes=_attention_scratch(2, tq, t),
        compiler_params=_params(2, VMEM_LIMIT),
        name="diff_attention",
    )(q, k, v, lam_params, subln_g)


def _mix_ffn_kernel(*refs, n_stream, n_ctx_tiles, row_off, final):
    (of_ref, ob_ref, gate_ref, mix_ref, ng_ref, mod_ref, g_ref, wo_ref, wg_ref, wu_ref, wd_ref) = refs[n_stream:n_stream + 11]
    rest = refs[n_stream + 11:]
    o_ref = rest[-1]
    n_samples, tm, wa = of_ref.shape
    x_in = _stream_tile(refs[:n_stream], n_ctx_tiles, row_off)

    def per_sample(fn):
        return jnp.concatenate([fn(s, slice(s * tm, (s + 1) * tm)) for s in range(n_samples)], axis=0)

    def read_out(s, _):
        o = of_ref[s] + ob_ref[s]
        gate = gate_ref[s].astype(F32)
        return jnp.concatenate([(_rms(o[:, cols]) * ng_ref[...] * _silu(gate[:, cols])).astype(BF16)
                                for cols in _lane_blocks(wa)], axis=1)

    u = (_dot(per_sample(read_out), wo_ref[0:wa, :])
         + _dot(per_sample(lambda s, _: mix_ref[s]), wo_ref[wa:, :]))
    x = per_sample(lambda s, rows: x_in[s] + mod_ref[s, 2:3, :] * u[rows])
    h = per_sample(lambda s, rows: (_rms(x[rows]) * g_ref[1:2, :] * (1.0 + mod_ref[s, 4:5, :])
                                    + mod_ref[s, 3:4, :]).astype(BF16))
    n_hidden = wg_ref.shape[1]
    split = (n_hidden // 2 + MXU_WIDTH - 1) // MXU_WIDTH * MXU_WIDTH
    ffn = None
    for cols in (slice(0, split), slice(split, n_hidden)):
        act = (_silu(_dot(h, wg_ref[:, cols])) * _dot(h, wu_ref[:, cols])).astype(BF16)
        part = _dot(act, wd_ref[cols, :])
        ffn = part if ffn is None else ffn + part
    for s in range(n_samples):
        rows = slice(s * tm, (s + 1) * tm)
        y = x[rows] + mod_ref[s, 5:6, :] * ffn[rows]
        o_ref[s] = _rms(y) * rest[0][...] if final else y


def _mix_ffn(stream, o_f, o_b, gates, gate_block, mix, norm_gain, mod, norm_g, w_out, wg, wu, wd, layer, tm,
             n_ctx_tiles, final_g=None):
    b, t, wa = o_f.shape
    d = w_out.shape[2]
    final = final_g is not None
    off = n_ctx_tiles if final else 0
    n_rows = t - off * tm
    g = MIX_GROUP if b % MIX_GROUP == 0 else 1
    stream_specs, stream_args = _stream_specs(stream, tm, n_ctx_tiles, off, g)
    in_specs = stream_specs + [
        _rows(tm, wa, off, g), _rows(tm, wa, off, g), _rows(tm, wa, off, g, gate_block),
        _rows(tm, mix.shape[2], off, g), _resident(norm_gain), _mod_spec(layer, b, n_ctx_tiles, d, off, g),
        _norm_spec(layer, d), _resident(w_out, layer), _resident(wg, layer), _resident(wu, layer),
        _resident(wd, layer)]
    args = stream_args + [o_f, o_b, gates, mix, norm_gain, mod, norm_g, w_out, wg, wu, wd]
    if final:
        in_specs.append(_resident(final_g))
        args.append(final_g)
    return pl.pallas_call(
        functools.partial(_mix_ffn_kernel, n_stream=len(stream_args), n_ctx_tiles=n_ctx_tiles, row_off=off,
                          final=final),
        grid=(b // g, n_rows // tm),
        in_specs=in_specs,
        out_specs=_rows(tm, d, group=g),
        out_shape=jax.ShapeDtypeStruct((b, n_rows, d), F32),
        compiler_params=_params(2, MIX_VMEM_LIMIT),
        name="mix_ffn",
    )(*args)


def _rope_tables(ctx_len, seq, kind, scale=1.0):
    f32 = np.float32
    lane = np.arange(LANES)
    n = np.arange(seq)
    if kind == "ret":
        half = RET_QK // 2
        idx = lane % RET_QK
        active = np.ones(LANES, bool)
        pos = np.broadcast_to(n[:, None], (seq, LANES))
    else:
        width = DIFF_QK if kind == "diff" else MLA_ROPE
        half = width // 4
        if kind == "diff":
            idx = lane % (width // 2)
            by_col = (lane % width) >= width // 2
            active = np.ones(LANES, bool)
        else:
            idx = (lane - MLA_NOPE) % (width // 2)
            by_col = (lane - MLA_NOPE) >= width // 2
            active = (lane >= MLA_NOPE) & (lane < MLA_NOPE + width)
        pos = np.where(by_col[None, :], (n % GRID_W)[:, None], (n // GRID_W)[:, None])
    upper = (idx >= half)[None, :]
    inv = f32(ROPE_BASE) ** (-(idx % half).astype(f32) / f32(half))
    ang = pos.astype(f32) * inv[None, :].astype(f32)
    cos = np.where(active[None, :], np.cos(ang), f32(1.0)).astype(f32)
    sin = np.where(active[None, :], np.sin(ang), f32(0.0)).astype(f32)
    lat = np.stack([cos, np.where(upper, sin, f32(0.0)), np.where(upper, f32(0.0), -sin)])
    ident = np.stack([np.ones((ctx_len, LANES), f32), np.zeros((ctx_len, LANES), f32),
                      np.zeros((ctx_len, LANES), f32)])
    return jnp.asarray(np.concatenate([ident, lat], axis=1) * f32(scale), F32)


def kernel(x, c, ctx, c_ctx, ada_w, ada_b, norm_g, mix_w_out, ffn_w_gate, ffn_w_up, ffn_w_down,
           even_w_in, hgrn_lb_logits, hgrn_norm_g, mla_q_norm_g, mla_w_uq, mla_kv_norm_g, mla_w_ukv,
           odd_w_in, ret_decay_logits, diff_lambda, diff_subln_g, final_norm_g):
    batch, seq, d = x.shape
    ctx_len = ctx.shape[1]
    depth = ada_w.shape[0]
    tm = math.gcd(ROW_TILE, math.gcd(ctx_len, seq))
    n_ctx_tiles = ctx_len // tm
    assert tm % (2 * CHUNK) == 0 and seq % GRID_W == 0

    mod_rows = -(-(batch + MIX_GROUP) // 8) * 8
    s_rows = jnp.concatenate([c, jnp.broadcast_to(c_ctx[None, :], (MIX_GROUP, d)),
                              jnp.zeros((mod_rows - batch - MIX_GROUP, d), F32)], axis=0)
    mod = _modulation(s_rows, ada_w, ada_b).reshape(depth, mod_rows, 6, d)

    lb_soft = jax.nn.softmax(hgrn_lb_logits.astype(F32), axis=0)
    lbs = jnp.cumsum(lb_soft, axis=0) - lb_soft[0:1]

    tab_mla_q = _rope_tables(ctx_len, seq, "mla", (MLA_NOPE + MLA_ROPE) ** -0.5)
    tab_mla_k = _rope_tables(ctx_len, seq, "mla")
    tab_ret = _rope_tables(ctx_len, seq, "ret")
    tab_diff = _rope_tables(ctx_len, seq, "diff")

    stream = (ctx, x)
    even_w = _even_weights(even_w_in, tm)
    odd_w, w_out, w_gate, w_up, w_down = (a.astype(BF16) for a in (odd_w_in, mix_w_out, ffn_w_gate, ffn_w_up,
                                                                    ffn_w_down))
    for layer in range(depth):
        if layer % 2 == 0:
            e = layer // 2
            wq = mla_w_uq[e].reshape(MLA_Q_RANK, MLA_HEADS, MLA_NOPE + MLA_ROPE)
            wq = jnp.pad(wq, ((0, 0), (0, 0), (0, LANES - MLA_NOPE - MLA_ROPE))).reshape(MLA_Q_RANK, -1)
            wkv = mla_w_ukv[e].reshape(MLA_KV_RANK, MLA_HEADS, MLA_NOPE + MLA_V)
            wk = jnp.pad(wkv[:, :, :MLA_NOPE], ((0, 0), (0, 0), (0, LANES - MLA_NOPE))).reshape(MLA_KV_RANK, -1)
            wv = wkv[:, :, MLA_NOPE:].reshape(MLA_KV_RANK, -1)
            gates, q, k, v = _even_in_projection(
                stream, ctx_len + seq, mod, norm_g, even_w, e, mla_q_norm_g[e][None, :], mla_kv_norm_g[e][None, :],
                wq.astype(BF16),
                jnp.concatenate([wk, wv], axis=1).astype(BF16), tab_mla_q, tab_mla_k, layer, tm, n_ctx_tiles)
            o_f, o_b = _hgrn_scan(gates, lbs[e].reshape(2, 1, HGRN_WIDTH), tm, n_ctx_tiles)
            mix = _mla_attention(q, k, v, tm, ctx_len)
            gate_block, gain = 4, hgrn_norm_g[e][None, :]
        else:
            o = layer // 2
            qk_ret, gates, q_d, k_d, v_d = _odd_in_projection(stream, mod, norm_g, odd_w, o, tab_ret, tab_diff, layer,
                                                              tm, n_ctx_tiles)
            decay = jnp.broadcast_to(ret_decay_logits[o][:, :, None, None], (2, RET_HEADS, 1, LANES))
            o_f, o_b = _ret_scan(qk_ret, gates, decay, tm, n_ctx_tiles)
            lambda_init = 0.8 - 0.6 * math.exp(-0.3 * layer)
            mix = _diff_attention(q_d, k_d, v_d, diff_lambda[o], diff_subln_g[o][None, :], tm, ctx_len, lambda_init)
            gate_block, gain = 1, jnp.ones((1, LANES), F32)
        stream = _mix_ffn(stream, o_f, o_b, gates, gate_block, mix, gain, mod, norm_g, w_out, w_gate, w_up, w_down,
                          layer, tm, n_ctx_tiles, final_norm_g[None, :] if layer == depth - 1 else None)
    return stream
```

```python
import functools
import math

import numpy as np
import jax
import jax.numpy as jnp
from jax import lax
from jax.experimental import pallas as pl
from jax.experimental.pallas import tpu as pltpu

F32 = jnp.float32
BF16 = jnp.bfloat16

LANES = 128
GRID_W = 64
CHUNK = 64
RET_CHUNK = 128
ROPE_BASE = 10000.0
EPS = 1e-6

HGRN_HEADS, HGRN_DIM = 4, 128
HGRN_WIDTH = HGRN_HEADS * HGRN_DIM
MLA_HEADS, MLA_Q_RANK, MLA_KV_RANK, MLA_NOPE, MLA_ROPE, MLA_V = 8, 384, 256, 64, 32, 64
RET_HEADS, RET_QK, RET_V = 4, 64, 128
DIFF_HEADS, DIFF_QK, DIFF_V = 4, 64, 128
RET_QK_WIDTH = RET_HEADS * RET_QK
RET_V_WIDTH = RET_HEADS * RET_V
DIFF_QK_WIDTH = DIFF_HEADS * 2 * DIFF_QK
DIFF_V_WIDTH = DIFF_HEADS * DIFF_V

EV_HGRN = 5 * HGRN_WIDTH
EV_CQ = 0
EV_CKV = EV_CQ + MLA_Q_RANK
EV_KROPE = EV_CKV + MLA_KV_RANK
EV_HGRN0 = EV_KROPE + LANES
OD_KR = RET_QK_WIDTH
OD_VR = OD_KR + RET_QK_WIDTH
OD_QD = OD_VR + 2 * RET_V_WIDTH
OD_KD = OD_QD + DIFF_QK_WIDTH
OD_VD = OD_KD + DIFF_QK_WIDTH

ROW_TILE = 256
KEY_CHUNK = 768
VMEM_LIMIT = 48 * 1024 * 1024
MIX_VMEM_LIMIT = 56 * 1024 * 1024
MIX_GROUP = 2
MXU_WIDTH = 256


def _params(n_axes, vmem=None):
    return pltpu.CompilerParams(dimension_semantics=("arbitrary",) * n_axes, vmem_limit_bytes=vmem)


def _resident(a, index=None):
    if index is None:
        return pl.BlockSpec(a.shape, lambda *_: (0,) * a.ndim, pipeline_mode=pl.Buffered(1))
    return pl.BlockSpec((None,) + a.shape[1:], lambda *_: (index,) + (0,) * (a.ndim - 1),
                        pipeline_mode=pl.Buffered(1))


def _dot(a, b):
    return jnp.dot(a, b, preferred_element_type=F32)


def _dot_nt(a, b):
    return lax.dot_general(a, b, (((1,), (1,)), ((), ())), preferred_element_type=F32)


def _dot_tn(a, b):
    return lax.dot_general(a, b, (((0,), (0,)), ((), ())), preferred_element_type=F32)


def _rms(x):
    return x * lax.rsqrt(jnp.mean(x * x, axis=-1, keepdims=True) + EPS)


def _silu(x):
    return x * jax.nn.sigmoid(x)


def _log_sigmoid(z):
    return jnp.minimum(z, 0.0) - jnp.log1p(jnp.exp(-jnp.abs(z)))


def _rope(x, tab_ref, shift):
    return x * tab_ref[0] + pltpu.roll(x, shift, 1) * tab_ref[1] + pltpu.roll(x, LANES - shift, 1) * tab_ref[2]


def _lane_blocks(width):
    return [slice(i * LANES, (i + 1) * LANES) for i in range(width // LANES)]


def _mod_kernel(s_ref, w_ref, b_ref, o_ref):
    s = _silu(s_ref[...])
    rows = s.shape[0]
    s_hi = s.astype(BF16)
    rest = s - s_hi.astype(F32)
    s_mid = rest.astype(BF16)
    pieces = jnp.concatenate([s_hi, s_mid, (rest - s_mid.astype(F32)).astype(BF16)], axis=0)
    w = w_ref[...]
    w_hi = w.astype(BF16)
    acc = _dot(pieces, w_hi) + _dot(pieces, (w - w_hi.astype(F32)).astype(BF16))
    o_ref[...] = acc[:rows] + acc[rows:2 * rows] + acc[2 * rows:] + b_ref[...]


def _modulation(s_rows, ada_w, ada_b):
    depth, d, d6 = ada_w.shape
    rows = s_rows.shape[0]
    tn = d
    return pl.pallas_call(
        _mod_kernel,
        grid=(depth, d6 // tn),
        in_specs=[pl.BlockSpec((rows, d), lambda l, n: (0, 0)),
                  pl.BlockSpec((None, d, tn), lambda l, n: (l, 0, n)),
                  pl.BlockSpec((None, 1, tn), lambda l, n: (l, 0, n))],
        out_specs=pl.BlockSpec((None, rows, tn), lambda l, n: (l, 0, n)),
        out_shape=jax.ShapeDtypeStruct((depth, rows, d6), F32),
        compiler_params=_params(2),
        name="modulation",
    )(s_rows, ada_w, ada_b.reshape(depth, 1, d6))


def _mod_spec(layer, batch, n_ctx_tiles, d, row_off=0, group=None):
    return pl.BlockSpec((None, group, 6, d),
                        lambda b, t: (layer, jnp.where(t + row_off < n_ctx_tiles, batch // (group or 1), b), 0, 0))


def _norm_spec(layer, d):
    return pl.BlockSpec((None, 2, d), lambda i, j: (layer, 0, 0))


def _rows(tm, width, row_off=0, group=None, col_block=0):
    return pl.BlockSpec((group, tm, width), lambda i, j: (i, j + row_off, col_block))


def _stream_specs(stream, tm, n_ctx_tiles, row_off=0, group=None):
    if not isinstance(stream, tuple):
        return [_rows(tm, stream.shape[2], row_off, group)], [stream]
    d = stream[0].shape[2]
    return ([pl.BlockSpec((group, tm, d), lambda i, j: (i, jnp.minimum(j + row_off, n_ctx_tiles - 1), 0)),
             pl.BlockSpec((group, tm, d), lambda i, j: (i, jnp.maximum(j + row_off - n_ctx_tiles, 0), 0))],
            list(stream))


def _stream_tile(refs, n_ctx_tiles, row_off=0):
    if len(refs) == 1:
        return refs[0][...]
    return jnp.where(pl.program_id(1) + row_off < n_ctx_tiles, refs[0][...], refs[1][...])


def _even_in_kernel(*refs, n_ctx_tiles):
    (mod_ref, g_ref, w_ref, gq_ref, gkv_ref, wq_ref, wkv_ref, tq_ref, tk_ref, hg_ref, q_ref, k_ref, v_ref) = refs[-13:]
    shift = MLA_ROPE // 4
    n_samples, tm, _ = hg_ref.shape
    x = _stream_tile(refs[:-13], n_ctx_tiles)
    h = jnp.concatenate([(_rms(x[s]) * g_ref[0:1, :] * (1.0 + mod_ref[s, 1:2, :]) + mod_ref[s, 0:1, :])
                         .astype(BF16) for s in range(n_samples)], axis=0)
    p = _dot(h, w_ref[:, :EV_HGRN0])
    hg = _dot(h, w_ref[:, EV_HGRN0:]).astype(BF16)
    q_all = _dot((_rms(p[:, EV_CQ:EV_CKV]) * gq_ref[...]).astype(BF16), wq_ref[...])
    kv_all = _dot((_rms(p[:, EV_CKV:EV_KROPE]) * gkv_ref[...]).astype(BF16), wkv_ref[...])
    for s in range(n_samples):
        rows = slice(s * tm, (s + 1) * tm)
        hg_ref[s] = hg[rows]
        q, kv = q_all[rows], kv_all[rows]
        k_rope = _rope(pltpu.roll(p[rows, EV_KROPE:], MLA_NOPE, 1), tk_ref, shift)
        for cols in _lane_blocks(MLA_HEADS * LANES):
            q_ref[s, :, cols] = _rope(q[:, cols], tq_ref, shift).astype(BF16)
            k_ref[s, :, cols] = (kv[:, cols] + k_rope).astype(BF16)
        v_ref[s] = kv[:, MLA_HEADS * LANES:].astype(BF16)


def _even_in_projection(stream, t, mod, norm_g, w, w_index, gq, gkv, wq, wkv, tab_q, tab_k, layer, tm, n_ctx_tiles):
    d = w.shape[1]
    b = stream[0].shape[0] if isinstance(stream, tuple) else stream.shape[0]
    hq = MLA_HEADS * LANES
    tab = pl.BlockSpec((3, tm, LANES), lambda i, j: (0, j, 0))
    shapes = [(EV_HGRN, BF16), (hq, BF16), (hq, BF16), (MLA_HEADS * MLA_V, BF16)]
    g = MIX_GROUP if b % MIX_GROUP == 0 else 1
    stream_specs, stream_args = _stream_specs(stream, tm, n_ctx_tiles, group=g)
    return pl.pallas_call(
        functools.partial(_even_in_kernel, n_ctx_tiles=n_ctx_tiles),
        grid=(b // g, t // tm),
        in_specs=stream_specs + [_mod_spec(layer, b, n_ctx_tiles, d, group=g), _norm_spec(layer, d),
                                 _resident(w, w_index), _resident(gq), _resident(gkv), _resident(wq),
                                 _resident(wkv), tab, tab],
        out_specs=[_rows(tm, width, group=g) for width, _ in shapes],
        out_shape=[jax.ShapeDtypeStruct((b, t, width), dt) for width, dt in shapes],
        compiler_params=_params(2, VMEM_LIMIT),
        name="even_in_projection",
    )(*stream_args, mod, norm_g, w, gq, gkv, wq, wkv, tab_q, tab_k)


def _even_w_kernel(w_ref, o_ref):
    n_low = w_ref.shape[1] - EV_HGRN
    o_ref[:, :n_low] = w_ref[:, EV_HGRN:].astype(BF16)
    o_ref[:, n_low:EV_HGRN0] = jnp.zeros((o_ref.shape[0], EV_HGRN0 - n_low), BF16)
    o_ref[:, EV_HGRN0:EV_HGRN0 + HGRN_WIDTH] = (w_ref[:, :HGRN_WIDTH] * (HGRN_DIM ** -0.5)).astype(BF16)
    o_ref[:, EV_HGRN0 + HGRN_WIDTH:] = w_ref[:, HGRN_WIDTH:EV_HGRN].astype(BF16)


def _even_weights(even_w_in, tm):
    n, d, cols = even_w_in.shape
    return pl.pallas_call(
        _even_w_kernel,
        grid=(n, d // tm),
        in_specs=[pl.BlockSpec((None, tm, cols), lambda e, r: (e, r, 0))],
        out_specs=pl.BlockSpec((None, tm, EV_HGRN0 + EV_HGRN), lambda e, r: (e, r, 0)),
        out_shape=jax.ShapeDtypeStruct((n, d, EV_HGRN0 + EV_HGRN), BF16),
        compiler_params=_params(2),
        name="even_weights",
    )(even_w_in)


def _odd_in_kernel(x_ref, mod_ref, g_ref, w_ref, tr_ref, td_ref, qkr_ref, vg_ref, qd_ref, kd_ref, vd_ref):
    n_samples, tm, _ = x_ref.shape
    h = jnp.concatenate([(_rms(x_ref[s]) * g_ref[0:1, :] * (1.0 + mod_ref[s, 1:2, :]) + mod_ref[s, 0:1, :])
                         .astype(BF16) for s in range(n_samples)], axis=0)
    p_all = _dot(h, w_ref[...])
    for s in range(n_samples):
        p = p_all[s * tm:(s + 1) * tm]
        for cols in _lane_blocks(RET_QK_WIDTH):
            qkr_ref[s, :, cols] = _rope(p[:, cols], tr_ref, RET_QK // 2)
            kc = slice(cols.start + OD_KR, cols.stop + OD_KR)
            qkr_ref[s, :, kc] = _rope(p[:, kc], tr_ref, RET_QK // 2) * (RET_QK ** -0.5)
        vg_ref[s] = p[:, OD_VR:OD_QD].astype(BF16)
        for cols in _lane_blocks(DIFF_QK_WIDTH):
            qc = slice(cols.start + OD_QD, cols.stop + OD_QD)
            kc = slice(cols.start + OD_KD, cols.stop + OD_KD)
            qd_ref[s, :, cols] = (_rope(p[:, qc], td_ref, DIFF_QK // 4) * (DIFF_QK ** -0.5)).astype(BF16)
            kd_ref[s, :, cols] = _rope(p[:, kc], td_ref, DIFF_QK // 4).astype(BF16)
        vd_ref[s] = p[:, OD_VD:].astype(BF16)


def _odd_in_projection(x, mod, norm_g, w, w_index, tab_ret, tab_diff, layer, tm, n_ctx_tiles):
    b, t, d = x.shape
    tab = pl.BlockSpec((3, tm, LANES), lambda i, j: (0, j, 0))
    shapes = [(2 * RET_QK_WIDTH, F32), (2 * RET_V_WIDTH, BF16), (DIFF_QK_WIDTH, BF16), (DIFF_QK_WIDTH, BF16),
              (DIFF_V_WIDTH, BF16)]
    g = MIX_GROUP if b % MIX_GROUP == 0 else 1
    return pl.pallas_call(
        _odd_in_kernel,
        grid=(b // g, t // tm),
        in_specs=[_rows(tm, d, group=g), _mod_spec(layer, b, n_ctx_tiles, d, group=g), _norm_spec(layer, d),
                  _resident(w, w_index), tab, tab],
        out_specs=[_rows(tm, width, group=g) for width, _ in shapes],
        out_shape=[jax.ShapeDtypeStruct((b, t, width), dt) for width, dt in shapes],
        compiler_params=_params(2, VMEM_LIMIT),
        name="odd_in_projection",
    )(x, mod, norm_g, w, tab_ret, tab_diff)


_LEVELS = 6
HGRN_STEP_HEADS = 4


def _gla_masks():
    c = CHUNK
    masks = np.zeros((_LEVELS + 1, c, c), np.float32)
    masks[0] = np.eye(c)
    for lv in range(_LEVELS):
        b = 32 >> lv
        for tau in range(c):
            mid = (tau // (2 * b)) * 2 * b + b
            if tau >= mid:
                masks[lv + 1, tau, mid - b:mid] = 1.0
    mk = np.stack([masks, masks[:, ::-1, ::-1]]).reshape(2, (_LEVELS + 1) * c, c)
    return jnp.asarray(mk, F32)


def _gla_gates(z, lb):
    k = (1.0 - lb) * jax.nn.sigmoid(-z)
    return k, (_log_sigmoid(z) + jnp.log1p(lb * jnp.exp(-z))) * math.log2(math.e)


SUBLANES = 8


def _scan_positions(reverse):
    sub = lax.broadcasted_iota(jnp.int32, (SUBLANES, LANES), 0)
    return (SUBLANES - 1 - sub) if reverse else sub


def _gla_tile_consts(reverse):
    pos = _scan_positions(reverse)
    consts = {("from", s): (pos >= s).astype(F32) for s in (1, 2, 4)}
    consts["right", 1] = (pos % 2 == 1).astype(F32)
    for b in (2, 4):
        consts["sign", b] = jnp.where(pos % (2 * b) >= b, 1.0, -1.0)
    return consts


def _gla_decays(g2, reverse, consts):
    n_tiles = CHUNK // SUBLANES
    pos = _scan_positions(reverse)
    tiles = [g2[v * SUBLANES:(v + 1) * SUBLANES] for v in range(n_tiles)]

    def from_earlier(x, s):
        return pltpu.roll(x, (SUBLANES - s) if reverse else s, 0)

    def at(x, p):
        j = SUBLANES - 1 - p if reverse else p
        return jnp.broadcast_to(x[j:j + 1], x.shape)

    scan = []
    for x in tiles:
        for s in (1, 2, 4):
            x = x + from_earlier(x, s) * consts["from", s]
        scan.append(x)
    order = list(reversed(range(n_tiles))) if reverse else list(range(n_tiles))
    prefix, carry = {}, None
    for v in order:
        prefix[v] = scan[v] if carry is None else scan[v] + carry
        carry = at(prefix[v], SUBLANES - 1)
    total = carry
    p_tiles = [prefix[v] for v in order]

    levels = []
    for lv in range(_LEVELS):
        b = 32 >> lv
        out = []
        for n, p in enumerate(p_tiles):
            if b >= SUBLANES:
                span = 2 * b // SUBLANES
                left_last = (n // span) * span + span // 2 - 1
                edge = at(p_tiles[left_last], SUBLANES - 1)
                out.append(p - edge if n % span >= span // 2 else edge - p)
            elif b == 1:
                out.append(tiles[order[n]] * consts["right", 1])
            else:
                edge = at(p, b - 1)
                for first in range(2 * b, SUBLANES, 2 * b):
                    edge = jnp.where(pos >= first, at(p, first + b - 1), edge)
                out.append((p - edge) * consts["sign", b])
        levels.append(out)

    def rows(tile_list):
        return jnp.concatenate(list(reversed(tile_list)) if reverse else tile_list, axis=0)

    dec = [jnp.exp2(rows(t)) for t in levels]
    return dec, jnp.exp2(rows(p_tiles)), jnp.exp2(rows([total - p for p in p_tiles]))


def _sibling_select(qs, k, b, reverse):
    c = CHUNK
    if b % 8 == 0:
        right = lambda r0: (((c - b - r0) if reverse else r0) // b) % 2 == 1
        return jnp.concatenate([(qs if right(r0) else k)[r0:r0 + b] for r0 in range(0, c, b)], axis=0)
    row = lax.broadcasted_iota(jnp.int32, (c, LANES), 0)
    tau = (c - 1 - row) if reverse else row
    return jnp.where((tau & b) != 0, qs, k)


def _gla_chunk(qs, v, k, g2, masks, st, reverse, consts):
    c = CHUNK
    tile_rows = range(0, c, SUBLANES)
    levels, dec_q, dec_k = _gla_decays(g2, reverse, consts)
    att = masks[0:c] * _dot_nt(qs.astype(BF16), k.astype(BF16))
    att = [att[r0:r0 + SUBLANES] for r0 in tile_rows]
    for lv in range(_LEVELS):
        b = 32 >> lv
        zl = (_sibling_select(qs, k, b, reverse) * levels[lv]).astype(BF16)
        term = _dot_nt(zl, zl)
        for n, r0 in enumerate(tile_rows):
            scan_row = (c - SUBLANES - r0) if reverse else r0
            if b < SUBLANES or (scan_row // b) % 2 == 1:
                att[n] = att[n] + masks[(lv + 1) * c + r0:(lv + 1) * c + r0 + SUBLANES] * term[r0:r0 + SUBLANES]
    att = jnp.concatenate(att, axis=0)
    vb = v.astype(BF16)
    o = _dot(att.astype(BF16), vb) + _dot_nt((qs * dec_q).astype(BF16), st.astype(BF16))
    last = 0 if reverse else c - 1
    st = st * dec_q[last:last + 1, :] + _dot_tn(vb, (k * dec_k).astype(BF16))
    return o, st


def _hgrn_kernel(qf_ref, if_ref, zf_ref, qb_ref, ib_ref, zb_ref, lbf_ref, lbb_ref, mk_ref,
                 of_ref, ob_ref, sf_ref, sb_ref, *, n_chunks):
    @pl.when(pl.program_id(2) == 0)
    def _():
        sf_ref[...] = jnp.zeros_like(sf_ref)
        sb_ref[...] = jnp.zeros_like(sb_ref)

    dirs = ((qf_ref, if_ref, zf_ref, lbf_ref, of_ref, sf_ref), (qb_ref, ib_ref, zb_ref, lbb_ref, ob_ref, sb_ref))
    n_heads = sf_ref.shape[0]
    states = {(d, h): dirs[d][5][h] for d in range(2) for h in range(n_heads)}
    consts = [_gla_tile_consts(d == 1) for d in range(2)]
    for step in range(n_chunks):
        for d, (q_ref, i_ref, z_ref, lb_ref, o_ref, _) in enumerate(dirs):
            ci = step if d == 0 else n_chunks - 1 - step
            rows = slice(ci * CHUNK, (ci + 1) * CHUNK)
            for h, cols in enumerate(_lane_blocks(n_heads * LANES)):
                k, g2 = _gla_gates(z_ref[rows, cols].astype(F32), lb_ref[:, cols])
                o, states[d, h] = _gla_chunk(q_ref[rows, cols].astype(F32), i_ref[rows, cols].astype(F32), k, g2,
                                             mk_ref[d], states[d, h], d == 1, consts[d])
                o_ref[rows, cols] = o
    for (d, h), st in states.items():
        dirs[d][5][h] = st


def _scan_block_maps(n_blocks, n_ctx_blocks):
    def bwd(j):
        return jnp.where(j < n_ctx_blocks, n_ctx_blocks - 1 - j, n_blocks - 1 - (j - n_ctx_blocks))
    return (lambda j: j), bwd


def _hgrn_scan(p, lbs, tb, n_ctx_blocks):
    b, t, _ = p.shape
    nb = t // tb
    fwd, bwd = _scan_block_maps(nb, n_ctx_blocks)
    masks = _gla_masks()

    hs = HGRN_STEP_HEADS
    width = hs * LANES
    groups = HGRN_HEADS // hs

    def col(group, order):
        return pl.BlockSpec((None, tb, width), lambda i, h, j: (i, order(j), group * groups + h))

    def lb_spec(d):
        return pl.BlockSpec((None, 1, width), lambda i, h, j: (d, 0, h))

    def out_spec(order):
        return pl.BlockSpec((None, tb, width), lambda i, h, j: (i, order(j), h))

    out = jax.ShapeDtypeStruct((b, t, HGRN_WIDTH), F32)
    state = pltpu.VMEM((hs, HGRN_DIM, HGRN_DIM), F32)
    return pl.pallas_call(
        functools.partial(_hgrn_kernel, n_chunks=tb // CHUNK),
        grid=(b, groups, nb),
        in_specs=[col(0, fwd), col(1, fwd), col(2, fwd), col(0, bwd), col(1, bwd), col(3, bwd),
                  lb_spec(0), lb_spec(1), _resident(masks)],
        out_specs=[out_spec(fwd), out_spec(bwd)],
        out_shape=[out, out],
        scratch_shapes=[state, state],
        compiler_params=_params(3),
        name="hgrn_scan",
    )(p, p, p, p, p, p, lbs, lbs, masks)


def _key_chunk(n_keys):
    return max(c for c in range(LANES, min(KEY_CHUNK, n_keys) + 1, LANES) if n_keys % c == 0)


def _score_steps(q, k_ref, k_cols, n_keys, s_ref, m_ref):
    kc = _key_chunk(n_keys)
    m = None
    for c in range(n_keys // kc):
        s = _dot_nt(q, k_ref[c * kc:(c + 1) * kc, k_cols])
        s_ref[:, c * kc:(c + 1) * kc] = s
        for blk in _lane_blocks(kc):
            m = s[:, blk] if m is None else jnp.maximum(m, s[:, blk])
        yield
    m_ref[...] = jnp.broadcast_to(jnp.max(m, axis=-1, keepdims=True), m.shape)


def _weight_steps(s_ref, m_ref, v_ref, n_keys, out):
    kc = _key_chunk(n_keys)
    m = m_ref[...]
    ones = jnp.ones((kc, LANES), BF16)
    acc = None
    for c in range(n_keys // kc):
        p = jnp.concatenate([jnp.exp(s_ref[:, c * kc + blk.start:c * kc + blk.stop] - m)
                             for blk in _lane_blocks(kc)], axis=1)
        part = _dot(p.astype(BF16), jnp.concatenate([v_ref[c * kc:(c + 1) * kc, :], ones], axis=1))
        acc = part if acc is None else acc + part
        yield
    out.append(acc)


def _interleave(*step_iters):
    live = list(step_iters)
    while live:
        for it in list(live):
            if next(it, StopIteration) is StopIteration:
                live.remove(it)


def _attention_tiles(n_streams, scores, weights, finish, tq, ctx_len, t):
    def weigh(buf, n_keys, row0, *extra):
        accs = []
        for s in range(n_streams):
            out = []
            _interleave(weights(s, buf, n_keys, out), *[e(s) for e in extra])
            accs.append(out[0])
        finish(row0, accs)

    for i in range(ctx_len // tq):
        for s in range(n_streams):
            _interleave(scores(s, i * tq, 0, ctx_len))
        weigh(0, ctx_len, i * tq)
    n_lat = (t - ctx_len) // tq
    for s in range(n_streams):
        _interleave(scores(s, ctx_len, 0, t))

    def next_scores(row0, buf):
        return lambda s: scores(s, row0, buf, t)

    def pair(i, carry):
        row0 = pl.multiple_of(ctx_len + 2 * i * tq, tq)
        weigh(0, t, row0, next_scores(row0 + tq, 1))
        weigh(1, t, row0 + tq, next_scores(row0 + 2 * tq, 0))
        return carry

    n_pairs = (n_lat - 1) // 2
    lax.fori_loop(0, n_pairs, pair, 0)
    row0 = ctx_len + 2 * n_pairs * tq
    if n_lat % 2 == 0:
        weigh(0, t, row0, next_scores(row0 + tq, 1))
        weigh(1, t, row0 + tq)
    else:
        weigh(0, t, row0)


def _tile_rows(row0, tq):
    return pl.ds(row0 if isinstance(row0, int) else pl.multiple_of(row0, tq), tq)


def _attention_scratch(n_streams, tq, t):
    scores = pltpu.VMEM((n_streams, tq, t), F32)
    row_max = pltpu.VMEM((n_streams, tq, LANES), F32)
    return [scores, scores, row_max, row_max]


def _mla_attn_kernel(q_ref, k_ref, v_ref, o_ref, s0_ref, s1_ref, m0_ref, m1_ref, *, tq, ctx_len):
    heads = _lane_blocks(2 * LANES)
    s_bufs, m_bufs = (s0_ref, s1_ref), (m0_ref, m1_ref)

    def scores(h, row0, buf, n_keys):
        return _score_steps(q_ref[_tile_rows(row0, tq), heads[h]], k_ref, heads[h], n_keys,
                            s_bufs[buf].at[h], m_bufs[buf].at[h])

    def weights(h, buf, n_keys, out):
        return _weight_steps(s_bufs[buf].at[h], m_bufs[buf].at[h], v_ref, n_keys, out)

    def finish(row0, accs):
        o = [a[:, :LANES] / a[:, LANES:] for a in accs]
        lane = lax.broadcasted_iota(jnp.int32, o[0].shape, 1)
        o_ref[_tile_rows(row0, tq), :] = jnp.where(lane < MLA_V, o[0], o[1]).astype(BF16)

    _attention_tiles(2, scores, weights, finish, tq, ctx_len, k_ref.shape[0])


def _mla_attention(q, k, v, tq, ctx_len):
    b, t, _ = q.shape
    return pl.pallas_call(
        functools.partial(_mla_attn_kernel, tq=tq, ctx_len=ctx_len),
        grid=(b, MLA_HEADS // 2),
        in_specs=[pl.BlockSpec((None, t, 2 * LANES), lambda i, h: (i, 0, h)),
                  pl.BlockSpec((None, t, 2 * LANES), lambda i, h: (i, 0, h)),
                  pl.BlockSpec((None, t, LANES), lambda i, h: (i, 0, h))],
        out_specs=pl.BlockSpec((None, t, LANES), lambda i, h: (i, 0, h)),
        out_shape=jax.ShapeDtypeStruct((b, t, MLA_HEADS * MLA_V), BF16),
        scratch_shapes=_attention_scratch(2, tq, t),
        compiler_params=_params(2, VMEM_LIMIT),
        name="mla_attention",
    )(q, k, v)


def _ret_chunk(q, k, v, decays, st, head_lanes):
    dmat, q_dec, k_dec, c_dec = decays
    qm = jnp.where(head_lanes, q, 0.0)
    att = _dot_nt(qm.astype(BF16), k.astype(BF16)) * dmat
    o = _dot(att.astype(BF16), v) + _dot_nt((qm * q_dec).astype(BF16), st.astype(BF16))
    st = st * c_dec + _dot_tn(v, (k * k_dec).astype(BF16))
    return o, st


def _ret_decays(lg, reverse):
    c = RET_CHUNK
    row = lax.broadcasted_iota(jnp.int32, (c, LANES), 0)
    tau = ((c - 1 - row) if reverse else row).astype(F32)
    ti = lax.broadcasted_iota(jnp.int32, (c, c), 0)
    si = lax.broadcasted_iota(jnp.int32, (c, c), 1)
    rel = ((si - ti) if reverse else (ti - si)).astype(F32)
    dmat = jnp.where(rel >= 0, jnp.exp(jnp.maximum(rel, 0.0) * lg[:, :1]), 0.0)
    return dmat, jnp.exp((tau + 1.0) * lg), jnp.exp((c - 1.0 - tau) * lg), jnp.exp(c * lg)


def _ret_kernel(qkf_ref, vf_ref, qkb_ref, vb_ref, lg_ref, of_ref, ob_ref, sf_ref, sb_ref, *, n_chunks):
    @pl.when(pl.program_id(1) == 0)
    def _():
        sf_ref[...] = jnp.zeros_like(sf_ref)
        sb_ref[...] = jnp.zeros_like(sb_ref)

    per_block = LANES // RET_QK
    lane = lax.broadcasted_iota(jnp.int32, (RET_CHUNK, LANES), 1)
    dirs = ((qkf_ref, vf_ref, of_ref, sf_ref), (qkb_ref, vb_ref, ob_ref, sb_ref))
    chains = [(d, h) for d in range(2) for h in range(RET_HEADS)]
    decays = {(d, h): _ret_decays(_log_sigmoid(lg_ref[d, h]), d == 1) for d, h in chains}
    states = {(d, h): dirs[d][3][h] for d, h in chains}
    for step in range(n_chunks):
        for d, h in chains:
            qk_ref, v_ref, o_ref, _ = dirs[d]
            ci = step if d == 0 else n_chunks - 1 - step
            rows = slice(ci * RET_CHUNK, (ci + 1) * RET_CHUNK)
            q_cols = _lane_blocks(2 * RET_QK_WIDTH)[h // per_block]
            k_cols = _lane_blocks(2 * RET_QK_WIDTH)[RET_QK_WIDTH // LANES + h // per_block]
            v_cols = _lane_blocks(RET_V_WIDTH)[h]
            o, states[d, h] = _ret_chunk(qk_ref[rows, q_cols], qk_ref[rows, k_cols], v_ref[rows, v_cols],
                                         decays[d, h], states[d, h], (lane // RET_QK) == (h % per_block))
            o_ref[rows, v_cols] = o
    for (d, h), st in states.items():
        dirs[d][3][h] = st


def _ret_scan(qk, vg, decay, tb, n_ctx_blocks):
    b, t, _ = qk.shape
    nb = t // tb
    fwd, bwd = _scan_block_maps(nb, n_ctx_blocks)

    def rows(width, order):
        return pl.BlockSpec((None, tb, width), lambda i, j: (i, order(j), 0))

    out = jax.ShapeDtypeStruct((b, t, RET_V_WIDTH), F32)
    state = pltpu.VMEM((RET_HEADS, RET_V, LANES), F32)
    return pl.pallas_call(
        functools.partial(_ret_kernel, n_chunks=tb // RET_CHUNK),
        grid=(b, nb),
        in_specs=[rows(2 * RET_QK_WIDTH, fwd), rows(RET_V_WIDTH, fwd), rows(2 * RET_QK_WIDTH, bwd),
                  rows(RET_V_WIDTH, bwd), _resident(decay)],
        out_specs=[rows(RET_V_WIDTH, fwd), rows(RET_V_WIDTH, bwd)],
        out_shape=[out, out],
        scratch_shapes=[state, state],
        compiler_params=_params(2),
        name="retention_scan",
    )(qk, vg, qk, vg, decay)


def _diff_attn_kernel(q_ref, k_ref, v_ref, lam_ref, g_ref, o_ref, s0_ref, s1_ref, m0_ref, m1_ref,
                      *, tq, ctx_len, lambda_init):
    lp = lam_ref[...]
    lam = (jnp.exp(jnp.sum(lp[0:1] * lp[1:2], axis=-1, keepdims=True))
           - jnp.exp(jnp.sum(lp[2:3] * lp[3:4], axis=-1, keepdims=True)) + lambda_init)

    s_bufs, m_bufs = (s0_ref, s1_ref), (m0_ref, m1_ref)
    all_lanes = slice(0, LANES)

    def scores(i, row0, buf, n_keys):
        q = q_ref[_tile_rows(row0, tq), :]
        first = lax.broadcasted_iota(jnp.int32, q.shape, 1) < DIFF_QK
        q = jnp.where(first == (i == 0), q, jnp.zeros_like(q))
        return _score_steps(q, k_ref, all_lanes, n_keys, s_bufs[buf].at[i], m_bufs[buf].at[i])

    def weights(i, buf, n_keys, out):
        return _weight_steps(s_bufs[buf].at[i], m_bufs[buf].at[i], v_ref, n_keys, out)

    def finish(row0, accs):
        o = accs[0][:, :LANES] / accs[0][:, LANES:] - lam * (accs[1][:, :LANES] / accs[1][:, LANES:])
        o_ref[_tile_rows(row0, tq), :] = (_rms(o) * g_ref[...] * (1.0 - lambda_init)).astype(BF16)

    _attention_tiles(2, scores, weights, finish, tq, ctx_len, k_ref.shape[0])


def _diff_attention(q, k, v, lam_params, subln_g, tq, ctx_len, lambda_init):
    b, t, _ = q.shape
    head = pl.BlockSpec((None, t, LANES), lambda i, h: (i, 0, h))
    return pl.pallas_call(
        functools.partial(_diff_attn_kernel, tq=tq, ctx_len=ctx_len, lambda_init=lambda_init),
        grid=(b, DIFF_HEADS),
        in_specs=[head, head, head, _resident(lam_params), _resident(subln_g)],
        out_specs=head,
        out_shape=jax.ShapeDtypeStruct((b, t, DIFF_V_WIDTH), BF16),
        scratch_shapes=_attention_scratch(2, tq, t),
        compiler_params=_params(2, VMEM_LIMIT),
        name="diff_attention",
    )(q, k, v, lam_params, subln_g)


def _mix_ffn_kernel(*refs, n_stream, n_ctx_tiles, row_off, final):
    (of_ref, ob_ref, gate_ref, mix_ref, ng_ref, mod_ref, g_ref, wo_ref, wg_ref, wu_ref, wd_ref) = refs[n_stream:n_stream + 11]
    rest = refs[n_stream + 11:]
    o_ref = rest[-1]
    n_samples, tm, wa = of_ref.shape
    x_in = _stream_tile(refs[:n_stream], n_ctx_tiles, row_off)

    def per_sample(fn):
        return jnp.concatenate([fn(s, slice(s * tm, (s + 1) * tm)) for s in range(n_samples)], axis=0)

    def read_out(s, _):
        o = of_ref[s] + ob_ref[s]
        gate = gate_ref[s].astype(F32)
        return jnp.concatenate([(_rms(o[:, cols]) * ng_ref[...] * _silu(gate[:, cols])).astype(BF16)
                                for cols in _lane_blocks(wa)], axis=1)

    u = (_dot(per_sample(read_out), wo_ref[0:wa, :])
         + _dot(per_sample(lambda s, _: mix_ref[s]), wo_ref[wa:, :]))
    x = per_sample(lambda s, rows: x_in[s] + mod_ref[s, 2:3, :] * u[rows])
    h = per_sample(lambda s, rows: (_rms(x[rows]) * g_ref[1:2, :] * (1.0 + mod_ref[s, 4:5, :])
                                    + mod_ref[s, 3:4, :]).astype(BF16))
    n_hidden = wg_ref.shape[1]
    split = (n_hidden // 2 + MXU_WIDTH - 1) // MXU_WIDTH * MXU_WIDTH
    ffn = None
    for cols in (slice(0, split), slice(split, n_hidden)):
        act = (_silu(_dot(h, wg_ref[:, cols])) * _dot(h, wu_ref[:, cols])).astype(BF16)
        part = _dot(act, wd_ref[cols, :])
        ffn = part if ffn is None else ffn + part
    for s in range(n_samples):
        rows = slice(s * tm, (s + 1) * tm)
        y = x[rows] + mod_ref[s, 5:6, :] * ffn[rows]
        o_ref[s] = _rms(y) * rest[0][...] if final else y


def _mix_ffn(stream, o_f, o_b, gates, gate_block, mix, norm_gain, mod, norm_g, w_out, wg, wu, wd, layer, tm,
             n_ctx_tiles, final_g=None):
    b, t, wa = o_f.shape
    d = w_out.shape[2]
    final = final_g is not None
    off = n_ctx_tiles if final else 0
    n_rows = t - off * tm
    g = MIX_GROUP if b % MIX_GROUP == 0 else 1
    stream_specs, stream_args = _stream_specs(stream, tm, n_ctx_tiles, off, g)
    in_specs = stream_specs + [
        _rows(tm, wa, off, g), _rows(tm, wa, off, g), _rows(tm, wa, off, g, gate_block),
        _rows(tm, mix.shape[2], off, g), _resident(norm_gain), _mod_spec(layer, b, n_ctx_tiles, d, off, g),
        _norm_spec(layer, d), _resident(w_out, layer), _resident(wg, layer), _resident(wu, layer),
        _resident(wd, layer)]
    args = stream_args + [o_f, o_b, gates, mix, norm_gain, mod, norm_g, w_out, wg, wu, wd]
    if final:
        in_specs.append(_resident(final_g))
        args.append(final_g)
    return pl.pallas_call(
        functools.partial(_mix_ffn_kernel, n_stream=len(stream_args), n_ctx_tiles=n_ctx_tiles, row_off=off,
                          final=final),
        grid=(b // g, n_rows // tm),
        in_specs=in_specs,
        out_specs=_rows(tm, d, group=g),
        out_shape=jax.ShapeDtypeStruct((b, n_rows, d), F32),
        compiler_params=_params(2, MIX_VMEM_LIMIT),
        name="mix_ffn",
    )(*args)


def _rope_tables(ctx_len, seq, kind, scale=1.0):
    f32 = np.float32
    lane = np.arange(LANES)
    n = np.arange(seq)
    if kind == "ret":
        half = RET_QK // 2
        idx = lane % RET_QK
        active = np.ones(LANES, bool)
        pos = np.broadcast_to(n[:, None], (seq, LANES))
    else:
        width = DIFF_QK if kind == "diff" else MLA_ROPE
        half = width // 4
        if kind == "diff":
            idx = lane % (width // 2)
            by_col = (lane % width) >= width // 2
            active = np.ones(LANES, bool)
        else:
            idx = (lane - MLA_NOPE) % (width // 2)
            by_col = (lane - MLA_NOPE) >= width // 2
            active = (lane >= MLA_NOPE) & (lane < MLA_NOPE + width)
        pos = np.where(by_col[None, :], (n % GRID_W)[:, None], (n // GRID_W)[:, None])
    upper = (idx >= half)[None, :]
    inv = f32(ROPE_BASE) ** (-(idx % half).astype(f32) / f32(half))
    ang = pos.astype(f32) * inv[None, :].astype(f32)
    cos = np.where(active[None, :], np.cos(ang), f32(1.0)).astype(f32)
    sin = np.where(active[None, :], np.sin(ang), f32(0.0)).astype(f32)
    lat = np.stack([cos, np.where(upper, sin, f32(0.0)), np.where(upper, f32(0.0), -sin)])
    ident = np.stack([np.ones((ctx_len, LANES), f32), np.zeros((ctx_len, LANES), f32),
                      np.zeros((ctx_len, LANES), f32)])
    return jnp.asarray(np.concatenate([ident, lat], axis=1) * f32(scale), F32)


def kernel(x, c, ctx, c_ctx, ada_w, ada_b, norm_g, mix_w_out, ffn_w_gate, ffn_w_up, ffn_w_down,
           even_w_in, hgrn_lb_logits, hgrn_norm_g, mla_q_norm_g, mla_w_uq, mla_kv_norm_g, mla_w_ukv,
           odd_w_in, ret_decay_logits, diff_lambda, diff_subln_g, final_norm_g):
    batch, seq, d = x.shape
    ctx_len = ctx.shape[1]
    depth = ada_w.shape[0]
    tm = math.gcd(ROW_TILE, math.gcd(ctx_len, seq))
    n_ctx_tiles = ctx_len // tm
    assert tm % CHUNK == 0 and tm % RET_CHUNK == 0 and seq % GRID_W == 0

    mod_rows = -(-(batch + MIX_GROUP) // 8) * 8
    s_rows = jnp.concatenate([c, jnp.broadcast_to(c_ctx[None, :], (MIX_GROUP, d)),
                              jnp.zeros((mod_rows - batch - MIX_GROUP, d), F32)], axis=0)
    mod = _modulation(s_rows, ada_w, ada_b).reshape(depth, mod_rows, 6, d)

    lb_soft = jax.nn.softmax(hgrn_lb_logits.astype(F32), axis=0)
    lbs = jnp.cumsum(lb_soft, axis=0) - lb_soft[0:1]

    tab_mla_q = _rope_tables(ctx_len, seq, "mla", (MLA_NOPE + MLA_ROPE) ** -0.5)
    tab_mla_k = _rope_tables(ctx_len, seq, "mla")
    tab_ret = _rope_tables(ctx_len, seq, "ret")
    tab_diff = _rope_tables(ctx_len, seq, "diff")

    stream = (ctx, x)
    even_w = _even_weights(even_w_in, tm)
    odd_w, w_out, w_gate, w_up, w_down = (a.astype(BF16) for a in (odd_w_in, mix_w_out, ffn_w_gate, ffn_w_up,
                                                                    ffn_w_down))
    for layer in range(depth):
        if layer % 2 == 0:
            e = layer // 2
            wq = mla_w_uq[e].reshape(MLA_Q_RANK, MLA_HEADS, MLA_NOPE + MLA_ROPE)
            wq = jnp.pad(wq, ((0, 0), (0, 0), (0, LANES - MLA_NOPE - MLA_ROPE))).reshape(MLA_Q_RANK, -1)
            wkv = mla_w_ukv[e].reshape(MLA_KV_RANK, MLA_HEADS, MLA_NOPE + MLA_V)
            wk = jnp.pad(wkv[:, :, :MLA_NOPE], ((0, 0), (0, 0), (0, LANES - MLA_NOPE))).reshape(MLA_KV_RANK, -1)
            wv = wkv[:, :, MLA_NOPE:].reshape(MLA_KV_RANK, -1)
            gates, q, k, v = _even_in_projection(
                stream, ctx_len + seq, mod, norm_g, even_w, e, mla_q_norm_g[e][None, :], mla_kv_norm_g[e][None, :],
                wq.astype(BF16),
                jnp.concatenate([wk, wv], axis=1).astype(BF16), tab_mla_q, tab_mla_k, layer, tm, n_ctx_tiles)
            o_f, o_b = _hgrn_scan(gates, lbs[e].reshape(2, 1, HGRN_WIDTH), tm, n_ctx_tiles)
            mix = _mla_attention(q, k, v, tm, ctx_len)
            gate_block, gain = 4, hgrn_norm_g[e][None, :]
        else:
            o = layer // 2
            qk_ret, gates, q_d, k_d, v_d = _odd_in_projection(stream, mod, norm_g, odd_w, o, tab_ret, tab_diff, layer,
                                                              tm, n_ctx_tiles)
            decay = jnp.broadcast_to(ret_decay_logits[o][:, :, None, None], (2, RET_HEADS, 1, LANES))
            o_f, o_b = _ret_scan(qk_ret, gates, decay, tm, n_ctx_tiles)
            lambda_init = 0.8 - 0.6 * math.exp(-0.3 * layer)
            mix = _diff_attention(q_d, k_d, v_d, diff_lambda[o], diff_subln_g[o][None, :], tm, ctx_len, lambda_init)
            gate_block, gain = 1, jnp.ones((1, LANES), F32)
        stream = _mix_ffn(stream, o_f, o_b, gates, gate_block, mix, gain, mod, norm_g, w_out, w_gate, w_up, w_down,
                          layer, tm, n_ctx_tiles, final_norm_g[None, :] if layer == depth - 1 else None)
    return stream
```

```python
import functools
import math

import numpy as np
import jax
import jax.numpy as jnp
from jax import lax
from jax.experimental import pallas as pl
from jax.experimental.pallas import tpu as pltpu

F32 = jnp.float32
BF16 = jnp.bfloat16

LANES = 128
GRID_W = 64
CHUNK = 128
RET_CHUNK = 128
ROPE_BASE = 10000.0
EPS = 1e-6

HGRN_HEADS, HGRN_DIM = 4, 128
HGRN_WIDTH = HGRN_HEADS * HGRN_DIM
MLA_HEADS, MLA_Q_RANK, MLA_KV_RANK, MLA_NOPE, MLA_ROPE, MLA_V = 8, 384, 256, 64, 32, 64
RET_HEADS, RET_QK, RET_V = 4, 64, 128
DIFF_HEADS, DIFF_QK, DIFF_V = 4, 64, 128
RET_QK_WIDTH = RET_HEADS * RET_QK
RET_V_WIDTH = RET_HEADS * RET_V
DIFF_QK_WIDTH = DIFF_HEADS * 2 * DIFF_QK
DIFF_V_WIDTH = DIFF_HEADS * DIFF_V

EV_HGRN = 5 * HGRN_WIDTH
EV_CQ = 0
EV_CKV = EV_CQ + MLA_Q_RANK
EV_KROPE = EV_CKV + MLA_KV_RANK
EV_HGRN0 = EV_KROPE + LANES
OD_KR = RET_QK_WIDTH
OD_VR = OD_KR + RET_QK_WIDTH
OD_QD = OD_VR + 2 * RET_V_WIDTH
OD_KD = OD_QD + DIFF_QK_WIDTH
OD_VD = OD_KD + DIFF_QK_WIDTH

ROW_TILE = 256
KEY_CHUNK = 768
VMEM_LIMIT = 48 * 1024 * 1024
MIX_VMEM_LIMIT = 56 * 1024 * 1024
MIX_GROUP = 2
MXU_WIDTH = 256


def _params(n_axes, vmem=None):
    return pltpu.CompilerParams(dimension_semantics=("arbitrary",) * n_axes, vmem_limit_bytes=vmem)


def _resident(a, index=None):
    if index is None:
        return pl.BlockSpec(a.shape, lambda *_: (0,) * a.ndim, pipeline_mode=pl.Buffered(1))
    return pl.BlockSpec((None,) + a.shape[1:], lambda *_: (index,) + (0,) * (a.ndim - 1),
                        pipeline_mode=pl.Buffered(1))


def _dot(a, b):
    return jnp.dot(a, b, preferred_element_type=F32)


def _dot_nt(a, b):
    return lax.dot_general(a, b, (((1,), (1,)), ((), ())), preferred_element_type=F32)


def _dot_tn(a, b):
    return lax.dot_general(a, b, (((0,), (0,)), ((), ())), preferred_element_type=F32)


def _rms(x):
    return x * lax.rsqrt(jnp.mean(x * x, axis=-1, keepdims=True) + EPS)


def _silu(x):
    return x * jax.nn.sigmoid(x)


def _log_sigmoid(z):
    return jnp.minimum(z, 0.0) - jnp.log1p(jnp.exp(-jnp.abs(z)))


def _rope(x, tab_ref, shift):
    return x * tab_ref[0] + pltpu.roll(x, shift, 1) * tab_ref[1] + pltpu.roll(x, LANES - shift, 1) * tab_ref[2]


def _lane_blocks(width):
    return [slice(i * LANES, (i + 1) * LANES) for i in range(width // LANES)]


def _mod_kernel(s_ref, w_ref, b_ref, o_ref):
    s = _silu(s_ref[...])
    rows = s.shape[0]
    s_hi = s.astype(BF16)
    rest = s - s_hi.astype(F32)
    s_mid = rest.astype(BF16)
    pieces = jnp.concatenate([s_hi, s_mid, (rest - s_mid.astype(F32)).astype(BF16)], axis=0)
    w = w_ref[...]
    w_hi = w.astype(BF16)
    acc = _dot(pieces, w_hi) + _dot(pieces, (w - w_hi.astype(F32)).astype(BF16))
    o_ref[...] = acc[:rows] + acc[rows:2 * rows] + acc[2 * rows:] + b_ref[...]


def _modulation(s_rows, ada_w, ada_b):
    depth, d, d6 = ada_w.shape
    rows = s_rows.shape[0]
    tn = d
    return pl.pallas_call(
        _mod_kernel,
        grid=(depth, d6 // tn),
        in_specs=[pl.BlockSpec((rows, d), lambda l, n: (0, 0)),
                  pl.BlockSpec((None, d, tn), lambda l, n: (l, 0, n)),
                  pl.BlockSpec((None, 1, tn), lambda l, n: (l, 0, n))],
        out_specs=pl.BlockSpec((None, rows, tn), lambda l, n: (l, 0, n)),
        out_shape=jax.ShapeDtypeStruct((depth, rows, d6), F32),
        compiler_params=_params(2),
        name="modulation",
    )(s_rows, ada_w, ada_b.reshape(depth, 1, d6))


def _mod_spec(layer, batch, n_ctx_tiles, d, row_off=0, group=None):
    return pl.BlockSpec((None, group, 6, d),
                        lambda b, t: (layer, jnp.where(t + row_off < n_ctx_tiles, batch // (group or 1), b), 0, 0))


def _norm_spec(layer, d):
    return pl.BlockSpec((None, 2, d), lambda i, j: (layer, 0, 0))


def _rows(tm, width, row_off=0, group=None, col_block=0):
    return pl.BlockSpec((group, tm, width), lambda i, j: (i, j + row_off, col_block))


def _stream_specs(stream, tm, n_ctx_tiles, row_off=0, group=None):
    if not isinstance(stream, tuple):
        return [_rows(tm, stream.shape[2], row_off, group)], [stream]
    d = stream[0].shape[2]
    return ([pl.BlockSpec((group, tm, d), lambda i, j: (i, jnp.minimum(j + row_off, n_ctx_tiles - 1), 0)),
             pl.BlockSpec((group, tm, d), lambda i, j: (i, jnp.maximum(j + row_off - n_ctx_tiles, 0), 0))],
            list(stream))


def _stream_tile(refs, n_ctx_tiles, row_off=0):
    if len(refs) == 1:
        return refs[0][...]
    return jnp.where(pl.program_id(1) + row_off < n_ctx_tiles, refs[0][...], refs[1][...])


def _even_in_kernel(*refs, n_ctx_tiles):
    (mod_ref, g_ref, w_ref, gq_ref, gkv_ref, wq_ref, wkv_ref, tq_ref, tk_ref, hg_ref, q_ref, k_ref, v_ref) = refs[-13:]
    shift = MLA_ROPE // 4
    n_samples, tm, _ = hg_ref.shape
    x = _stream_tile(refs[:-13], n_ctx_tiles)
    h = jnp.concatenate([(_rms(x[s]) * g_ref[0:1, :] * (1.0 + mod_ref[s, 1:2, :]) + mod_ref[s, 0:1, :])
                         .astype(BF16) for s in range(n_samples)], axis=0)
    p = _dot(h, w_ref[:, :EV_HGRN0])
    hg = _dot(h, w_ref[:, EV_HGRN0:]).astype(BF16)
    q_all = _dot((_rms(p[:, EV_CQ:EV_CKV]) * gq_ref[...]).astype(BF16), wq_ref[...])
    kv_all = _dot((_rms(p[:, EV_CKV:EV_KROPE]) * gkv_ref[...]).astype(BF16), wkv_ref[...])
    for s in range(n_samples):
        rows = slice(s * tm, (s + 1) * tm)
        hg_ref[s] = hg[rows]
        q, kv = q_all[rows], kv_all[rows]
        k_rope = _rope(pltpu.roll(p[rows, EV_KROPE:], MLA_NOPE, 1), tk_ref, shift)
        for cols in _lane_blocks(MLA_HEADS * LANES):
            q_ref[s, :, cols] = _rope(q[:, cols], tq_ref, shift).astype(BF16)
            k_ref[s, :, cols] = (kv[:, cols] + k_rope).astype(BF16)
        v_ref[s] = kv[:, MLA_HEADS * LANES:].astype(BF16)


def _even_in_projection(stream, t, mod, norm_g, w, w_index, gq, gkv, wq, wkv, tab_q, tab_k, layer, tm, n_ctx_tiles):
    d = w.shape[1]
    b = stream[0].shape[0] if isinstance(stream, tuple) else stream.shape[0]
    hq = MLA_HEADS * LANES
    tab = pl.BlockSpec((3, tm, LANES), lambda i, j: (0, j, 0))
    shapes = [(EV_HGRN, BF16), (hq, BF16), (hq, BF16), (MLA_HEADS * MLA_V, BF16)]
    g = MIX_GROUP if b % MIX_GROUP == 0 else 1
    stream_specs, stream_args = _stream_specs(stream, tm, n_ctx_tiles, group=g)
    return pl.pallas_call(
        functools.partial(_even_in_kernel, n_ctx_tiles=n_ctx_tiles),
        grid=(b // g, t // tm),
        in_specs=stream_specs + [_mod_spec(layer, b, n_ctx_tiles, d, group=g), _norm_spec(layer, d),
                                 _resident(w, w_index), _resident(gq), _resident(gkv), _resident(wq),
                                 _resident(wkv), tab, tab],
        out_specs=[_rows(tm, width, group=g) for width, _ in shapes],
        out_shape=[jax.ShapeDtypeStruct((b, t, width), dt) for width, dt in shapes],
        compiler_params=_params(2, VMEM_LIMIT),
        name="even_in_projection",
    )(*stream_args, mod, norm_g, w, gq, gkv, wq, wkv, tab_q, tab_k)


def _even_w_kernel(w_ref, o_ref):
    n_low = w_ref.shape[1] - EV_HGRN
    o_ref[:, :n_low] = w_ref[:, EV_HGRN:].astype(BF16)
    o_ref[:, n_low:EV_HGRN0] = jnp.zeros((o_ref.shape[0], EV_HGRN0 - n_low), BF16)
    o_ref[:, EV_HGRN0:EV_HGRN0 + HGRN_WIDTH] = (w_ref[:, :HGRN_WIDTH] * (HGRN_DIM ** -0.5)).astype(BF16)
    o_ref[:, EV_HGRN0 + HGRN_WIDTH:] = w_ref[:, HGRN_WIDTH:EV_HGRN].astype(BF16)


def _even_weights(even_w_in, tm):
    n, d, cols = even_w_in.shape
    return pl.pallas_call(
        _even_w_kernel,
        grid=(n, d // tm),
        in_specs=[pl.BlockSpec((None, tm, cols), lambda e, r: (e, r, 0))],
        out_specs=pl.BlockSpec((None, tm, EV_HGRN0 + EV_HGRN), lambda e, r: (e, r, 0)),
        out_shape=jax.ShapeDtypeStruct((n, d, EV_HGRN0 + EV_HGRN), BF16),
        compiler_params=_params(2),
        name="even_weights",
    )(even_w_in)


def _odd_in_kernel(x_ref, mod_ref, g_ref, w_ref, tr_ref, td_ref, qkr_ref, vg_ref, qd_ref, kd_ref, vd_ref):
    n_samples, tm, _ = x_ref.shape
    h = jnp.concatenate([(_rms(x_ref[s]) * g_ref[0:1, :] * (1.0 + mod_ref[s, 1:2, :]) + mod_ref[s, 0:1, :])
                         .astype(BF16) for s in range(n_samples)], axis=0)
    p_all = _dot(h, w_ref[...])
    for s in range(n_samples):
        p = p_all[s * tm:(s + 1) * tm]
        for cols in _lane_blocks(RET_QK_WIDTH):
            qkr_ref[s, :, cols] = _rope(p[:, cols], tr_ref, RET_QK // 2)
            kc = slice(cols.start + OD_KR, cols.stop + OD_KR)
            qkr_ref[s, :, kc] = _rope(p[:, kc], tr_ref, RET_QK // 2) * (RET_QK ** -0.5)
        vg_ref[s] = p[:, OD_VR:OD_QD].astype(BF16)
        for cols in _lane_blocks(DIFF_QK_WIDTH):
            qc = slice(cols.start + OD_QD, cols.stop + OD_QD)
            kc = slice(cols.start + OD_KD, cols.stop + OD_KD)
            qd_ref[s, :, cols] = (_rope(p[:, qc], td_ref, DIFF_QK // 4) * (DIFF_QK ** -0.5)).astype(BF16)
            kd_ref[s, :, cols] = _rope(p[:, kc], td_ref, DIFF_QK // 4).astype(BF16)
        vd_ref[s] = p[:, OD_VD:].astype(BF16)


def _odd_in_projection(x, mod, norm_g, w, w_index, tab_ret, tab_diff, layer, tm, n_ctx_tiles):
    b, t, d = x.shape
    tab = pl.BlockSpec((3, tm, LANES), lambda i, j: (0, j, 0))
    shapes = [(2 * RET_QK_WIDTH, F32), (2 * RET_V_WIDTH, BF16), (DIFF_QK_WIDTH, BF16), (DIFF_QK_WIDTH, BF16),
              (DIFF_V_WIDTH, BF16)]
    g = MIX_GROUP if b % MIX_GROUP == 0 else 1
    return pl.pallas_call(
        _odd_in_kernel,
        grid=(b // g, t // tm),
        in_specs=[_rows(tm, d, group=g), _mod_spec(layer, b, n_ctx_tiles, d, group=g), _norm_spec(layer, d),
                  _resident(w, w_index), tab, tab],
        out_specs=[_rows(tm, width, group=g) for width, _ in shapes],
        out_shape=[jax.ShapeDtypeStruct((b, t, width), dt) for width, dt in shapes],
        compiler_params=_params(2, VMEM_LIMIT),
        name="odd_in_projection",
    )(x, mod, norm_g, w, tab_ret, tab_diff)


_LEVELS = 7
HGRN_STEP_HEADS = 4


def _gla_masks():
    c = CHUNK
    masks = np.zeros((_LEVELS + 1, c, c), np.float32)
    masks[0] = np.eye(c)
    for lv in range(_LEVELS):
        b = (CHUNK // 2) >> lv
        for tau in range(c):
            mid = (tau // (2 * b)) * 2 * b + b
            if tau >= mid:
                masks[lv + 1, tau, mid - b:mid] = 1.0
    mk = np.stack([masks, masks[:, ::-1, ::-1]]).reshape(2, (_LEVELS + 1) * c, c)
    return jnp.asarray(mk, F32)


def _gla_gates(z, lb):
    k = (1.0 - lb) * jax.nn.sigmoid(-z)
    return k, (_log_sigmoid(z) + jnp.log1p(lb * jnp.exp(-z))) * math.log2(math.e)


SUBLANES = 8


def _scan_positions(reverse):
    sub = lax.broadcasted_iota(jnp.int32, (SUBLANES, LANES), 0)
    return (SUBLANES - 1 - sub) if reverse else sub


def _gla_tile_consts(reverse):
    pos = _scan_positions(reverse)
    consts = {("from", s): (pos >= s).astype(F32) for s in (1, 2, 4)}
    consts["right", 1] = (pos % 2 == 1).astype(F32)
    for b in (2, 4):
        consts["sign", b] = jnp.where(pos % (2 * b) >= b, 1.0, -1.0)
    return consts


def _gla_decays(g2, reverse, consts):
    n_tiles = CHUNK // SUBLANES
    pos = _scan_positions(reverse)
    tiles = [g2[v * SUBLANES:(v + 1) * SUBLANES] for v in range(n_tiles)]

    def from_earlier(x, s):
        return pltpu.roll(x, (SUBLANES - s) if reverse else s, 0)

    def at(x, p):
        j = SUBLANES - 1 - p if reverse else p
        return jnp.broadcast_to(x[j:j + 1], x.shape)

    scan = []
    for x in tiles:
        for s in (1, 2, 4):
            x = x + from_earlier(x, s) * consts["from", s]
        scan.append(x)
    order = list(reversed(range(n_tiles))) if reverse else list(range(n_tiles))
    prefix, carry = {}, None
    for v in order:
        prefix[v] = scan[v] if carry is None else scan[v] + carry
        carry = at(prefix[v], SUBLANES - 1)
    total = carry
    p_tiles = [prefix[v] for v in order]

    levels = []
    for lv in range(_LEVELS):
        b = (CHUNK // 2) >> lv
        out = []
        for n, p in enumerate(p_tiles):
            if b >= SUBLANES:
                span = 2 * b // SUBLANES
                left_last = (n // span) * span + span // 2 - 1
                edge = at(p_tiles[left_last], SUBLANES - 1)
                out.append(p - edge if n % span >= span // 2 else edge - p)
            elif b == 1:
                out.append(tiles[order[n]] * consts["right", 1])
            else:
                edge = at(p, b - 1)
                for first in range(2 * b, SUBLANES, 2 * b):
                    edge = jnp.where(pos >= first, at(p, first + b - 1), edge)
                out.append((p - edge) * consts["sign", b])
        levels.append(out)

    def rows(tile_list):
        return jnp.concatenate(list(reversed(tile_list)) if reverse else tile_list, axis=0)

    dec = [jnp.exp2(rows(t)) for t in levels]
    return dec, jnp.exp2(rows(p_tiles)), jnp.exp2(rows([total - p for p in p_tiles]))


def _sibling_select(qs, k, b, reverse):
    c = CHUNK
    if b % 8 == 0:
        right = lambda r0: (((c - b - r0) if reverse else r0) // b) % 2 == 1
        return jnp.concatenate([(qs if right(r0) else k)[r0:r0 + b] for r0 in range(0, c, b)], axis=0)
    row = lax.broadcasted_iota(jnp.int32, (c, LANES), 0)
    tau = (c - 1 - row) if reverse else row
    return jnp.where((tau & b) != 0, qs, k)


def _gla_chunk(qs, v, k, g2, masks, st, reverse, consts):
    c = CHUNK
    tile_rows = range(0, c, SUBLANES)
    levels, dec_q, dec_k = _gla_decays(g2, reverse, consts)
    att = masks[0:c] * _dot_nt(qs.astype(BF16), k.astype(BF16))
    att = [att[r0:r0 + SUBLANES] for r0 in tile_rows]
    for lv in range(_LEVELS):
        b = (CHUNK // 2) >> lv
        zl = (_sibling_select(qs, k, b, reverse) * levels[lv]).astype(BF16)
        term = _dot_nt(zl, zl)
        for n, r0 in enumerate(tile_rows):
            scan_row = (c - SUBLANES - r0) if reverse else r0
            if b < SUBLANES or (scan_row // b) % 2 == 1:
                att[n] = att[n] + masks[(lv + 1) * c + r0:(lv + 1) * c + r0 + SUBLANES] * term[r0:r0 + SUBLANES]
    att = jnp.concatenate(att, axis=0)
    vb = v.astype(BF16)
    o = _dot(att.astype(BF16), vb) + _dot_nt((qs * dec_q).astype(BF16), st.astype(BF16))
    last = 0 if reverse else c - 1
    st = st * dec_q[last:last + 1, :] + _dot_tn(vb, (k * dec_k).astype(BF16))
    return o, st


def _hgrn_kernel(qf_ref, if_ref, zf_ref, qb_ref, ib_ref, zb_ref, lbf_ref, lbb_ref, mk_ref,
                 of_ref, ob_ref, sf_ref, sb_ref, *, n_chunks):
    @pl.when(pl.program_id(2) == 0)
    def _():
        sf_ref[...] = jnp.zeros_like(sf_ref)
        sb_ref[...] = jnp.zeros_like(sb_ref)

    dirs = ((qf_ref, if_ref, zf_ref, lbf_ref, of_ref, sf_ref), (qb_ref, ib_ref, zb_ref, lbb_ref, ob_ref, sb_ref))
    n_heads = sf_ref.shape[0]
    states = {(d, h): dirs[d][5][h] for d in range(2) for h in range(n_heads)}
    consts = [_gla_tile_consts(d == 1) for d in range(2)]
    for step in range(n_chunks):
        for d, (q_ref, i_ref, z_ref, lb_ref, o_ref, _) in enumerate(dirs):
            ci = step if d == 0 else n_chunks - 1 - step
            rows = slice(ci * CHUNK, (ci + 1) * CHUNK)
            for h, cols in enumerate(_lane_blocks(n_heads * LANES)):
                k, g2 = _gla_gates(z_ref[rows, cols].astype(F32), lb_ref[:, cols])
                o, states[d, h] = _gla_chunk(q_ref[rows, cols].astype(F32), i_ref[rows, cols].astype(F32), k, g2,
                                             mk_ref[d], states[d, h], d == 1, consts[d])
                o_ref[rows, cols] = o
    for (d, h), st in states.items():
        dirs[d][5][h] = st


def _scan_block_maps(n_blocks, n_ctx_blocks):
    def bwd(j):
        return jnp.where(j < n_ctx_blocks, n_ctx_blocks - 1 - j, n_blocks - 1 - (j - n_ctx_blocks))
    return (lambda j: j), bwd


def _hgrn_scan(p, lbs, tb, n_ctx_blocks):
    b, t, _ = p.shape
    nb = t // tb
    fwd, bwd = _scan_block_maps(nb, n_ctx_blocks)
    masks = _gla_masks()

    hs = HGRN_STEP_HEADS
    width = hs * LANES
    groups = HGRN_HEADS // hs

    def col(group, order):
        return pl.BlockSpec((None, tb, width), lambda i, h, j: (i, order(j), group * groups + h))

    def lb_spec(d):
        return pl.BlockSpec((None, 1, width), lambda i, h, j: (d, 0, h))

    def out_spec(order):
        return pl.BlockSpec((None, tb, width), lambda i, h, j: (i, order(j), h))

    out = jax.ShapeDtypeStruct((b, t, HGRN_WIDTH), F32)
    state = pltpu.VMEM((hs, HGRN_DIM, HGRN_DIM), F32)
    return pl.pallas_call(
        functools.partial(_hgrn_kernel, n_chunks=tb // CHUNK),
        grid=(b, groups, nb),
        in_specs=[col(0, fwd), col(1, fwd), col(2, fwd), col(0, bwd), col(1, bwd), col(3, bwd),
                  lb_spec(0), lb_spec(1), _resident(masks)],
        out_specs=[out_spec(fwd), out_spec(bwd)],
        out_shape=[out, out],
        scratch_shapes=[state, state],
        compiler_params=_params(3),
        name="hgrn_scan",
    )(p, p, p, p, p, p, lbs, lbs, masks)


def _key_chunk(n_keys):
    return max(c for c in range(LANES, min(KEY_CHUNK, n_keys) + 1, LANES) if n_keys % c == 0)


def _score_steps(q, k_ref, k_cols, n_keys, s_ref, m_ref):
    kc = _key_chunk(n_keys)
    m = None
    for c in range(n_keys // kc):
        s = _dot_nt(q, k_ref[c * kc:(c + 1) * kc, k_cols])
        s_ref[:, c * kc:(c + 1) * kc] = s
        for blk in _lane_blocks(kc):
            m = s[:, blk] if m is None else jnp.maximum(m, s[:, blk])
        yield
    m_ref[...] = jnp.broadcast_to(jnp.max(m, axis=-1, keepdims=True), m.shape)


def _weight_steps(s_ref, m_ref, v_ref, n_keys, out):
    kc = _key_chunk(n_keys)
    m = m_ref[...]
    ones = jnp.ones((kc, LANES), BF16)
    acc = None
    for c in range(n_keys // kc):
        p = jnp.concatenate([jnp.exp(s_ref[:, c * kc + blk.start:c * kc + blk.stop] - m)
                             for blk in _lane_blocks(kc)], axis=1)
        part = _dot(p.astype(BF16), jnp.concatenate([v_ref[c * kc:(c + 1) * kc, :], ones], axis=1))
        acc = part if acc is None else acc + part
        yield
    out.append(acc)


def _interleave(*step_iters):
    live = list(step_iters)
    while live:
        for it in list(live):
            if next(it, StopIteration) is StopIteration:
                live.remove(it)


def _attention_tiles(n_streams, scores, weights, finish, tq, ctx_len, t):
    def weigh(buf, n_keys, row0, *extra):
        accs = []
        for s in range(n_streams):
            out = []
            _interleave(weights(s, buf, n_keys, out), *[e(s) for e in extra])
            accs.append(out[0])
        finish(row0, accs)

    for i in range(ctx_len // tq):
        for s in range(n_streams):
            _interleave(scores(s, i * tq, 0, ctx_len))
        weigh(0, ctx_len, i * tq)
    n_lat = (t - ctx_len) // tq
    for s in range(n_streams):
        _interleave(scores(s, ctx_len, 0, t))

    def next_scores(row0, buf):
        return lambda s: scores(s, row0, buf, t)

    def pair(i, carry):
        row0 = pl.multiple_of(ctx_len + 2 * i * tq, tq)
        weigh(0, t, row0, next_scores(row0 + tq, 1))
        weigh(1, t, row0 + tq, next_scores(row0 + 2 * tq, 0))
        return carry

    n_pairs = (n_lat - 1) // 2
    lax.fori_loop(0, n_pairs, pair, 0)
    row0 = ctx_len + 2 * n_pairs * tq
    if n_lat % 2 == 0:
        weigh(0, t, row0, next_scores(row0 + tq, 1))
        weigh(1, t, row0 + tq)
    else:
        weigh(0, t, row0)


def _tile_rows(row0, tq):
    return pl.ds(row0 if isinstance(row0, int) else pl.multiple_of(row0, tq), tq)


def _attention_scratch(n_streams, tq, t):
    scores = pltpu.VMEM((n_streams, tq, t), F32)
    row_max = pltpu.VMEM((n_streams, tq, LANES), F32)
    return [scores, scores, row_max, row_max]


def _mla_attn_kernel(q_ref, k_ref, v_ref, o_ref, s0_ref, s1_ref, m0_ref, m1_ref, *, tq, ctx_len):
    heads = _lane_blocks(2 * LANES)
    s_bufs, m_bufs = (s0_ref, s1_ref), (m0_ref, m1_ref)

    def scores(h, row0, buf, n_keys):
        return _score_steps(q_ref[_tile_rows(row0, tq), heads[h]], k_ref, heads[h], n_keys,
                            s_bufs[buf].at[h], m_bufs[buf].at[h])

    def weights(h, buf, n_keys, out):
        return _weight_steps(s_bufs[buf].at[h], m_bufs[buf].at[h], v_ref, n_keys, out)

    def finish(row0, accs):
        o = [a[:, :LANES] / a[:, LANES:] for a in accs]
        lane = lax.broadcasted_iota(jnp.int32, o[0].shape, 1)
        o_ref[_tile_rows(row0, tq), :] = jnp.where(lane < MLA_V, o[0], o[1]).astype(BF16)

    _attention_tiles(2, scores, weights, finish, tq, ctx_len, k_ref.shape[0])


def _mla_attention(q, k, v, tq, ctx_len):
    b, t, _ = q.shape
    return pl.pallas_call(
        functools.partial(_mla_attn_kernel, tq=tq, ctx_len=ctx_len),
        grid=(b, MLA_HEADS // 2),
        in_specs=[pl.BlockSpec((None, t, 2 * LANES), lambda i, h: (i, 0, h)),
                  pl.BlockSpec((None, t, 2 * LANES), lambda i, h: (i, 0, h)),
                  pl.BlockSpec((None, t, LANES), lambda i, h: (i, 0, h))],
        out_specs=pl.BlockSpec((None, t, LANES), lambda i, h: (i, 0, h)),
        out_shape=jax.ShapeDtypeStruct((b, t, MLA_HEADS * MLA_V), BF16),
        scratch_shapes=_attention_scratch(2, tq, t),
        compiler_params=_params(2, VMEM_LIMIT),
        name="mla_attention",
    )(q, k, v)


def _ret_chunk(q, k, v, decays, st, head_lanes):
    dmat, q_dec, k_dec, c_dec = decays
    qm = jnp.where(head_lanes, q, 0.0)
    att = _dot_nt(qm.astype(BF16), k.astype(BF16)) * dmat
    o = _dot(att.astype(BF16), v) + _dot_nt((qm * q_dec).astype(BF16), st.astype(BF16))
    st = st * c_dec + _dot_tn(v, (k * k_dec).astype(BF16))
    return o, st


def _ret_decays(lg, reverse):
    c = RET_CHUNK
    row = lax.broadcasted_iota(jnp.int32, (c, LANES), 0)
    tau = ((c - 1 - row) if reverse else row).astype(F32)
    ti = lax.broadcasted_iota(jnp.int32, (c, c), 0)
    si = lax.broadcasted_iota(jnp.int32, (c, c), 1)
    rel = ((si - ti) if reverse else (ti - si)).astype(F32)
    dmat = jnp.where(rel >= 0, jnp.exp(jnp.maximum(rel, 0.0) * lg[:, :1]), 0.0)
    return dmat, jnp.exp((tau + 1.0) * lg), jnp.exp((c - 1.0 - tau) * lg), jnp.exp(c * lg)


def _ret_kernel(qkf_ref, vf_ref, qkb_ref, vb_ref, lg_ref, of_ref, ob_ref, sf_ref, sb_ref, *, n_chunks):
    @pl.when(pl.program_id(1) == 0)
    def _():
        sf_ref[...] = jnp.zeros_like(sf_ref)
        sb_ref[...] = jnp.zeros_like(sb_ref)

    per_block = LANES // RET_QK
    lane = lax.broadcasted_iota(jnp.int32, (RET_CHUNK, LANES), 1)
    dirs = ((qkf_ref, vf_ref, of_ref, sf_ref), (qkb_ref, vb_ref, ob_ref, sb_ref))
    chains = [(d, h) for d in range(2) for h in range(RET_HEADS)]
    decays = {(d, h): _ret_decays(_log_sigmoid(lg_ref[d, h]), d == 1) for d, h in chains}
    states = {(d, h): dirs[d][3][h] for d, h in chains}
    for step in range(n_chunks):
        for d, h in chains:
            qk_ref, v_ref, o_ref, _ = dirs[d]
            ci = step if d == 0 else n_chunks - 1 - step
            rows = slice(ci * RET_CHUNK, (ci + 1) * RET_CHUNK)
            q_cols = _lane_blocks(2 * RET_QK_WIDTH)[h // per_block]
            k_cols = _lane_blocks(2 * RET_QK_WIDTH)[RET_QK_WIDTH // LANES + h // per_block]
            v_cols = _lane_blocks(RET_V_WIDTH)[h]
            o, states[d, h] = _ret_chunk(qk_ref[rows, q_cols], qk_ref[rows, k_cols], v_ref[rows, v_cols],
                                         decays[d, h], states[d, h], (lane // RET_QK) == (h % per_block))
            o_ref[rows, v_cols] = o
    for (d, h), st in states.items():
        dirs[d][3][h] = st


def _ret_scan(qk, vg, decay, tb, n_ctx_blocks):
    b, t, _ = qk.shape
    nb = t // tb
    fwd, bwd = _scan_block_maps(nb, n_ctx_blocks)

    def rows(width, order):
        return pl.BlockSpec((None, tb, width), lambda i, j: (i, order(j), 0))

    out = jax.ShapeDtypeStruct((b, t, RET_V_WIDTH), F32)
    state = pltpu.VMEM((RET_HEADS, RET_V, LANES), F32)
    return pl.pallas_call(
        functools.partial(_ret_kernel, n_chunks=tb // RET_CHUNK),
        grid=(b, nb),
        in_specs=[rows(2 * RET_QK_WIDTH, fwd), rows(RET_V_WIDTH, fwd), rows(2 * RET_QK_WIDTH, bwd),
                  rows(RET_V_WIDTH, bwd), _resident(decay)],
        out_specs=[rows(RET_V_WIDTH, fwd), rows(RET_V_WIDTH, bwd)],
        out_shape=[out, out],
        scratch_shapes=[state, state],
        compiler_params=_params(2),
        name="retention_scan",
    )(qk, vg, qk, vg, decay)


def _diff_attn_kernel(q_ref, k_ref, v_ref, lam_ref, g_ref, o_ref, s0_ref, s1_ref, m0_ref, m1_ref,
                      *, tq, ctx_len, lambda_init):
    lp = lam_ref[...]
    lam = (jnp.exp(jnp.sum(lp[0:1] * lp[1:2], axis=-1, keepdims=True))
           - jnp.exp(jnp.sum(lp[2:3] * lp[3:4], axis=-1, keepdims=True)) + lambda_init)

    s_bufs, m_bufs = (s0_ref, s1_ref), (m0_ref, m1_ref)
    all_lanes = slice(0, LANES)

    def scores(i, row0, buf, n_keys):
        q = q_ref[_tile_rows(row0, tq), :]
        first = lax.broadcasted_iota(jnp.int32, q.shape, 1) < DIFF_QK
        q = jnp.where(first == (i == 0), q, jnp.zeros_like(q))
        return _score_steps(q, k_ref, all_lanes, n_keys, s_bufs[buf].at[i], m_bufs[buf].at[i])

    def weights(i, buf, n_keys, out):
        return _weight_steps(s_bufs[buf].at[i], m_bufs[buf].at[i], v_ref, n_keys, out)

    def finish(row0, accs):
        o = accs[0][:, :LANES] / accs[0][:, LANES:] - lam * (accs[1][:, :LANES] / accs[1][:, LANES:])
        o_ref[_tile_rows(row0, tq), :] = (_rms(o) * g_ref[...] * (1.0 - lambda_init)).astype(BF16)

    _attention_tiles(2, scores, weights, finish, tq, ctx_len, k_ref.shape[0])


def _diff_attention(q, k, v, lam_params, subln_g, tq, ctx_len, lambda_init):
    b, t, _ = q.shape
    head = pl.BlockSpec((None, t, LANES), lambda i, h: (i, 0, h))
    return pl.pallas_call(
        functools.partial(_diff_attn_kernel, tq=tq, ctx_len=ctx_len, lambda_init=lambda_init),
        grid=(b, DIFF_HEADS),
        in_specs=[head, head, head, _resident(lam_params), _resident(subln_g)],
        out_specs=head,
        out_shape=jax.ShapeDtypeStruct((b, t, DIFF_V_WIDTH), BF16),
        scratch_shapes=_attention_scratch(2, tq, t),
        compiler_params=_params(2, VMEM_LIMIT),
        name="diff_attention",
    )(q, k, v, lam_params, subln_g)


def _mix_ffn_kernel(*refs, n_stream, n_ctx_tiles, row_off, final):
    (of_ref, ob_ref, gate_ref, mix_ref, ng_ref, mod_ref, g_ref, wo_ref, wg_ref, wu_ref, wd_ref) = refs[n_stream:n_stream + 11]
    rest = refs[n_stream + 11:]
    o_ref = rest[-1]
    n_samples, tm, wa = of_ref.shape
    x_in = _stream_tile(refs[:n_stream], n_ctx_tiles, row_off)

    def per_sample(fn):
        return jnp.concatenate([fn(s, slice(s * tm, (s + 1) * tm)) for s in range(n_samples)], axis=0)

    def read_out(s, _):
        o = of_ref[s] + ob_ref[s]
        gate = gate_ref[s].astype(F32)
        return jnp.concatenate([(_rms(o[:, cols]) * ng_ref[...] * _silu(gate[:, cols])).astype(BF16)
                                for cols in _lane_blocks(wa)], axis=1)

    u = (_dot(per_sample(read_out), wo_ref[0:wa, :])
         + _dot(per_sample(lambda s, _: mix_ref[s]), wo_ref[wa:, :]))
    x = per_sample(lambda s, rows: x_in[s] + mod_ref[s, 2:3, :] * u[rows])
    h = per_sample(lambda s, rows: (_rms(x[rows]) * g_ref[1:2, :] * (1.0 + mod_ref[s, 4:5, :])
                                    + mod_ref[s, 3:4, :]).astype(BF16))
    n_hidden = wg_ref.shape[1]
    split = (n_hidden // 2 + MXU_WIDTH - 1) // MXU_WIDTH * MXU_WIDTH
    ffn = None
    for cols in (slice(0, split), slice(split, n_hidden)):
        act = (_silu(_dot(h, wg_ref[:, cols])) * _dot(h, wu_ref[:, cols])).astype(BF16)
        part = _dot(act, wd_ref[cols, :])
        ffn = part if ffn is None else ffn + part
    for s in range(n_samples):
        rows = slice(s * tm, (s + 1) * tm)
        y = x[rows] + mod_ref[s, 5:6, :] * ffn[rows]
        o_ref[s] = _rms(y) * rest[0][...] if final else y


def _mix_ffn(stream, o_f, o_b, gates, gate_block, mix, norm_gain, mod, norm_g, w_out, wg, wu, wd, layer, tm,
             n_ctx_tiles, final_g=None):
    b, t, wa = o_f.shape
    d = w_out.shape[2]
    final = final_g is not None
    off = n_ctx_tiles if final else 0
    n_rows = t - off * tm
    g = MIX_GROUP if b % MIX_GROUP == 0 else 1
    stream_specs, stream_args = _stream_specs(stream, tm, n_ctx_tiles, off, g)
    in_specs = stream_specs + [
        _rows(tm, wa, off, g), _rows(tm, wa, off, g), _rows(tm, wa, off, g, gate_block),
        _rows(tm, mix.shape[2], off, g), _resident(norm_gain), _mod_spec(layer, b, n_ctx_tiles, d, off, g),
        _norm_spec(layer, d), _resident(w_out, layer), _resident(wg, layer), _resident(wu, layer),
        _resident(wd, layer)]
    args = stream_args + [o_f, o_b, gates, mix, norm_gain, mod, norm_g, w_out, wg, wu, wd]
    if final:
        in_specs.append(_resident(final_g))
        args.append(final_g)
    return pl.pallas_call(
        functools.partial(_mix_ffn_kernel, n_stream=len(stream_args), n_ctx_tiles=n_ctx_tiles, row_off=off,
                          final=final),
        grid=(b // g, n_rows // tm),
        in_specs=in_specs,
        out_specs=_rows(tm, d, group=g),
        out_shape=jax.ShapeDtypeStruct((b, n_rows, d), F32),
        compiler_params=_params(2, MIX_VMEM_LIMIT),
        name="mix_ffn",
    )(*args)


def _rope_tables(ctx_len, seq, kind, scale=1.0):
    f32 = np.float32
    lane = np.arange(LANES)
    n = np.arange(seq)
    if kind == "ret":
        half = RET_QK // 2
        idx = lane % RET_QK
        active = np.ones(LANES, bool)
        pos = np.broadcast_to(n[:, None], (seq, LANES))
    else:
        width = DIFF_QK if kind == "diff" else MLA_ROPE
        half = width // 4
        if kind == "diff":
            idx = lane % (width // 2)
            by_col = (lane % width) >= width // 2
            active = np.ones(LANES, bool)
        else:
            idx = (lane - MLA_NOPE) % (width // 2)
            by_col = (lane - MLA_NOPE) >= width // 2
            active = (lane >= MLA_NOPE) & (lane < MLA_NOPE + width)
        pos = np.where(by_col[None, :], (n % GRID_W)[:, None], (n // GRID_W)[:, None])
    upper = (idx >= half)[None, :]
    inv = f32(ROPE_BASE) ** (-(idx % half).astype(f32) / f32(half))
    ang = pos.astype(f32) * inv[None, :].astype(f32)
    cos = np.where(active[None, :], np.cos(ang), f32(1.0)).astype(f32)
    sin = np.where(active[None, :], np.sin(ang), f32(0.0)).astype(f32)
    lat = np.stack([cos, np.where(upper, sin, f32(0.0)), np.where(upper, f32(0.0), -sin)])
    ident = np.stack([np.ones((ctx_len, LANES), f32), np.zeros((ctx_len, LANES), f32),
                      np.zeros((ctx_len, LANES), f32)])
    return jnp.asarray(np.concatenate([ident, lat], axis=1) * f32(scale), F32)


def kernel(x, c, ctx, c_ctx, ada_w, ada_b, norm_g, mix_w_out, ffn_w_gate, ffn_w_up, ffn_w_down,
           even_w_in, hgrn_lb_logits, hgrn_norm_g, mla_q_norm_g, mla_w_uq, mla_kv_norm_g, mla_w_ukv,
           odd_w_in, ret_decay_logits, diff_lambda, diff_subln_g, final_norm_g):
    batch, seq, d = x.shape
    ctx_len = ctx.shape[1]
    depth = ada_w.shape[0]
    tm = math.gcd(ROW_TILE, math.gcd(ctx_len, seq))
    n_ctx_tiles = ctx_len // tm
    assert tm % CHUNK == 0 and tm % RET_CHUNK == 0 and seq % GRID_W == 0

    mod_rows = -(-(batch + MIX_GROUP) // 8) * 8
    s_rows = jnp.concatenate([c, jnp.broadcast_to(c_ctx[None, :], (MIX_GROUP, d)),
                              jnp.zeros((mod_rows - batch - MIX_GROUP, d), F32)], axis=0)
    mod = _modulation(s_rows, ada_w, ada_b).reshape(depth, mod_rows, 6, d)

    lb_soft = jax.nn.softmax(hgrn_lb_logits.astype(F32), axis=0)
    lbs = jnp.cumsum(lb_soft, axis=0) - lb_soft[0:1]

    tab_mla_q = _rope_tables(ctx_len, seq, "mla", (MLA_NOPE + MLA_ROPE) ** -0.5)
    tab_mla_k = _rope_tables(ctx_len, seq, "mla")
    tab_ret = _rope_tables(ctx_len, seq, "ret")
    tab_diff = _rope_tables(ctx_len, seq, "diff")

    stream = (ctx, x)
    even_w = _even_weights(even_w_in, tm)
    odd_w, w_out, w_gate, w_up, w_down = (a.astype(BF16) for a in (odd_w_in, mix_w_out, ffn_w_gate, ffn_w_up,
                                                                    ffn_w_down))
    for layer in range(depth):
        if layer % 2 == 0:
            e = layer // 2
            wq = mla_w_uq[e].reshape(MLA_Q_RANK, MLA_HEADS, MLA_NOPE + MLA_ROPE)
            wq = jnp.pad(wq, ((0, 0), (0, 0), (0, LANES - MLA_NOPE - MLA_ROPE))).reshape(MLA_Q_RANK, -1)
            wkv = mla_w_ukv[e].reshape(MLA_KV_RANK, MLA_HEADS, MLA_NOPE + MLA_V)
            wk = jnp.pad(wkv[:, :, :MLA_NOPE], ((0, 0), (0, 0), (0, LANES - MLA_NOPE))).reshape(MLA_KV_RANK, -1)
            wv = wkv[:, :, MLA_NOPE:].reshape(MLA_KV_RANK, -1)
            gates, q, k, v = _even_in_projection(
                stream, ctx_len + seq, mod, norm_g, even_w, e, mla_q_norm_g[e][None, :], mla_kv_norm_g[e][None, :],
                wq.astype(BF16),
                jnp.concatenate([wk, wv], axis=1).astype(BF16), tab_mla_q, tab_mla_k, layer, tm, n_ctx_tiles)
            o_f, o_b = _hgrn_scan(gates, lbs[e].reshape(2, 1, HGRN_WIDTH), tm, n_ctx_tiles)
            mix = _mla_attention(q, k, v, tm, ctx_len)
            gate_block, gain = 4, hgrn_norm_g[e][None, :]
        else:
            o = layer // 2
            qk_ret, gates, q_d, k_d, v_d = _odd_in_projection(stream, mod, norm_g, odd_w, o, tab_ret, tab_diff, layer,
                                                              tm, n_ctx_tiles)
            decay = jnp.broadcast_to(ret_decay_logits[o][:, :, None, None], (2, RET_HEADS, 1, LANES))
            o_f, o_b = _ret_scan(qk_ret, gates, decay, tm, n_ctx_tiles)
            lambda_init = 0.8 - 0.6 * math.exp(-0.3 * layer)
            mix = _diff_attention(q_d, k_d, v_d, diff_lambda[o], diff_subln_g[o][None, :], tm, ctx_len, lambda_init)
            gate_block, gain = 1, jnp.ones((1, LANES), F32)
        stream = _mix_ffn(stream, o_f, o_b, gates, gate_block, mix, gain, mod, norm_g, w_out, w_gate, w_up, w_down,
                          layer, tm, n_ctx_tiles, final_norm_g[None, :] if layer == depth - 1 else None)
    return stream
```
